```python
import math
import jax, jax.numpy as jnp
from jax import lax
import numpy as np

D_MODEL = 1024
BATCH = 16
SEQ = 2048
DEPTH = 2

D_MIX = D_MODEL
HGRN_HEADS = 4
HGRN_HEAD_DIM = 128
D_HGRN = HGRN_HEADS * HGRN_HEAD_DIM
HGRN_CHUNK = 64
TMLP_GROUPS = 4
TMLP_GROUP_DIM = 128
D_TMLP = TMLP_GROUPS * TMLP_GROUP_DIM
TMLP_CHUNK = 128
D_IN = 4 * D_HGRN + 2 * D_TMLP
N_EXPERT_GROUPS = 4
EXPERTS_PER_GROUP = 8
N_EXPERTS = N_EXPERT_GROUPS * EXPERTS_PER_GROUP
TOP_K_IN_GROUP = 2
D_EXPERT = 256
RMS_EPS = 1e-6
LN_EPS = 1e-5
F_FLOOR = 1e-30

kernel_name = "hybrid_hgrn2_tokenmlp_hmoe"


def rmsnorm(x, gain):
    x32 = x.astype(jnp.float32)
    y = x32 * lax.rsqrt(jnp.mean(x32 * x32, axis=-1, keepdims=True) + RMS_EPS)
    return y.astype(x.dtype) * gain


def hgrn2_mixer(q_raw, f_raw, i_raw, g_raw, lower_bound, g_norm):
    B, T, _ = q_raw.shape
    H, K, C = HGRN_HEADS, HGRN_HEAD_DIM, HGRN_CHUNK
    nC = T // C
    q = jax.nn.silu(q_raw.astype(jnp.float32))
    lb = lower_bound.astype(jnp.float32)
    f = lb + (1.0 - lb) * jax.nn.sigmoid(f_raw.astype(jnp.float32))
    log_f = jnp.log(jnp.maximum(f, F_FLOOR))
    k = 1.0 - f
    v = i_raw.astype(jnp.float32)

    def to_chunks(a):
        return a.reshape(B, nC, C, H, K).transpose(1, 0, 3, 2, 4)

    mask = jnp.tril(jnp.ones((C, C), dtype=bool))[:, :, None]

    def step(S, inp):
        qc, kc, vc, lfc = inp
        b = jnp.cumsum(lfc, axis=2)
        inter = jnp.einsum('bhtk,bhkv->bhtv', qc * jnp.exp(b), S)
        diff = b[:, :, :, None, :] - b[:, :, None, :, :]
        decay = jnp.where(mask, jnp.exp(jnp.where(mask, diff, 0.0)), 0.0)
        attn = jnp.einsum('bhtk,bhsk,bhtsk->bhts', qc, kc, decay)
        o = inter + jnp.einsum('bhts,bhsv->bhtv', attn, vc)
        b_end = b[:, :, -1, :]
        S = jnp.exp(b_end)[..., None] * S + jnp.einsum(
            'bhsk,bhsv->bhkv', kc * jnp.exp(b_end[:, :, None, :] - b), vc)
        return S, o

    S0 = jnp.zeros((B, H, K, K), jnp.float32)
    _, o = lax.scan(step, S0, (to_chunks(q), to_chunks(k), to_chunks(v), to_chunks(log_f)))
    o = o.transpose(1, 0, 3, 2, 4).reshape(B, T, H, K)
    o = rmsnorm(o, g_norm.astype(jnp.float32))
    g = jax.nn.silu(g_raw.astype(jnp.float32)).reshape(B, T, H, K)
    return (o * g).reshape(B, T, D_HGRN).astype(q_raw.dtype)


def token_mlp_mixer(u_raw, v_raw, ln_g, ln_b, w_s, b_s):
    B, T, _ = u_raw.shape
    G, Cg, P = TMLP_GROUPS, TMLP_GROUP_DIM, TMLP_CHUNK
    nC = T // P
    u = jax.nn.gelu(u_raw, approximate=False).reshape(B, nC, P, G, Cg)
    v = jax.nn.gelu(v_raw, approximate=False).reshape(B, nC, P, G, Cg)
    v32 = v.astype(jnp.float32)
    mu = jnp.mean(v32, axis=-1, keepdims=True)
    var = jnp.mean(jnp.square(v32 - mu), axis=-1, keepdims=True)
    vn = ((v32 - mu) * lax.rsqrt(var + LN_EPS)).astype(v.dtype)
    vn = vn * ln_g.reshape(G, Cg) + ln_b.reshape(G, Cg)
    w = jnp.where(jnp.tril(jnp.ones((P, P), dtype=bool))[None], w_s, 0.0)
    mixed = jnp.einsum('gts,bnsgc->bntgc', w, vn) + b_s.T[None, None, :, :, None]
    return (u * mixed).reshape(B, T, D_TMLP)


def hierarchical_moe(h, w_router_group, w_router_expert, w_gate, w_up, w_down):
    B, T, D = h.shape
    hf = h.reshape(B * T, D)
    p_group = jax.nn.softmax(jnp.einsum('nd,dg->ng', hf, w_router_group).astype(jnp.float32), axis=-1)
    g_idx = jnp.argmax(p_group, axis=-1)
    p_sel = jnp.max(p_group, axis=-1)
    e_logits = jnp.einsum('nd,de->ne', hf, w_router_expert).astype(jnp.float32)
    e_logits = e_logits.reshape(-1, N_EXPERT_GROUPS, EXPERTS_PER_GROUP)
    in_group = jnp.take_along_axis(e_logits, g_idx[:, None, None], axis=1)[:, 0, :]
    top_vals, top_idx = lax.top_k(in_group, TOP_K_IN_GROUP)
    w_top = p_sel[:, None] * jax.nn.softmax(top_vals, axis=-1)
    e_idx = g_idx[:, None] * EXPERTS_PER_GROUP + top_idx
    combine = jnp.sum(jax.nn.one_hot(e_idx, N_EXPERTS, dtype=jnp.float32) * w_top[..., None], axis=1)
    combine = combine.astype(h.dtype).reshape(B, T, N_EXPERTS)

    def per_sequence(args):
        hb, cb = args
        hid = jax.nn.silu(jnp.einsum('td,edf->tef', hb, w_gate)) * jnp.einsum('td,edf->tef', hb, w_up)
        return jnp.einsum('tef,efd->td', hid * cb[..., None], w_down)

    return lax.map(per_sequence, (h, combine))


def setup_inputs(seed: int = 0) -> dict:
    key = jax.random.key(seed)
    ks = jax.random.split(key, 20)
    f32 = jnp.float32
    nrm = lambda k, shape, scale: jax.random.normal(k, shape, f32) * scale
    gain = lambda k, shape: 1.0 + 0.05 * jax.random.normal(k, shape, f32)
    P = TMLP_CHUNK
    return {
        "x": jax.random.normal(ks[0], (BATCH, SEQ, D_MODEL), f32),
        "lb_logits": nrm(ks[1], (DEPTH, D_HGRN), 0.5),
        "norm_mix": gain(ks[2], (DEPTH, D_MODEL)),
        "w_in": nrm(ks[3], (DEPTH, D_MODEL, D_IN), D_MODEL ** -0.5),
        "hgrn_norm": gain(ks[4], (DEPTH, HGRN_HEAD_DIM)),
        "tmlp_ln_g": gain(ks[5], (DEPTH, D_TMLP)),
        "tmlp_ln_b": nrm(ks[6], (DEPTH, D_TMLP), 0.02),
        "w_spatial": nrm(ks[7], (DEPTH, TMLP_GROUPS, P, P), 0.5 * P ** -0.5),
        "b_spatial": 1.0 + nrm(ks[8], (DEPTH, TMLP_GROUPS, P), 0.05),
        "w_out": nrm(ks[9], (DEPTH, D_MIX, D_MODEL), D_MIX ** -0.5),
        "norm_ffn": gain(ks[10], (DEPTH, D_MODEL)),
        "w_router_group": nrm(ks[11], (DEPTH, D_MODEL, N_EXPERT_GROUPS), D_MODEL ** -0.5),
        "w_router_expert": nrm(ks[12], (DEPTH, D_MODEL, N_EXPERTS), D_MODEL ** -0.5),
        "w_gate": nrm(ks[13], (DEPTH, N_EXPERTS, D_MODEL, D_EXPERT), D_MODEL ** -0.5),
        "w_up": nrm(ks[14], (DEPTH, N_EXPERTS, D_MODEL, D_EXPERT), D_MODEL ** -0.5),
        "w_down": nrm(ks[15], (DEPTH, N_EXPERTS, D_EXPERT, D_MODEL), D_EXPERT ** -0.5),
        "norm_final": gain(ks[16], (D_MODEL,)),
    }


def reference(x, lb_logits, norm_mix, w_in, hgrn_norm, tmlp_ln_g, tmlp_ln_b, w_spatial, b_spatial,
              w_out, norm_ffn, w_router_group, w_router_expert, w_gate, w_up, w_down, norm_final):
    p = jax.nn.softmax(lb_logits.astype(jnp.float32), axis=0)
    lower_bounds = jnp.cumsum(p, axis=0) - p[0:1]
    splits = [D_HGRN, 2 * D_HGRN, 3 * D_HGRN, 4 * D_HGRN, 4 * D_HGRN + D_TMLP]
    for layer in range(DEPTH):
        h = rmsnorm(x, norm_mix[layer])
        z = jnp.einsum('btd,de->bte', h, w_in[layer])
        q_r, f_r, i_r, g_r, u_r, v_r = jnp.split(z, splits, axis=-1)
        y_hgrn = hgrn2_mixer(q_r, f_r, i_r, g_r, lower_bounds[layer], hgrn_norm[layer])
        y_tmlp = token_mlp_mixer(u_r, v_r, tmlp_ln_g[layer], tmlp_ln_b[layer],
                                 w_spatial[layer], b_spatial[layer])
        y = jnp.concatenate([y_hgrn, y_tmlp], axis=-1)
        x = x + jnp.einsum('bte,ed->btd', y, w_out[layer])
        h = rmsnorm(x, norm_ffn[layer])
        x = x + hierarchical_moe(h, w_router_group[layer], w_router_expert[layer],
                                 w_gate[layer], w_up[layer], w_down[layer])
    return rmsnorm(x, norm_final)
```

```python
import functools

import jax
import jax.numpy as jnp
from jax import lax
from jax.experimental import pallas as pl
from jax.experimental.pallas import tpu as pltpu

F32 = jnp.float32
BF16 = jnp.bfloat16

D_MODEL = 1024
N_HEADS = 4
HEAD_DIM = 128
D_HGRN = N_HEADS * HEAD_DIM
N_GROUPS = 4
GROUP_DIM = 128
D_TMLP = N_GROUPS * GROUP_DIM
D_IN = 4 * D_HGRN + 2 * D_TMLP
SUB = 128
N_EXPERT_GROUPS = 4
EXPERTS_PER_GROUP = 8
N_EXPERTS = N_EXPERT_GROUPS * EXPERTS_PER_GROUP
D_EXPERT = 256
ROUTER_LANES = 128
RMS_EPS = 1e-6
LN_EPS = 1e-5
F_FLOOR = 1e-30
SQRT_HALF = 0.7071067811865476

MIXER_ROWS = 512
MOE_ROWS = 1024
VMEM_LIMIT_BYTES = 56 * 1024 * 1024


def _dot(a, b):
    return jnp.dot(a, b, preferred_element_type=F32)


def _dot_nt(a, b):
    return lax.dot_general(a, b, (((1,), (1,)), ((), ())), preferred_element_type=F32)


def _dot_tn(a, b):
    return lax.dot_general(a, b, (((0,), (0,)), ((), ())), preferred_element_type=F32)


def _split_bf16(a):
    hi = a.astype(BF16)
    lo = (a - hi.astype(F32)).astype(BF16)
    return hi, lo


def _gelu(a):
    return 0.5 * a * (1.0 + lax.erf(a * SQRT_HALF))


def _silu(a):
    return a * jax.nn.sigmoid(a)


def _boundary_rows(b_ref, m, width):
    pieces = []
    if 2 * m >= 8:
        for s0 in range(0, SUB, 2 * m):
            row = b_ref[pl.ds(s0 + m - 1, 1), :]
            pieces.append(jnp.broadcast_to(row, (2 * m, width)))
    else:
        row8 = lax.broadcasted_iota(jnp.int32, (8, width), 0)
        for g0 in range(0, SUB, 8):
            acc = None
            for s0 in range(0, 8, 2 * m):
                row = jnp.broadcast_to(b_ref[pl.ds(g0 + s0 + m - 1, 1), :], (8, width))
                acc = row if acc is None else jnp.where(row8 >= s0, row, acc)
            pieces.append(acc)
    return jnp.concatenate(pieces, axis=0)


def _hgrn_attention(q, kk, b, b_ref):
    t_idx = lax.broadcasted_iota(jnp.int32, (SUB, SUB), 0)
    s_idx = lax.broadcasted_iota(jnp.int32, (SUB, SUB), 1)
    tx = jnp.bitwise_xor(t_idx, s_idx)
    row = lax.broadcasted_iota(jnp.int32, (SUB, D_HGRN), 0)

    qb = q.astype(BF16)
    kb = kk.astype(BF16)
    att = []
    for h in range(N_HEADS):
        sl = slice(h * HEAD_DIM, (h + 1) * HEAD_DIM)
        att.append(jnp.where(tx == 0, _dot_nt(qb[:, sl], kb[:, sl]), 0.0))
    m = 1
    while m < SUB:
        ref_pt = _boundary_rows(b_ref, m, D_HGRN)
        decay = jnp.exp(-jnp.abs(b - ref_pt))
        right = jnp.bitwise_and(row, m) != 0
        qt = jnp.where(right, q * decay, 0.0).astype(BF16)
        kt = jnp.where(right, 0.0, kk * decay).astype(BF16)
        same_block = tx < 2 * m
        for h in range(N_HEADS):
            sl = slice(h * HEAD_DIM, (h + 1) * HEAD_DIM)
            att[h] = att[h] + jnp.where(same_block, _dot_nt(qt[:, sl], kt[:, sl]), 0.0)
        m *= 2
    return att


def _mixer_kernel(x_ref, gain_ref, win_ref, lb_ref, hn_ref, lng_ref, lnb_ref, wsp_ref, bsp_ref,
                  wout_ref, o_ref, z_ref, y_ref, b_ref, st_ref):
    @pl.when(pl.program_id(1) == 0)
    def _():
        st_ref[...] = jnp.zeros_like(st_ref)

    x = x_ref[0]
    ms = jnp.mean(x * x, axis=-1, keepdims=True)
    h = (x * lax.rsqrt(ms + RMS_EPS)) * gain_ref[...]
    z_ref[...] = _dot(h.astype(BF16), win_ref[...])

    t_idx = lax.broadcasted_iota(jnp.int32, (SUB, SUB), 0)
    s_idx = lax.broadcasted_iota(jnp.int32, (SUB, SUB), 1)
    tri = (s_idx <= t_idx).astype(BF16)

    def sub_chunk(c, carry):
        rows = pl.ds(pl.multiple_of(c * SUB, SUB), SUB)

        q = _silu(z_ref[rows, 0:D_HGRN])
        lb = lb_ref[...]
        fg = lb + (1.0 - lb) * jax.nn.sigmoid(z_ref[rows, D_HGRN:2 * D_HGRN])
        lf = jnp.log(jnp.maximum(fg, F_FLOOR))
        kk = 1.0 - fg
        v = z_ref[rows, 2 * D_HGRN:3 * D_HGRN].astype(BF16)
        lf_hi, lf_lo = _split_bf16(lf)
        b = _dot(tri, lf_hi) + _dot(tri, lf_lo)
        b_ref[...] = b
        att = _hgrn_attention(q, kk, b, b_ref)
        b_end = b_ref[pl.ds(SUB - 1, 1), :]
        q0 = (q * jnp.exp(b)).astype(BF16)
        k_end = (kk * jnp.exp(b_end - b)).astype(BF16)
        s_decay = jnp.exp(b_end)
        g = _silu(z_ref[rows, 3 * D_HGRN:4 * D_HGRN])
        for hd in range(N_HEADS):
            sl = slice(hd * HEAD_DIM, (hd + 1) * HEAD_DIM)
            st = st_ref[hd]
            o = _dot(att[hd].astype(BF16), v[:, sl]) + _dot_nt(q0[:, sl], st.astype(BF16))
            st_ref[hd] = st * s_decay[:, sl] + _dot_tn(v[:, sl], k_end[:, sl])
            oms = jnp.mean(o * o, axis=-1, keepdims=True)
            on = (o * lax.rsqrt(oms + RMS_EPS)) * hn_ref[...]
            y_ref[rows, sl] = (on * g[:, sl]).astype(BF16)

        u = _gelu(z_ref[rows, 4 * D_HGRN:4 * D_HGRN + D_TMLP])
        vv = _gelu(z_ref[rows, 4 * D_HGRN + D_TMLP:D_IN])
        for gi in range(N_GROUPS):
            sl = slice(gi * GROUP_DIM, (gi + 1) * GROUP_DIM)
            vg = vv[:, sl]
            mu = jnp.mean(vg, axis=-1, keepdims=True)
            cen = vg - mu
            var = jnp.mean(cen * cen, axis=-1, keepdims=True)
            vn = (cen * lax.rsqrt(var + LN_EPS)) * lng_ref[:, sl] + lnb_ref[:, sl]
            mixed = _dot(wsp_ref[gi], vn.astype(BF16)) + bsp_ref[:, sl]
            y_ref[rows, D_HGRN + gi * GROUP_DIM:D_HGRN + (gi + 1) * GROUP_DIM] = (
                u[:, sl] * mixed).astype(BF16)
        return carry

    lax.fori_loop(0, MIXER_ROWS // SUB, sub_chunk, 0)
    o_ref[0] = x + _dot(y_ref[...], wout_ref[...])


def _mixer(x, gain, w_in, lb, hgrn_norm, ln_g, ln_b, w_sp, b_sp, w_out):
    bsz, seq, _ = x.shape
    assert seq % MIXER_ROWS == 0 and MIXER_ROWS % SUB == 0
    const2 = lambda b, j: (0, 0)
    const3 = lambda b, j: (0, 0, 0)
    return pl.pallas_call(
        _mixer_kernel,
        grid=(bsz, seq // MIXER_ROWS),
        in_specs=[
            pl.BlockSpec((1, MIXER_ROWS, D_MODEL), lambda b, j: (b, j, 0)),
            pl.BlockSpec((1, D_MODEL), const2),
            pl.BlockSpec((D_MODEL, D_IN), const2),
            pl.BlockSpec((1, D_HGRN), const2),
            pl.BlockSpec((1, HEAD_DIM), const2),
            pl.BlockSpec((1, D_TMLP), const2),
            pl.BlockSpec((1, D_TMLP), const2),
            pl.BlockSpec((N_GROUPS, SUB, SUB), const3),
            pl.BlockSpec((SUB, D_TMLP), const2),
            pl.BlockSpec((D_HGRN + D_TMLP, D_MODEL), const2),
        ],
        out_specs=pl.BlockSpec((1, MIXER_ROWS, D_MODEL), lambda b, j: (b, j, 0)),
        out_shape=jax.ShapeDtypeStruct(x.shape, F32),
        scratch_shapes=[
            pltpu.VMEM((MIXER_ROWS, D_IN), F32),
            pltpu.VMEM((MIXER_ROWS, D_HGRN + D_TMLP), BF16),
            pltpu.VMEM((SUB, D_HGRN), F32),
            pltpu.VMEM((N_HEADS, HEAD_DIM, HEAD_DIM), F32),
        ],
        compiler_params=pltpu.CompilerParams(
            dimension_semantics=("arbitrary", "arbitrary"),
            vmem_limit_bytes=VMEM_LIMIT_BYTES),
        name="mixer",
    )(x, gain, w_in, lb, hgrn_norm, ln_g, ln_b, w_sp, b_sp, w_out)


def _route(h, wr_hi_ref, wr_lo_ref):
    h_hi, h_lo = _split_bf16(h)
    logits = _dot(h_hi, wr_hi_ref[...]) + (_dot(h_hi, wr_lo_ref[...]) + _dot(h_lo, wr_hi_ref[...]))
    lane = lax.broadcasted_iota(jnp.int32, logits.shape, 1)
    neg = jnp.float32(-jnp.inf)
    big = jnp.int32(1 << 20)
    is_group = (lane >= N_EXPERTS) & (lane < N_EXPERTS + N_EXPERT_GROUPS)
    gl = jnp.where(is_group, logits, neg)
    gmax = jnp.max(gl, axis=-1, keepdims=True)
    p_sel = 1.0 / jnp.sum(jnp.exp(gl - gmax), axis=-1, keepdims=True)
    g_idx = jnp.min(jnp.where(gl == gmax, lane - N_EXPERTS, big), axis=-1, keepdims=True)
    in_group = (lane < N_EXPERTS) & (jnp.right_shift(lane, 3) == g_idx)
    el = jnp.where(in_group, logits, neg)
    v1 = jnp.max(el, axis=-1, keepdims=True)
    i1 = jnp.min(jnp.where(el == v1, lane, big), axis=-1, keepdims=True)
    el2 = jnp.where(lane == i1, neg, el)
    v2 = jnp.max(el2, axis=-1, keepdims=True)
    i2 = jnp.min(jnp.where(el2 == v2, lane, big), axis=-1, keepdims=True)
    e2 = jnp.exp(v2 - v1)
    w1 = p_sel / (1.0 + e2)
    w2 = p_sel * e2 / (1.0 + e2)
    return jnp.where(lane == i1, w1, 0.0) + jnp.where(lane == i2, w2, 0.0)


def _moe_kernel(x_ref, gain_ref, wr_hi_ref, wr_lo_ref, wg_ref, wu_ref, wd_ref, gfin_ref, o_ref,
                h_ref, comb_ref, acc_ref, *, final_norm):
    e = pl.program_id(1)

    @pl.when(e == 0)
    def _():
        x = x_ref[...]
        ms = jnp.mean(x * x, axis=-1, keepdims=True)
        h = (x * lax.rsqrt(ms + RMS_EPS)) * gain_ref[...]
        h_ref[...] = h.astype(BF16)
        comb_ref[...] = _route(h, wr_hi_ref, wr_lo_ref)
        acc_ref[...] = jnp.zeros_like(acc_ref)

    hb = h_ref[...]
    lane = lax.broadcasted_iota(jnp.int32, comb_ref.shape, 1)
    c_e = jnp.sum(jnp.where(lane == e, comb_ref[...], 0.0), axis=-1, keepdims=True)
    hid = _silu(_dot(hb, wg_ref[0])) * _dot(hb, wu_ref[0])
    acc_ref[...] += _dot((hid * c_e).astype(BF16), wd_ref[0])

    @pl.when(e == N_EXPERTS - 1)
    def _():
        y = x_ref[...] + acc_ref[...]
        if final_norm:
            ms = jnp.mean(y * y, axis=-1, keepdims=True)
            y = (y * lax.rsqrt(ms + RMS_EPS)) * gfin_ref[...]
        o_ref[...] = y


def _moe(x2d, gain, wr_hi, wr_lo, w_gate, w_up, w_down, gain_final, final_norm):
    n = x2d.shape[0]
    assert n % MOE_ROWS == 0
    const2 = lambda i, e: (0, 0)
    return pl.pallas_call(
        functools.partial(_moe_kernel, final_norm=final_norm),
        grid=(n // MOE_ROWS, N_EXPERTS),
        in_specs=[
            pl.BlockSpec((MOE_ROWS, D_MODEL), lambda i, e: (i, 0)),
            pl.BlockSpec((1, D_MODEL), const2),
            pl.BlockSpec((D_MODEL, ROUTER_LANES), const2),
            pl.BlockSpec((D_MODEL, ROUTER_LANES), const2),
            pl.BlockSpec((1, D_MODEL, D_EXPERT), lambda i, e: (e, 0, 0)),
            pl.BlockSpec((1, D_MODEL, D_EXPERT), lambda i, e: (e, 0, 0)),
            pl.BlockSpec((1, D_EXPERT, D_MODEL), lambda i, e: (e, 0, 0)),
            pl.BlockSpec((1, D_MODEL), const2),
        ],
        out_specs=pl.BlockSpec((MOE_ROWS, D_MODEL), lambda i, e: (i, 0)),
        out_shape=jax.ShapeDtypeStruct(x2d.shape, F32),
        scratch_shapes=[
            pltpu.VMEM((MOE_ROWS, D_MODEL), BF16),
            pltpu.VMEM((MOE_ROWS, ROUTER_LANES), F32),
            pltpu.VMEM((MOE_ROWS, D_MODEL), F32),
        ],
        compiler_params=pltpu.CompilerParams(
            dimension_semantics=("arbitrary", "arbitrary"),
            vmem_limit_bytes=VMEM_LIMIT_BYTES),
        name="moe",
    )(x2d, gain, wr_hi, wr_lo, w_gate, w_up, w_down, gain_final)


def kernel(x, lb_logits, norm_mix, w_in, hgrn_norm, tmlp_ln_g, tmlp_ln_b, w_spatial, b_spatial,
           w_out, norm_ffn, w_router_group, w_router_expert, w_gate, w_up, w_down, norm_final):
    depth = w_in.shape[0]
    bsz, seq, _ = x.shape
    p = jax.nn.softmax(lb_logits.astype(F32), axis=0)
    lower_bounds = jnp.cumsum(p, axis=0) - p[0:1]
    tril = jnp.tril(jnp.ones((SUB, SUB), dtype=bool))
    for layer in range(depth):
        w_sp = jnp.where(tril[None], w_spatial[layer], 0.0).astype(BF16)
        b_sp = jnp.repeat(b_spatial[layer].T, GROUP_DIM, axis=1)
        x = _mixer(x, norm_mix[layer][None], w_in[layer].astype(BF16), lower_bounds[layer][None],
                   hgrn_norm[layer][None], tmlp_ln_g[layer][None], tmlp_ln_b[layer][None],
                   w_sp, b_sp, w_out[layer].astype(BF16))
        w_r = jnp.concatenate([w_router_expert[layer], w_router_group[layer]], axis=1)
        w_r = jnp.pad(w_r, ((0, 0), (0, ROUTER_LANES - w_r.shape[1])))
        wr_hi, wr_lo = _split_bf16(w_r)
        x2d = _moe(x.reshape(bsz * seq, D_MODEL), norm_ffn[layer][None], wr_hi, wr_lo,
                   w_gate[layer].astype(BF16), w_up[layer].astype(BF16), w_down[layer].astype(BF16),
                   norm_final[None], final_norm=(layer == depth - 1))
        x = x2d.reshape(bsz, seq, D_MODEL)
    return x
```

```python
import functools

import jax
import jax.numpy as jnp
from jax import lax
from jax.experimental import pallas as pl
from jax.experimental.pallas import tpu as pltpu

F32 = jnp.float32
BF16 = jnp.bfloat16

D_MODEL = 1024
N_HEADS = 4
HEAD_DIM = 128
D_HGRN = N_HEADS * HEAD_DIM
N_GROUPS = 4
GROUP_DIM = 128
D_TMLP = N_GROUPS * GROUP_DIM
D_IN = 4 * D_HGRN + 2 * D_TMLP
SUB = 128
N_EXPERT_GROUPS = 4
EXPERTS_PER_GROUP = 8
N_EXPERTS = N_EXPERT_GROUPS * EXPERTS_PER_GROUP
D_EXPERT = 256
ROUTER_LANES = 128
RMS_EPS = 1e-6
LN_EPS = 1e-5
F_FLOOR = 1e-30
SQRT_HALF = 0.7071067811865476

MIXER_ROWS = 512
MOE_ROWS = 1024
MOE_TILE_SHIFT = 7
MOE_TILE = 1 << MOE_TILE_SHIFT
MOE_SORTED = MOE_ROWS + N_EXPERT_GROUPS * MOE_TILE
VMEM_LIMIT_BYTES = 56 * 1024 * 1024


def _dot(a, b):
    return jnp.dot(a, b, preferred_element_type=F32)


def _dot_nt(a, b):
    return lax.dot_general(a, b, (((1,), (1,)), ((), ())), preferred_element_type=F32)


def _dot_tn(a, b):
    return lax.dot_general(a, b, (((0,), (0,)), ((), ())), preferred_element_type=F32)


def _split_bf16(a):
    hi = a.astype(BF16)
    lo = (a - hi.astype(F32)).astype(BF16)
    return hi, lo


def _gelu(a):
    return 0.5 * a * (1.0 + lax.erf(a * SQRT_HALF))


def _silu(a):
    return a * jax.nn.sigmoid(a)


def _boundary_rows(b_ref, m, width):
    pieces = []
    if 2 * m >= 8:
        for s0 in range(0, SUB, 2 * m):
            row = b_ref[pl.ds(s0 + m - 1, 1), :]
            pieces.append(jnp.broadcast_to(row, (2 * m, width)))
    else:
        row8 = lax.broadcasted_iota(jnp.int32, (8, width), 0)
        for g0 in range(0, SUB, 8):
            acc = None
            for s0 in range(0, 8, 2 * m):
                row = jnp.broadcast_to(b_ref[pl.ds(g0 + s0 + m - 1, 1), :], (8, width))
                acc = row if acc is None else jnp.where(row8 >= s0, row, acc)
            pieces.append(acc)
    return jnp.concatenate(pieces, axis=0)


def _hgrn_attention(q, kk, b, b_ref):
    t_idx = lax.broadcasted_iota(jnp.int32, (SUB, SUB), 0)
    s_idx = lax.broadcasted_iota(jnp.int32, (SUB, SUB), 1)
    tx = jnp.bitwise_xor(t_idx, s_idx)
    row = lax.broadcasted_iota(jnp.int32, (SUB, D_HGRN), 0)

    qb = q.astype(BF16)
    kb = kk.astype(BF16)
    att = []
    for h in range(N_HEADS):
        sl = slice(h * HEAD_DIM, (h + 1) * HEAD_DIM)
        att.append(jnp.where(tx == 0, _dot_nt(qb[:, sl], kb[:, sl]), 0.0))
    m = 1
    while m < SUB:
        ref_pt = _boundary_rows(b_ref, m, D_HGRN)
        decay = jnp.exp(-jnp.abs(b - ref_pt))
        right = jnp.bitwise_and(row, m) != 0
        qt = jnp.where(right, q * decay, 0.0).astype(BF16)
        kt = jnp.where(right, 0.0, kk * decay).astype(BF16)
        same_block = tx < 2 * m
        for h in range(N_HEADS):
            sl = slice(h * HEAD_DIM, (h + 1) * HEAD_DIM)
            att[h] = att[h] + jnp.where(same_block, _dot_nt(qt[:, sl], kt[:, sl]), 0.0)
        m *= 2
    return att


def _mixer_kernel(x_ref, gain_ref, win_ref, lb_ref, hn_ref, lng_ref, lnb_ref, wsp_ref, bsp_ref,
                  wout_ref, o_ref, z_ref, y_ref, b_ref, st_ref):
    @pl.when(pl.program_id(1) == 0)
    def _():
        st_ref[...] = jnp.zeros_like(st_ref)

    x = x_ref[0]
    ms = jnp.mean(x * x, axis=-1, keepdims=True)
    h = (x * lax.rsqrt(ms + RMS_EPS)) * gain_ref[...]
    z_ref[...] = _dot(h.astype(BF16), win_ref[...])

    t_idx = lax.broadcasted_iota(jnp.int32, (SUB, SUB), 0)
    s_idx = lax.broadcasted_iota(jnp.int32, (SUB, SUB), 1)
    tri = (s_idx <= t_idx).astype(BF16)

    def sub_chunk(c, carry):
        rows = pl.ds(pl.multiple_of(c * SUB, SUB), SUB)

        q = _silu(z_ref[rows, 0:D_HGRN])
        lb = lb_ref[...]
        fg = lb + (1.0 - lb) * jax.nn.sigmoid(z_ref[rows, D_HGRN:2 * D_HGRN])
        lf = jnp.log(jnp.maximum(fg, F_FLOOR))
        kk = 1.0 - fg
        v = z_ref[rows, 2 * D_HGRN:3 * D_HGRN].astype(BF16)
        lf_hi, lf_lo = _split_bf16(lf)
        b = _dot(tri, lf_hi) + _dot(tri, lf_lo)
        b_ref[...] = b
        att = _hgrn_attention(q, kk, b, b_ref)
        b_end = b_ref[pl.ds(SUB - 1, 1), :]
        q0 = (q * jnp.exp(b)).astype(BF16)
        k_end = (kk * jnp.exp(b_end - b)).astype(BF16)
        s_decay = jnp.exp(b_end)
        g = _silu(z_ref[rows, 3 * D_HGRN:4 * D_HGRN])
        for hd in range(N_HEADS):
            sl = slice(hd * HEAD_DIM, (hd + 1) * HEAD_DIM)
            st = st_ref[hd]
            o = _dot(att[hd].astype(BF16), v[:, sl]) + _dot_nt(q0[:, sl], st.astype(BF16))
            st_ref[hd] = st * s_decay[:, sl] + _dot_tn(v[:, sl], k_end[:, sl])
            oms = jnp.mean(o * o, axis=-1, keepdims=True)
            on = (o * lax.rsqrt(oms + RMS_EPS)) * hn_ref[...]
            y_ref[rows, sl] = (on * g[:, sl]).astype(BF16)

        u = _gelu(z_ref[rows, 4 * D_HGRN:4 * D_HGRN + D_TMLP])
        vv = _gelu(z_ref[rows, 4 * D_HGRN + D_TMLP:D_IN])
        for gi in range(N_GROUPS):
            sl = slice(gi * GROUP_DIM, (gi + 1) * GROUP_DIM)
            vg = vv[:, sl]
            mu = jnp.mean(vg, axis=-1, keepdims=True)
            cen = vg - mu
            var = jnp.mean(cen * cen, axis=-1, keepdims=True)
            vn = (cen * lax.rsqrt(var + LN_EPS)) * lng_ref[:, sl] + lnb_ref[:, sl]
            mixed = _dot(wsp_ref[gi], vn.astype(BF16)) + bsp_ref[:, sl]
            y_ref[rows, D_HGRN + gi * GROUP_DIM:D_HGRN + (gi + 1) * GROUP_DIM] = (
                u[:, sl] * mixed).astype(BF16)
        return carry

    lax.fori_loop(0, MIXER_ROWS // SUB, sub_chunk, 0)
    o_ref[0] = x + _dot(y_ref[...], wout_ref[...])


def _mixer(x, gain, w_in, lb, hgrn_norm, ln_g, ln_b, w_sp, b_sp, w_out):
    bsz, seq, _ = x.shape
    assert seq % MIXER_ROWS == 0 and MIXER_ROWS % SUB == 0
    const2 = lambda b, j: (0, 0)
    const3 = lambda b, j: (0, 0, 0)
    return pl.pallas_call(
        _mixer_kernel,
        grid=(bsz, seq // MIXER_ROWS),
        in_specs=[
            pl.BlockSpec((1, MIXER_ROWS, D_MODEL), lambda b, j: (b, j, 0)),
            pl.BlockSpec((1, D_MODEL), const2),
            pl.BlockSpec((D_MODEL, D_IN), const2),
            pl.BlockSpec((1, D_HGRN), const2),
            pl.BlockSpec((1, HEAD_DIM), const2),
            pl.BlockSpec((1, D_TMLP), const2),
            pl.BlockSpec((1, D_TMLP), const2),
            pl.BlockSpec((N_GROUPS, SUB, SUB), const3),
            pl.BlockSpec((SUB, D_TMLP), const2),
            pl.BlockSpec((D_HGRN + D_TMLP, D_MODEL), const2),
        ],
        out_specs=pl.BlockSpec((1, MIXER_ROWS, D_MODEL), lambda b, j: (b, j, 0)),
        out_shape=jax.ShapeDtypeStruct(x.shape, F32),
        scratch_shapes=[
            pltpu.VMEM((MIXER_ROWS, D_IN), F32),
            pltpu.VMEM((MIXER_ROWS, D_HGRN + D_TMLP), BF16),
            pltpu.VMEM((SUB, D_HGRN), F32),
            pltpu.VMEM((N_HEADS, HEAD_DIM, HEAD_DIM), F32),
        ],
        compiler_params=pltpu.CompilerParams(
            dimension_semantics=("arbitrary", "arbitrary"),
            vmem_limit_bytes=VMEM_LIMIT_BYTES),
        name="mixer",
    )(x, gain, w_in, lb, hgrn_norm, ln_g, ln_b, w_sp, b_sp, w_out)


def _route(h, wr_hi_ref, wr_lo_ref):
    h_hi, h_lo = _split_bf16(h)
    logits = _dot(h_hi, wr_hi_ref[...]) + (_dot(h_hi, wr_lo_ref[...]) + _dot(h_lo, wr_hi_ref[...]))
    lane = lax.broadcasted_iota(jnp.int32, logits.shape, 1)
    neg = jnp.float32(-jnp.inf)
    big = jnp.int32(1 << 20)
    is_group = (lane >= N_EXPERTS) & (lane < N_EXPERTS + N_EXPERT_GROUPS)
    gl = jnp.where(is_group, logits, neg)
    gmax = jnp.max(gl, axis=-1, keepdims=True)
    p_sel = 1.0 / jnp.sum(jnp.exp(gl - gmax), axis=-1, keepdims=True)
    g_idx = jnp.min(jnp.where(gl == gmax, lane - N_EXPERTS, big), axis=-1, keepdims=True)
    in_group = (lane < N_EXPERTS) & (jnp.right_shift(lane, 3) == g_idx)
    el = jnp.where(in_group, logits, neg)
    v1 = jnp.max(el, axis=-1, keepdims=True)
    i1 = jnp.min(jnp.where(el == v1, lane, big), axis=-1, keepdims=True)
    el2 = jnp.where(lane == i1, neg, el)
    v2 = jnp.max(el2, axis=-1, keepdims=True)
    i2 = jnp.min(jnp.where(el2 == v2, lane, big), axis=-1, keepdims=True)
    e2 = jnp.exp(v2 - v1)
    w1 = p_sel / (1.0 + e2)
    w2 = p_sel * e2 / (1.0 + e2)
    comb = jnp.where(lane == i1, w1, 0.0) + jnp.where(lane == i2, w2, 0.0)
    return comb, g_idx


def _moe_kernel(x_ref, gain_ref, wr_hi_ref, wr_lo_ref, tri_ref, wg_ref, wu_ref, wd_ref, gfin_ref,
                o_ref, hs_ref, cs_ref, ys_ref, pos_ref, tile0_ref, ntile_ref, *, final_norm):
    e = pl.program_id(1)

    @pl.when(e == 0)
    def _():
        x = x_ref[...]
        ms = jnp.mean(x * x, axis=-1, keepdims=True)
        h = (x * lax.rsqrt(ms + RMS_EPS)) * gain_ref[...]
        comb, g_idx = _route(h, wr_hi_ref, wr_lo_ref)
        lane = lax.broadcasted_iota(jnp.int32, comb.shape, 1)
        onehot = lane == g_idx
        prefix = _dot(tri_ref[...], onehot.astype(F32).astype(BF16))
        rank = jnp.sum(jnp.where(onehot, prefix, 0.0), axis=-1, keepdims=True)
        start = jnp.int32(0)
        base = jnp.zeros_like(rank)
        for g in range(N_EXPERT_GROUPS):
            cnt = jnp.sum((g_idx == g).astype(jnp.int32))
            nt = lax.shift_right_logical(cnt + (MOE_TILE - 1), MOE_TILE_SHIFT)
            tile0_ref[g] = start
            ntile_ref[g] = nt
            base = jnp.where(g_idx == g, (start * MOE_TILE).astype(F32), base)
            start = start + nt
        pos = (base + rank).astype(jnp.int32)
        pos_ref[...] = jnp.broadcast_to(pos, pos_ref.shape)
        halves = jnp.where(lane == 0, jnp.right_shift(pos, 6),
                           jnp.where(lane == 1, jnp.bitwise_and(pos, 63), 0))
        pick = (lax.broadcasted_iota(jnp.int32, (8, ROUTER_LANES), 0)
                == lax.broadcasted_iota(jnp.int32, (8, ROUTER_LANES), 1))
        halves_t = _dot_nt(pick.astype(F32).astype(BF16), halves.astype(F32).astype(BF16))
        pos_row = (halves_t[0:1] * 64.0 + halves_t[1:2]).astype(jnp.int32)
        perm = (lax.broadcasted_iota(jnp.int32, (MOE_SORTED, MOE_ROWS), 0) == pos_row)
        perm = perm.astype(F32).astype(BF16)
        hs_ref[...] = _dot(perm, h.astype(BF16)).astype(BF16)
        c_hi = comb.astype(BF16)
        c_res = comb - c_hi.astype(F32)
        c_mid = c_res.astype(BF16)
        c_lo = (c_res - c_mid.astype(F32)).astype(BF16)
        cs_ref[...] = _dot(perm, c_hi) + (_dot(perm, c_mid) + _dot(perm, c_lo))
        ys_ref[...] = jnp.zeros_like(ys_ref)

    g = lax.shift_right_logical(e, 3)
    t0 = tile0_ref[g]
    lane_t = lax.broadcasted_iota(jnp.int32, (MOE_TILE, ROUTER_LANES), 1)

    def tile_body(k, carry):
        rows = pl.ds(pl.multiple_of((t0 + k) * MOE_TILE, MOE_TILE), MOE_TILE)
        hh = hs_ref[rows, :]
        c_e = jnp.sum(jnp.where(lane_t == e, cs_ref[rows, :], 0.0), axis=-1, keepdims=True)
        hid = _silu(_dot(hh, wg_ref[0])) * _dot(hh, wu_ref[0])
        ys_ref[rows, :] += _dot((hid * c_e).astype(BF16), wd_ref[0])
        return carry

    lax.fori_loop(0, ntile_ref[g], tile_body, 0)

    @pl.when(e == N_EXPERTS - 1)
    def _():
        col = lax.broadcasted_iota(jnp.int32, (MOE_ROWS, MOE_SORTED), 1)
        unperm = (col == pos_ref[:, 0:1]).astype(F32).astype(BF16)
        y = x_ref[...] + _dot(unperm, ys_ref[...].astype(BF16))
        if final_norm:
            ms = jnp.mean(y * y, axis=-1, keepdims=True)
            y = (y * lax.rsqrt(ms + RMS_EPS)) * gfin_ref[...]
        o_ref[...] = y


def _moe(x2d, gain, wr_hi, wr_lo, tri, w_gate, w_up, w_down, gain_final, final_norm):
    n = x2d.shape[0]
    assert n % MOE_ROWS == 0
    const2 = lambda i, e: (0, 0)
    return pl.pallas_call(
        functools.partial(_moe_kernel, final_norm=final_norm),
        grid=(n // MOE_ROWS, N_EXPERTS),
        in_specs=[
            pl.BlockSpec((MOE_ROWS, D_MODEL), lambda i, e: (i, 0)),
            pl.BlockSpec((1, D_MODEL), const2),
            pl.BlockSpec((D_MODEL, ROUTER_LANES), const2),
            pl.BlockSpec((D_MODEL, ROUTER_LANES), const2),
            pl.BlockSpec((MOE_ROWS, MOE_ROWS), const2),
            pl.BlockSpec((1, D_MODEL, D_EXPERT), lambda i, e: (e, 0, 0)),
            pl.BlockSpec((1, D_MODEL, D_EXPERT), lambda i, e: (e, 0, 0)),
            pl.BlockSpec((1, D_EXPERT, D_MODEL), lambda i, e: (e, 0, 0)),
            pl.BlockSpec((1, D_MODEL), const2),
        ],
        out_specs=pl.BlockSpec((MOE_ROWS, D_MODEL), lambda i, e: (i, 0)),
        out_shape=jax.ShapeDtypeStruct(x2d.shape, F32),
        scratch_shapes=[
            pltpu.VMEM((MOE_SORTED, D_MODEL), BF16),
            pltpu.VMEM((MOE_SORTED, ROUTER_LANES), F32),
            pltpu.VMEM((MOE_SORTED, D_MODEL), F32),
            pltpu.VMEM((MOE_ROWS, ROUTER_LANES), jnp.int32),
            pltpu.SMEM((N_EXPERT_GROUPS,), jnp.int32),
            pltpu.SMEM((N_EXPERT_GROUPS,), jnp.int32),
        ],
        compiler_params=pltpu.CompilerParams(
            dimension_semantics=("arbitrary", "arbitrary"),
            vmem_limit_bytes=VMEM_LIMIT_BYTES),
        name="moe",
    )(x2d, gain, wr_hi, wr_lo, tri, w_gate, w_up, w_down, gain_final)


def kernel(x, lb_logits, norm_mix, w_in, hgrn_norm, tmlp_ln_g, tmlp_ln_b, w_spatial, b_spatial,
           w_out, norm_ffn, w_router_group, w_router_expert, w_gate, w_up, w_down, norm_final):
    depth = w_in.shape[0]
    bsz, seq, _ = x.shape
    p = jax.nn.softmax(lb_logits.astype(F32), axis=0)
    lower_bounds = jnp.cumsum(p, axis=0) - p[0:1]
    tril = jnp.tril(jnp.ones((SUB, SUB), dtype=bool))
    earlier = jnp.tril(jnp.ones((MOE_ROWS, MOE_ROWS), BF16), -1)
    for layer in range(depth):
        w_sp = jnp.where(tril[None], w_spatial[layer], 0.0).astype(BF16)
        b_sp = jnp.repeat(b_spatial[layer].T, GROUP_DIM, axis=1)
        x = _mixer(x, norm_mix[layer][None], w_in[layer].astype(BF16), lower_bounds[layer][None],
                   hgrn_norm[layer][None], tmlp_ln_g[layer][None], tmlp_ln_b[layer][None],
                   w_sp, b_sp, w_out[layer].astype(BF16))
        w_r = jnp.concatenate([w_router_expert[layer], w_router_group[layer]], axis=1)
        w_r = jnp.pad(w_r, ((0, 0), (0, ROUTER_LANES - w_r.shape[1])))
        wr_hi, wr_lo = _split_bf16(w_r)
        x2d = _moe(x.reshape(bsz * seq, D_MODEL), norm_ffn[layer][None], wr_hi, wr_lo, earlier,
                   w_gate[layer].astype(BF16), w_up[layer].astype(BF16), w_down[layer].astype(BF16),
                   norm_final[None], final_norm=(layer == depth - 1))
        x = x2d.reshape(bsz, seq, D_MODEL)
    return x
```

```python
import functools

import jax
import jax.numpy as jnp
from jax import lax
from jax.experimental import pallas as pl
from jax.experimental.pallas import tpu as pltpu

F32 = jnp.float32
BF16 = jnp.bfloat16

D_MODEL = 1024
N_HEADS = 4
HEAD_DIM = 128
D_HGRN = N_HEADS * HEAD_DIM
N_GROUPS = 4
GROUP_DIM = 128
D_TMLP = N_GROUPS * GROUP_DIM
D_IN = 4 * D_HGRN + 2 * D_TMLP
SUB = 128
N_EXPERT_GROUPS = 4
EXPERTS_PER_GROUP = 8
N_EXPERTS = N_EXPERT_GROUPS * EXPERTS_PER_GROUP
D_EXPERT = 256
ROUTER_LANES = 128
RMS_EPS = 1e-6
LN_EPS = 1e-5
F_FLOOR = 1e-30
SQRT_HALF = 0.7071067811865476

MIXER_ROWS = 512
MOE_ROWS = 1024
MOE_CHUNK = 320
MOE_SORTED = (MOE_ROWS // MOE_CHUNK + N_EXPERT_GROUPS) * MOE_CHUNK
MOE_STEP_EXPERTS = 4
MOE_STEPS = N_EXPERTS // MOE_STEP_EXPERTS
VMEM_LIMIT_BYTES = 56 * 1024 * 1024


def _dot(a, b):
    return jnp.dot(a, b, preferred_element_type=F32)


def _dot_nt(a, b):
    return lax.dot_general(a, b, (((1,), (1,)), ((), ())), preferred_element_type=F32)


def _dot_tn(a, b):
    return lax.dot_general(a, b, (((0,), (0,)), ((), ())), preferred_element_type=F32)


def _split_bf16(a):
    hi = a.astype(BF16)
    lo = (a - hi.astype(F32)).astype(BF16)
    return hi, lo


def _gelu(a):
    return 0.5 * a * (1.0 + lax.erf(a * SQRT_HALF))


def _silu(a):
    return a * jax.nn.sigmoid(a)


def _boundary_rows(b_ref, m, width):
    pieces = []
    if 2 * m >= 8:
        for s0 in range(0, SUB, 2 * m):
            row = b_ref[pl.ds(s0 + m - 1, 1), :]
            pieces.append(jnp.broadcast_to(row, (2 * m, width)))
    else:
        row8 = lax.broadcasted_iota(jnp.int32, (8, width), 0)
        for g0 in range(0, SUB, 8):
            acc = None
            for s0 in range(0, 8, 2 * m):
                row = jnp.broadcast_to(b_ref[pl.ds(g0 + s0 + m - 1, 1), :], (8, width))
                acc = row if acc is None else jnp.where(row8 >= s0, row, acc)
            pieces.append(acc)
    return jnp.concatenate(pieces, axis=0)


def _hgrn_attention(q, kk, b, b_ref):
    t_idx = lax.broadcasted_iota(jnp.int32, (SUB, SUB), 0)
    s_idx = lax.broadcasted_iota(jnp.int32, (SUB, SUB), 1)
    tx = jnp.bitwise_xor(t_idx, s_idx)
    row = lax.broadcasted_iota(jnp.int32, (SUB, D_HGRN), 0)

    qb = q.astype(BF16)
    kb = kk.astype(BF16)
    att = []
    for h in range(N_HEADS):
        sl = slice(h * HEAD_DIM, (h + 1) * HEAD_DIM)
        att.append(jnp.where(tx == 0, _dot_nt(qb[:, sl], kb[:, sl]), 0.0))
    m = 1
    while m < SUB:
        ref_pt = _boundary_rows(b_ref, m, D_HGRN)
        decay = jnp.exp(-jnp.abs(b - ref_pt))
        right = jnp.bitwise_and(row, m) != 0
        qt = jnp.where(right, q * decay, 0.0).astype(BF16)
        kt = jnp.where(right, 0.0, kk * decay).astype(BF16)
        same_block = tx < 2 * m
        for h in range(N_HEADS):
            sl = slice(h * HEAD_DIM, (h + 1) * HEAD_DIM)
            att[h] = att[h] + jnp.where(same_block, _dot_nt(qt[:, sl], kt[:, sl]), 0.0)
        m *= 2
    return att


def _mixer_kernel(x_ref, gain_ref, win_ref, lb_ref, hn_ref, lng_ref, lnb_ref, wsp_ref, bsp_ref,
                  wout_ref, o_ref, z_ref, y_ref, b_ref, st_ref):
    @pl.when(pl.program_id(1) == 0)
    def _():
        st_ref[...] = jnp.zeros_like(st_ref)

    x = x_ref[0]
    ms = jnp.mean(x * x, axis=-1, keepdims=True)
    h = (x * lax.rsqrt(ms + RMS_EPS)) * gain_ref[...]
    z_ref[...] = _dot(h.astype(BF16), win_ref[...])

    t_idx = lax.broadcasted_iota(jnp.int32, (SUB, SUB), 0)
    s_idx = lax.broadcasted_iota(jnp.int32, (SUB, SUB), 1)
    tri = (s_idx <= t_idx).astype(BF16)

    def sub_chunk(c, carry):
        rows = pl.ds(pl.multiple_of(c * SUB, SUB), SUB)

        q = _silu(z_ref[rows, 0:D_HGRN])
        lb = lb_ref[...]
        fg = lb + (1.0 - lb) * jax.nn.sigmoid(z_ref[rows, D_HGRN:2 * D_HGRN])
        lf = jnp.log(jnp.maximum(fg, F_FLOOR))
        kk = 1.0 - fg
        v = z_ref[rows, 2 * D_HGRN:3 * D_HGRN].astype(BF16)
        lf_hi, lf_lo = _split_bf16(lf)
        b = _dot(tri, lf_hi) + _dot(tri, lf_lo)
        b_ref[...] = b
        att = _hgrn_attention(q, kk, b, b_ref)
        b_end = b_ref[pl.ds(SUB - 1, 1), :]
        q0 = (q * jnp.exp(b)).astype(BF16)
        k_end = (kk * jnp.exp(b_end - b)).astype(BF16)
        s_decay = jnp.exp(b_end)
        g = _silu(z_ref[rows, 3 * D_HGRN:4 * D_HGRN])
        for hd in range(N_HEADS):
            sl = slice(hd * HEAD_DIM, (hd + 1) * HEAD_DIM)
            st = st_ref[hd]
            o = _dot(att[hd].astype(BF16), v[:, sl]) + _dot_nt(q0[:, sl], st.astype(BF16))
            st_ref[hd] = st * s_decay[:, sl] + _dot_tn(v[:, sl], k_end[:, sl])
            oms = jnp.mean(o * o, axis=-1, keepdims=True)
            on = (o * lax.rsqrt(oms + RMS_EPS)) * hn_ref[...]
            y_ref[rows, sl] = (on * g[:, sl]).astype(BF16)

        u = _gelu(z_ref[rows, 4 * D_HGRN:4 * D_HGRN + D_TMLP])
        vv = _gelu(z_ref[rows, 4 * D_HGRN + D_TMLP:D_IN])
        for gi in range(N_GROUPS):
            sl = slice(gi * GROUP_DIM, (gi + 1) * GROUP_DIM)
            vg = vv[:, sl]
            mu = jnp.mean(vg, axis=-1, keepdims=True)
            cen = vg - mu
            var = jnp.mean(cen * cen, axis=-1, keepdims=True)
            vn = (cen * lax.rsqrt(var + LN_EPS)) * lng_ref[:, sl] + lnb_ref[:, sl]
            mixed = _dot(wsp_ref[gi], vn.astype(BF16)) + bsp_ref[:, sl]
            y_ref[rows, D_HGRN + gi * GROUP_DIM:D_HGRN + (gi + 1) * GROUP_DIM] = (
                u[:, sl] * mixed).astype(BF16)
        return carry

    lax.fori_loop(0, MIXER_ROWS // SUB, sub_chunk, 0)
    o_ref[0] = x + _dot(y_ref[...], wout_ref[...])


def _mixer(x, gain, w_in, lb, hgrn_norm, ln_g, ln_b, w_sp, b_sp, w_out):
    bsz, seq, _ = x.shape
    assert seq % MIXER_ROWS == 0 and MIXER_ROWS % SUB == 0
    const2 = lambda b, j: (0, 0)
    const3 = lambda b, j: (0, 0, 0)
    return pl.pallas_call(
        _mixer_kernel,
        grid=(bsz, seq // MIXER_ROWS),
        in_specs=[
            pl.BlockSpec((1, MIXER_ROWS, D_MODEL), lambda b, j: (b, j, 0)),
            pl.BlockSpec((1, D_MODEL), const2),
            pl.BlockSpec((D_MODEL, D_IN), const2),
            pl.BlockSpec((1, D_HGRN), const2),
            pl.BlockSpec((1, HEAD_DIM), const2),
            pl.BlockSpec((1, D_TMLP), const2),
            pl.BlockSpec((1, D_TMLP), const2),
            pl.BlockSpec((N_GROUPS, SUB, SUB), const3),
            pl.BlockSpec((SUB, D_TMLP), const2),
            pl.BlockSpec((D_HGRN + D_TMLP, D_MODEL), const2),
        ],
        out_specs=pl.BlockSpec((1, MIXER_ROWS, D_MODEL), lambda b, j: (b, j, 0)),
        out_shape=jax.ShapeDtypeStruct(x.shape, F32),
        scratch_shapes=[
            pltpu.VMEM((MIXER_ROWS, D_IN), F32),
            pltpu.VMEM((MIXER_ROWS, D_HGRN + D_TMLP), BF16),
            pltpu.VMEM((SUB, D_HGRN), F32),
            pltpu.VMEM((N_HEADS, HEAD_DIM, HEAD_DIM), F32),
        ],
        compiler_params=pltpu.CompilerParams(
            dimension_semantics=("arbitrary", "arbitrary"),
            vmem_limit_bytes=VMEM_LIMIT_BYTES),
        name="mixer",
    )(x, gain, w_in, lb, hgrn_norm, ln_g, ln_b, w_sp, b_sp, w_out)


def _route(h, wr_hi_ref, wr_lo_ref):
    h_hi, h_lo = _split_bf16(h)
    logits = _dot(h_hi, wr_hi_ref[...]) + (_dot(h_hi, wr_lo_ref[...]) + _dot(h_lo, wr_hi_ref[...]))
    lane = lax.broadcasted_iota(jnp.int32, logits.shape, 1)
    neg = jnp.float32(-jnp.inf)
    big = jnp.int32(1 << 20)
    is_group = (lane >= N_EXPERTS) & (lane < N_EXPERTS + N_EXPERT_GROUPS)
    gl = jnp.where(is_group, logits, neg)
    gmax = jnp.max(gl, axis=-1, keepdims=True)
    p_sel = 1.0 / jnp.sum(jnp.exp(gl - gmax), axis=-1, keepdims=True)
    g_idx = jnp.min(jnp.where(gl == gmax, lane - N_EXPERTS, big), axis=-1, keepdims=True)
    in_group = (lane < N_EXPERTS) & (jnp.right_shift(lane, 3) == g_idx)
    el = jnp.where(in_group, logits, neg)
    v1 = jnp.max(el, axis=-1, keepdims=True)
    i1 = jnp.min(jnp.where(el == v1, lane, big), axis=-1, keepdims=True)
    el2 = jnp.where(lane == i1, neg, el)
    v2 = jnp.max(el2, axis=-1, keepdims=True)
    i2 = jnp.min(jnp.where(el2 == v2, lane, big), axis=-1, keepdims=True)
    e2 = jnp.exp(v2 - v1)
    w1 = p_sel / (1.0 + e2)
    w2 = p_sel * e2 / (1.0 + e2)
    comb = jnp.where(lane == i1, w1, 0.0) + jnp.where(lane == i2, w2, 0.0)
    return comb, g_idx


def _moe_kernel(x_ref, gain_ref, wr_hi_ref, wr_lo_ref, tri_ref, wg_ref, wu_ref, wd_ref, gfin_ref,
                o_ref, hb_ref, comb_ref, posc_ref, posr_ref, hs_ref, cs_ref, ys_ref, chunk0_ref,
                nchunk_ref, *, final_norm):
    s = pl.program_id(1)

    @pl.when(s == 0)
    def _():
        x = x_ref[...]
        ms = jnp.mean(x * x, axis=-1, keepdims=True)
        h = (x * lax.rsqrt(ms + RMS_EPS)) * gain_ref[...]
        comb, g_idx = _route(h, wr_hi_ref, wr_lo_ref)
        hb_ref[...] = h.astype(BF16)
        comb_ref[...] = comb
        lane = lax.broadcasted_iota(jnp.int32, comb.shape, 1)
        onehot = lane == g_idx
        prefix = _dot(tri_ref[...], onehot.astype(F32).astype(BF16))
        rank = jnp.sum(jnp.where(onehot, prefix, 0.0), axis=-1, keepdims=True)
        start = jnp.int32(0)
        base = jnp.zeros_like(rank)
        for g in range(N_EXPERT_GROUPS):
            cnt = jnp.sum((g_idx == g).astype(jnp.int32))
            nc = lax.div(cnt + (MOE_CHUNK - 1), jnp.int32(MOE_CHUNK))
            chunk0_ref[g] = start
            nchunk_ref[g] = nc
            base = jnp.where(g_idx == g, (start * MOE_CHUNK).astype(F32), base)
            start = start + nc
        chunk0_ref[N_EXPERT_GROUPS] = start
        pos = (base + rank).astype(jnp.int32)
        posc_ref[...] = jnp.broadcast_to(pos, posc_ref.shape)
        halves = jnp.where(lane == 0, jnp.right_shift(pos, 6),
                           jnp.where(lane == 1, jnp.bitwise_and(pos, 63), 0))
        pick = (lax.broadcasted_iota(jnp.int32, (8, ROUTER_LANES), 0)
                == lax.broadcasted_iota(jnp.int32, (8, ROUTER_LANES), 1))
        halves_t = _dot_nt(pick.astype(F32).astype(BF16), halves.astype(F32).astype(BF16))
        pos_row = (halves_t[0:1] * 64.0 + halves_t[1:2]).astype(jnp.int32)
        posr_ref[...] = jnp.broadcast_to(pos_row, posr_ref.shape)

        def gather_chunk(k, carry):
            r0 = pl.multiple_of(k * MOE_CHUNK, 64)
            r_idx = lax.broadcasted_iota(jnp.int32, (MOE_CHUNK, MOE_ROWS), 0) + r0
            perm = (r_idx == posr_ref[0:1, :]).astype(F32).astype(BF16)
            hs_ref[pl.ds(r0, MOE_CHUNK), :] = _dot(perm, hb_ref[...]).astype(BF16)
            cw = comb_ref[...]
            c_hi = cw.astype(BF16)
            c_res = cw - c_hi.astype(F32)
            c_mid = c_res.astype(BF16)
            c_lo = (c_res - c_mid.astype(F32)).astype(BF16)
            cs_ref[pl.ds(r0, MOE_CHUNK), :] = _dot(perm, c_hi) + (_dot(perm, c_mid) + _dot(perm, c_lo))
            return carry

        lax.fori_loop(0, start, gather_chunk, 0)

    g = lax.shift_right_logical(s, 1)
    c0 = chunk0_ref[g]
    e_base = s * MOE_STEP_EXPERTS
    lane_c = lax.broadcasted_iota(jnp.int32, (MOE_CHUNK, ROUTER_LANES), 1)
    first_half = jnp.bitwise_and(s, 1) == 0

    def chunk_body(k, carry):
        rows = pl.ds(pl.multiple_of((c0 + k) * MOE_CHUNK, 64), MOE_CHUNK)
        hh = hs_ref[rows, :]
        cst = cs_ref[rows, :]
        parts = []
        for j in range(MOE_STEP_EXPERTS):
            c_e = jnp.sum(jnp.where(lane_c == e_base + j, cst, 0.0), axis=-1, keepdims=True)
            hid = _silu(_dot(hh, wg_ref[j])) * _dot(hh, wu_ref[j])
            parts.append((hid * c_e).astype(BF16))
        y = _dot(jnp.concatenate(parts, axis=1), wd_ref[0])

        @pl.when(first_half)
        def _():
            ys_ref[rows, :] = y

        @pl.when(jnp.logical_not(first_half))
        def _():
            ys_ref[rows, :] += y

        return carry

    lax.fori_loop(0, nchunk_ref[g], chunk_body, 0)

    @pl.when(s == MOE_STEPS - 1)
    def _():
        o_ref[...] = x_ref[...]

        def unsort_chunk(k, carry):
            r0 = pl.multiple_of(k * MOE_CHUNK, 64)
            c_idx = lax.broadcasted_iota(jnp.int32, (MOE_ROWS, MOE_CHUNK), 1) + r0
            unperm = (c_idx == posc_ref[:, 0:1]).astype(F32).astype(BF16)
            o_ref[...] += _dot(unperm, ys_ref[pl.ds(r0, MOE_CHUNK), :].astype(BF16))
            return carry

        lax.fori_loop(0, chunk0_ref[N_EXPERT_GROUPS], unsort_chunk, 0)
        if final_norm:
            y = o_ref[...]
            ms = jnp.mean(y * y, axis=-1, keepdims=True)
            o_ref[...] = (y * lax.rsqrt(ms + RMS_EPS)) * gfin_ref[...]


def _moe(x2d, gain, wr_hi, wr_lo, tri, w_gate, w_up, w_down, gain_final, final_norm):
    n = x2d.shape[0]
    assert n % MOE_ROWS == 0
    const2 = lambda i, s: (0, 0)
    once = pl.Buffered(1)
    return pl.pallas_call(
        functools.partial(_moe_kernel, final_norm=final_norm),
        grid=(n // MOE_ROWS, MOE_STEPS),
        in_specs=[
            pl.BlockSpec((MOE_ROWS, D_MODEL), lambda i, s: (i, 0), pipeline_mode=once),
            pl.BlockSpec((1, D_MODEL), const2),
            pl.BlockSpec((D_MODEL, ROUTER_LANES), const2),
            pl.BlockSpec((D_MODEL, ROUTER_LANES), const2),
            pl.BlockSpec((MOE_ROWS, MOE_ROWS), const2, pipeline_mode=once),
            pl.BlockSpec((MOE_STEP_EXPERTS, D_MODEL, D_EXPERT), lambda i, s: (s, 0, 0)),
            pl.BlockSpec((MOE_STEP_EXPERTS, D_MODEL, D_EXPERT), lambda i, s: (s, 0, 0)),
            pl.BlockSpec((1, MOE_STEP_EXPERTS * D_EXPERT, D_MODEL), lambda i, s: (s, 0, 0)),
            pl.BlockSpec((1, D_MODEL), const2),
        ],
        out_specs=pl.BlockSpec((MOE_ROWS, D_MODEL), lambda i, s: (i, 0)),
        out_shape=jax.ShapeDtypeStruct(x2d.shape, F32),
        scratch_shapes=[
            pltpu.VMEM((MOE_ROWS, D_MODEL), BF16),
            pltpu.VMEM((MOE_ROWS, ROUTER_LANES), F32),
            pltpu.VMEM((MOE_ROWS, ROUTER_LANES), jnp.int32),
            pltpu.VMEM((8, MOE_ROWS), jnp.int32),
            pltpu.VMEM((MOE_SORTED, D_MODEL), BF16),
            pltpu.VMEM((MOE_SORTED, ROUTER_LANES), F32),
            pltpu.VMEM((MOE_SORTED, D_MODEL), F32),
            pltpu.SMEM((N_EXPERT_GROUPS + 1,), jnp.int32),
            pltpu.SMEM((N_EXPERT_GROUPS,), jnp.int32),
        ],
        compiler_params=pltpu.CompilerParams(
            dimension_semantics=("arbitrary", "arbitrary"),
            vmem_limit_bytes=VMEM_LIMIT_BYTES),
        name="moe",
    )(x2d, gain, wr_hi, wr_lo, tri, w_gate, w_up, w_down, gain_final)


def kernel(x, lb_logits, norm_mix, w_in, hgrn_norm, tmlp_ln_g, tmlp_ln_b, w_spatial, b_spatial,
           w_out, norm_ffn, w_router_group, w_router_expert, w_gate, w_up, w_down, norm_final):
    depth = w_in.shape[0]
    bsz, seq, _ = x.shape
    p = jax.nn.softmax(lb_logits.astype(F32), axis=0)
    lower_bounds = jnp.cumsum(p, axis=0) - p[0:1]
    tril = jnp.tril(jnp.ones((SUB, SUB), dtype=bool))
    earlier = jnp.tril(jnp.ones((MOE_ROWS, MOE_ROWS), BF16), -1)
    for layer in range(depth):
        w_sp = jnp.where(tril[None], w_spatial[layer], 0.0).astype(BF16)
        b_sp = jnp.repeat(b_spatial[layer].T, GROUP_DIM, axis=1)
        x = _mixer(x, norm_mix[layer][None], w_in[layer].astype(BF16), lower_bounds[layer][None],
                   hgrn_norm[layer][None], tmlp_ln_g[layer][None], tmlp_ln_b[layer][None],
                   w_sp, b_sp, w_out[layer].astype(BF16))
        w_r = jnp.concatenate([w_router_expert[layer], w_router_group[layer]], axis=1)
        w_r = jnp.pad(w_r, ((0, 0), (0, ROUTER_LANES - w_r.shape[1])))
        wr_hi, wr_lo = _split_bf16(w_r)
        x2d = _moe(x.reshape(bsz * seq, D_MODEL), norm_ffn[layer][None], wr_hi, wr_lo, earlier,
                   w_gate[layer].astype(BF16), w_up[layer].astype(BF16),
                   w_down[layer].astype(BF16).reshape(MOE_STEPS, MOE_STEP_EXPERTS * D_EXPERT, D_MODEL),
                   norm_final[None], final_norm=(layer == depth - 1))
        x = x2d.reshape(bsz, seq, D_MODEL)
    return x
```

```python
import functools

import jax
import jax.numpy as jnp
from jax import lax
from jax.experimental import pallas as pl
from jax.experimental.pallas import tpu as pltpu

F32 = jnp.float32
BF16 = jnp.bfloat16

D_MODEL = 1024
N_HEADS = 4
HEAD_DIM = 128
D_HGRN = N_HEADS * HEAD_DIM
N_GROUPS = 4
GROUP_DIM = 128
D_TMLP = N_GROUPS * GROUP_DIM
D_IN = 4 * D_HGRN + 2 * D_TMLP
SUB = 128
N_EXPERT_GROUPS = 4
EXPERTS_PER_GROUP = 8
N_EXPERTS = N_EXPERT_GROUPS * EXPERTS_PER_GROUP
D_EXPERT = 256
ROUTER_LANES = 128
RMS_EPS = 1e-6
LN_EPS = 1e-5
F_FLOOR = 1e-30
HGRN_SAFE_EXP = 60.0
SQRT_HALF = 0.7071067811865476

MIXER_ROWS = 512
MOE_ROWS = 1024
MOE_CHUNK = 320
MOE_SORTED = (MOE_ROWS // MOE_CHUNK + N_EXPERT_GROUPS) * MOE_CHUNK
MOE_STEP_EXPERTS = 4
MOE_STEPS = N_EXPERTS // MOE_STEP_EXPERTS
VMEM_LIMIT_BYTES = 56 * 1024 * 1024


def _dot(a, b):
    return jnp.dot(a, b, preferred_element_type=F32)


def _dot_nt(a, b):
    return lax.dot_general(a, b, (((1,), (1,)), ((), ())), preferred_element_type=F32)


def _dot_tn(a, b):
    return lax.dot_general(a, b, (((0,), (0,)), ((), ())), preferred_element_type=F32)


def _split_bf16(a):
    hi = a.astype(BF16)
    lo = (a - hi.astype(F32)).astype(BF16)
    return hi, lo


def _gelu(a):
    return 0.5 * a * (1.0 + lax.erf(a * SQRT_HALF))


def _silu(a):
    return a * jax.nn.sigmoid(a)


def _boundary_rows(b_ref, r0, m, width):
    pieces = []
    if 2 * m >= 8:
        for s0 in range(0, SUB, 2 * m):
            row = b_ref[pl.ds(r0 + (s0 + m - 1), 1), :]
            pieces.append(jnp.broadcast_to(row, (2 * m, width)))
    else:
        row8 = lax.broadcasted_iota(jnp.int32, (8, width), 0)
        for g0 in range(0, SUB, 8):
            acc = None
            for s0 in range(0, 8, 2 * m):
                row = jnp.broadcast_to(b_ref[pl.ds(r0 + (g0 + s0 + m - 1), 1), :], (8, width))
                acc = row if acc is None else jnp.where(row8 >= s0, row, acc)
            pieces.append(acc)
    return jnp.concatenate(pieces, axis=0)


def _half_middle_rows(b_ref, r0, width):
    half = SUB // 2
    return jnp.concatenate(
        [jnp.broadcast_to(b_ref[pl.ds(r0 + (s0 + half // 2 - 1), 1), :], (half, width))
         for s0 in range(0, SUB, half)], axis=0)


def _hgrn_level(att, q, kk, b, b_ref, r0, m, tx, row):
    ref_pt = _boundary_rows(b_ref, r0, m, D_HGRN)
    decay = jnp.exp(-jnp.abs(b - ref_pt))
    right = jnp.bitwise_and(row, m) != 0
    qt = jnp.where(right, q * decay, 0.0).astype(BF16)
    kt = jnp.where(right, 0.0, kk * decay).astype(BF16)
    same_block = tx < 2 * m
    out = []
    for h in range(N_HEADS):
        sl = slice(h * HEAD_DIM, (h + 1) * HEAD_DIM)
        term = jnp.where(same_block, _dot_nt(qt[:, sl], kt[:, sl]), 0.0)
        out.append(term if att is None else att[h] + term)
    return out


def _hgrn_attention(q, kk, b, b_ref, r0, shared_reference):
    t_idx = lax.broadcasted_iota(jnp.int32, (SUB, SUB), 0)
    s_idx = lax.broadcasted_iota(jnp.int32, (SUB, SUB), 1)
    tx = jnp.bitwise_xor(t_idx, s_idx)
    row = lax.broadcasted_iota(jnp.int32, (SUB, D_HGRN), 0)
    half = SUB // 2
    att = _hgrn_level(None, q, kk, b, b_ref, r0, half, tx, row)
    if shared_reference:
        expo = b - _half_middle_rows(b_ref, r0, D_HGRN)
        qt = (q * jnp.exp(expo)).astype(BF16)
        kt = (kk * jnp.exp(-expo)).astype(BF16)
        keep = (tx < half) & (s_idx <= t_idx)
        for h in range(N_HEADS):
            sl = slice(h * HEAD_DIM, (h + 1) * HEAD_DIM)
            att[h] = att[h] + jnp.where(keep, _dot_nt(qt[:, sl], kt[:, sl]), 0.0)
        return att
    qb = q.astype(BF16)
    kb = kk.astype(BF16)
    for h in range(N_HEADS):
        sl = slice(h * HEAD_DIM, (h + 1) * HEAD_DIM)
        att[h] = att[h] + jnp.where(tx == 0, _dot_nt(qb[:, sl], kb[:, sl]), 0.0)
    m = 1
    while m < half:
        att = _hgrn_level(att, q, kk, b, b_ref, r0, m, tx, row)
        m *= 2
    return att


def _hgrn_sub_chunk(z_ref, kk_ref, b_ref, hn_ref, y_ref, r0, state, shared_reference):
    rows = pl.ds(r0, SUB)
    q = _silu(z_ref[rows, 0:D_HGRN])
    kk = kk_ref[rows, :]
    b = b_ref[rows, :]
    v = z_ref[rows, 2 * D_HGRN:3 * D_HGRN].astype(BF16)
    att = _hgrn_attention(q, kk, b, b_ref, r0, shared_reference)
    b_end = b_ref[pl.ds(r0 + (SUB - 1), 1), :]
    q0 = (q * jnp.exp(b)).astype(BF16)
    k_end = (kk * jnp.exp(b_end - b)).astype(BF16)
    s_decay = jnp.exp(b_end)
    g = _silu(z_ref[rows, 3 * D_HGRN:4 * D_HGRN])
    new_state = []
    for hd in range(N_HEADS):
        sl = slice(hd * HEAD_DIM, (hd + 1) * HEAD_DIM)
        st = state[hd]
        o = _dot(att[hd].astype(BF16), v[:, sl]) + _dot_nt(q0[:, sl], st.astype(BF16))
        new_state.append(st * s_decay[:, sl] + _dot_tn(v[:, sl], k_end[:, sl]))
        oms = jnp.mean(o * o, axis=-1, keepdims=True)
        on = (o * lax.rsqrt(oms + RMS_EPS)) * hn_ref[...]
        y_ref[rows, sl] = (on * g[:, sl]).astype(BF16)
    return new_state


def _mixer_kernel(x_ref, gain_ref, win_ref, lb_ref, hn_ref, lng_ref, lnb_ref, wsp_ref, bsp_ref,
                  wout_ref, o_ref, z_ref, y_ref, b_ref, kk_ref, st_ref):
    @pl.when(pl.program_id(1) == 0)
    def _():
        st_ref[...] = jnp.zeros_like(st_ref)

    x = x_ref[0]
    ms = jnp.mean(x * x, axis=-1, keepdims=True)
    h = (x * lax.rsqrt(ms + RMS_EPS)) * gain_ref[...]
    z_ref[...] = _dot(h.astype(BF16), win_ref[...])

    t_idx = lax.broadcasted_iota(jnp.int32, (SUB, SUB), 0)
    s_idx = lax.broadcasted_iota(jnp.int32, (SUB, SUB), 1)
    tri = (s_idx <= t_idx).astype(BF16)
    n_sub = MIXER_ROWS // SUB

    worst = jnp.zeros((SUB, D_HGRN), F32)
    for c in range(n_sub):
        rows = pl.ds(c * SUB, SUB)
        lb = lb_ref[...]
        fg = lb + (1.0 - lb) * jax.nn.sigmoid(z_ref[rows, D_HGRN:2 * D_HGRN])
        lf_hi, lf_lo = _split_bf16(jnp.log(jnp.maximum(fg, F_FLOOR)))
        kk_ref[rows, :] = 1.0 - fg
        b = _dot(tri, lf_hi) + _dot(tri, lf_lo)
        b_ref[rows, :] = b
        worst = jnp.maximum(worst, jnp.abs(b - _half_middle_rows(b_ref, c * SUB, D_HGRN)))
    shared_ok = jnp.max(worst) <= HGRN_SAFE_EXP

    for c in range(n_sub):
        rows = pl.ds(c * SUB, SUB)
        u = _gelu(z_ref[rows, 4 * D_HGRN:4 * D_HGRN + D_TMLP])
        vv = _gelu(z_ref[rows, 4 * D_HGRN + D_TMLP:D_IN])
        for gi in range(N_GROUPS):
            sl = slice(gi * GROUP_DIM, (gi + 1) * GROUP_DIM)
            vg = vv[:, sl]
            mu = jnp.mean(vg, axis=-1, keepdims=True)
            cen = vg - mu
            var = jnp.mean(cen * cen, axis=-1, keepdims=True)
            vn = (cen * lax.rsqrt(var + LN_EPS)) * lng_ref[:, sl] + lnb_ref[:, sl]
            mixed = _dot(wsp_ref[gi], vn.astype(BF16)) + bsp_ref[:, sl]
            y_ref[rows, D_HGRN + gi * GROUP_DIM:D_HGRN + (gi + 1) * GROUP_DIM] = (
                u[:, sl] * mixed).astype(BF16)

    @pl.when(shared_ok)
    def _():
        state = [st_ref[hd] for hd in range(N_HEADS)]
        for c in range(n_sub):
            state = _hgrn_sub_chunk(z_ref, kk_ref, b_ref, hn_ref, y_ref, c * SUB, state, True)
        for hd in range(N_HEADS):
            st_ref[hd] = state[hd]

    @pl.when(jnp.logical_not(shared_ok))
    def _():
        def sub_chunk(c, carry):
            r0 = pl.multiple_of(c * SUB, SUB)
            state = [st_ref[hd] for hd in range(N_HEADS)]
            state = _hgrn_sub_chunk(z_ref, kk_ref, b_ref, hn_ref, y_ref, r0, state, False)
            for hd in range(N_HEADS):
                st_ref[hd] = state[hd]
            return carry

        lax.fori_loop(0, n_sub, sub_chunk, 0)

    o_ref[0] = x + _dot(y_ref[...], wout_ref[...])


def _mixer(x, gain, w_in, lb, hgrn_norm, ln_g, ln_b, w_sp, b_sp, w_out):
    bsz, seq, _ = x.shape
    assert seq % MIXER_ROWS == 0 and MIXER_ROWS % SUB == 0
    const2 = lambda b, j: (0, 0)
    const3 = lambda b, j: (0, 0, 0)
    return pl.pallas_call(
        _mixer_kernel,
        grid=(bsz, seq // MIXER_ROWS),
        in_specs=[
            pl.BlockSpec((1, MIXER_ROWS, D_MODEL), lambda b, j: (b, j, 0)),
            pl.BlockSpec((1, D_MODEL), const2),
            pl.BlockSpec((D_MODEL, D_IN), const2),
            pl.BlockSpec((1, D_HGRN), const2),
            pl.BlockSpec((1, HEAD_DIM), const2),
            pl.BlockSpec((1, D_TMLP), const2),
            pl.BlockSpec((1, D_TMLP), const2),
            pl.BlockSpec((N_GROUPS, SUB, SUB), const3),
            pl.BlockSpec((SUB, D_TMLP), const2),
            pl.BlockSpec((D_HGRN + D_TMLP, D_MODEL), const2),
        ],
        out_specs=pl.BlockSpec((1, MIXER_ROWS, D_MODEL), lambda b, j: (b, j, 0)),
        out_shape=jax.ShapeDtypeStruct(x.shape, F32),
        scratch_shapes=[
            pltpu.VMEM((MIXER_ROWS, D_IN), F32),
            pltpu.VMEM((MIXER_ROWS, D_HGRN + D_TMLP), BF16),
            pltpu.VMEM((MIXER_ROWS, D_HGRN), F32),
            pltpu.VMEM((MIXER_ROWS, D_HGRN), F32),
            pltpu.VMEM((N_HEADS, HEAD_DIM, HEAD_DIM), F32),
        ],
        compiler_params=pltpu.CompilerParams(
            dimension_semantics=("arbitrary", "arbitrary"),
            vmem_limit_bytes=VMEM_LIMIT_BYTES),
        name="mixer",
    )(x, gain, w_in, lb, hgrn_norm, ln_g, ln_b, w_sp, b_sp, w_out)


def _route(h, wr_hi_ref, wr_lo_ref):
    h_hi, h_lo = _split_bf16(h)
    logits = _dot(h_hi, wr_hi_ref[...]) + (_dot(h_hi, wr_lo_ref[...]) + _dot(h_lo, wr_hi_ref[...]))
    lane = lax.broadcasted_iota(jnp.int32, logits.shape, 1)
    neg = jnp.float32(-jnp.inf)
    big = jnp.int32(1 << 20)
    is_group = (lane >= N_EXPERTS) & (lane < N_EXPERTS + N_EXPERT_GROUPS)
    gl = jnp.where(is_group, logits, neg)
    gmax = jnp.max(gl, axis=-1, keepdims=True)
    p_sel = 1.0 / jnp.sum(jnp.exp(gl - gmax), axis=-1, keepdims=True)
    g_idx = jnp.min(jnp.where(gl == gmax, lane - N_EXPERTS, big), axis=-1, keepdims=True)
    in_group = (lane < N_EXPERTS) & (jnp.right_shift(lane, 3) == g_idx)
    el = jnp.where(in_group, logits, neg)
    v1 = jnp.max(el, axis=-1, keepdims=True)
    i1 = jnp.min(jnp.where(el == v1, lane, big), axis=-1, keepdims=True)
    el2 = jnp.where(lane == i1, neg, el)
    v2 = jnp.max(el2, axis=-1, keepdims=True)
    i2 = jnp.min(jnp.where(el2 == v2, lane, big), axis=-1, keepdims=True)
    e2 = jnp.exp(v2 - v1)
    w1 = p_sel / (1.0 + e2)
    w2 = p_sel * e2 / (1.0 + e2)
    comb = jnp.where(lane == i1, w1, 0.0) + jnp.where(lane == i2, w2, 0.0)
    return comb, g_idx


def _moe_kernel(x_ref, gain_ref, wr_hi_ref, wr_lo_ref, tri_ref, wg_ref, wu_ref, wd_ref, gfin_ref,
                o_ref, hb_ref, comb_ref, posc_ref, posr_ref, hs_ref, cs_ref, ys_ref, chunk0_ref,
                nchunk_ref, *, final_norm):
    s = pl.program_id(1)

    @pl.when(s == 0)
    def _():
        x = x_ref[...]
        ms = jnp.mean(x * x, axis=-1, keepdims=True)
        h = (x * lax.rsqrt(ms + RMS_EPS)) * gain_ref[...]
        comb, g_idx = _route(h, wr_hi_ref, wr_lo_ref)
        hb_ref[...] = h.astype(BF16)
        comb_ref[...] = comb
        lane = lax.broadcasted_iota(jnp.int32, comb.shape, 1)
        onehot = lane == g_idx
        prefix = _dot(tri_ref[...], onehot.astype(F32).astype(BF16))
        rank = jnp.sum(jnp.where(onehot, prefix, 0.0), axis=-1, keepdims=True)
        start = jnp.int32(0)
        base = jnp.zeros_like(rank)
        for g in range(N_EXPERT_GROUPS):
            cnt = jnp.sum((g_idx == g).astype(jnp.int32))
            nc = lax.div(cnt + (MOE_CHUNK - 1), jnp.int32(MOE_CHUNK))
            chunk0_ref[g] = start
            nchunk_ref[g] = nc
            base = jnp.where(g_idx == g, (start * MOE_CHUNK).astype(F32), base)
            start = start + nc
        chunk0_ref[N_EXPERT_GROUPS] = start
        pos = (base + rank).astype(jnp.int32)
        posc_ref[...] = jnp.broadcast_to(pos, posc_ref.shape)
        halves = jnp.where(lane == 0, jnp.right_shift(pos, 6),
                           jnp.where(lane == 1, jnp.bitwise_and(pos, 63), 0))
        pick = (lax.broadcasted_iota(jnp.int32, (8, ROUTER_LANES), 0)
                == lax.broadcasted_iota(jnp.int32, (8, ROUTER_LANES), 1))
        halves_t = _dot_nt(pick.astype(F32).astype(BF16), halves.astype(F32).astype(BF16))
        pos_row = (halves_t[0:1] * 64.0 + halves_t[1:2]).astype(jnp.int32)
        posr_ref[...] = jnp.broadcast_to(pos_row, posr_ref.shape)

        def gather_chunk(k, carry):
            r0 = pl.multiple_of(k * MOE_CHUNK, 64)
            r_idx = lax.broadcasted_iota(jnp.int32, (MOE_CHUNK, MOE_ROWS), 0) + r0
            perm = (r_idx == posr_ref[0:1, :]).astype(F32).astype(BF16)
            hs_ref[pl.ds(r0, MOE_CHUNK), :] = _dot(perm, hb_ref[...]).astype(BF16)
            cw = comb_ref[...]
            c_hi = cw.astype(BF16)
            c_res = cw - c_hi.astype(F32)
            c_mid = c_res.astype(BF16)
            c_lo = (c_res - c_mid.astype(F32)).astype(BF16)
            cs_ref[pl.ds(r0, MOE_CHUNK), :] = _dot(perm, c_hi) + (_dot(perm, c_mid) + _dot(perm, c_lo))
            return carry

        lax.fori_loop(0, start, gather_chunk, 0)

    g = lax.shift_right_logical(s, 1)
    c0 = chunk0_ref[g]
    e_base = s * MOE_STEP_EXPERTS
    lane_c = lax.broadcasted_iota(jnp.int32, (MOE_CHUNK, ROUTER_LANES), 1)
    first_half = jnp.bitwise_and(s, 1) == 0

    def chunk_body(k, carry):
        rows = pl.ds(pl.multiple_of((c0 + k) * MOE_CHUNK, 64), MOE_CHUNK)
        hh = hs_ref[rows, :]
        cst = cs_ref[rows, :]
        parts = []
        for j in range(MOE_STEP_EXPERTS):
            c_e = jnp.sum(jnp.where(lane_c == e_base + j, cst, 0.0), axis=-1, keepdims=True)
            hid = _silu(_dot(hh, wg_ref[j])) * _dot(hh, wu_ref[j])
            parts.append((hid * c_e).astype(BF16))
        y = _dot(jnp.concatenate(parts, axis=1), wd_ref[0])

        @pl.when(first_half)
        def _():
            ys_ref[rows, :] = y

        @pl.when(jnp.logical_not(first_half))
        def _():
            ys_ref[rows, :] += y

        return carry

    lax.fori_loop(0, nchunk_ref[g], chunk_body, 0)

    @pl.when(s == MOE_STEPS - 1)
    def _():
        o_ref[...] = x_ref[...]

        def unsort_chunk(k, carry):
            r0 = pl.multiple_of(k * MOE_CHUNK, 64)
            c_idx = lax.broadcasted_iota(jnp.int32, (MOE_ROWS, MOE_CHUNK), 1) + r0
            unperm = (c_idx == posc_ref[:, 0:1]).astype(F32).astype(BF16)
            o_ref[...] += _dot(unperm, ys_ref[pl.ds(r0, MOE_CHUNK), :].astype(BF16))
            return carry

        lax.fori_loop(0, chunk0_ref[N_EXPERT_GROUPS], unsort_chunk, 0)
        if final_norm:
            y = o_ref[...]
            ms = jnp.mean(y * y, axis=-1, keepdims=True)
            o_ref[...] = (y * lax.rsqrt(ms + RMS_EPS)) * gfin_ref[...]


def _moe(x2d, gain, wr_hi, wr_lo, tri, w_gate, w_up, w_down, gain_final, final_norm):
    n = x2d.shape[0]
    assert n % MOE_ROWS == 0
    const2 = lambda i, s: (0, 0)
    once = pl.Buffered(1)
    return pl.pallas_call(
        functools.partial(_moe_kernel, final_norm=final_norm),
        grid=(n // MOE_ROWS, MOE_STEPS),
        in_specs=[
            pl.BlockSpec((MOE_ROWS, D_MODEL), lambda i, s: (i, 0), pipeline_mode=once),
            pl.BlockSpec((1, D_MODEL), const2),
            pl.BlockSpec((D_MODEL, ROUTER_LANES), const2),
            pl.BlockSpec((D_MODEL, ROUTER_LANES), const2),
            pl.BlockSpec((MOE_ROWS, MOE_ROWS), const2, pipeline_mode=once),
            pl.BlockSpec((MOE_STEP_EXPERTS, D_MODEL, D_EXPERT), lambda i, s: (s, 0, 0)),
            pl.BlockSpec((MOE_STEP_EXPERTS, D_MODEL, D_EXPERT), lambda i, s: (s, 0, 0)),
            pl.BlockSpec((1, MOE_STEP_EXPERTS * D_EXPERT, D_MODEL), lambda i, s: (s, 0, 0)),
            pl.BlockSpec((1, D_MODEL), const2),
        ],
        out_specs=pl.BlockSpec((MOE_ROWS, D_MODEL), lambda i, s: (i, 0)),
        out_shape=jax.ShapeDtypeStruct(x2d.shape, F32),
        scratch_shapes=[
            pltpu.VMEM((MOE_ROWS, D_MODEL), BF16),
            pltpu.VMEM((MOE_ROWS, ROUTER_LANES), F32),
            pltpu.VMEM((MOE_ROWS, ROUTER_LANES), jnp.int32),
            pltpu.VMEM((8, MOE_ROWS), jnp.int32),
            pltpu.VMEM((MOE_SORTED, D_MODEL), BF16),
            pltpu.VMEM((MOE_SORTED, ROUTER_LANES), F32),
            pltpu.VMEM((MOE_SORTED, D_MODEL), F32),
            pltpu.SMEM((N_EXPERT_GROUPS + 1,), jnp.int32),
            pltpu.SMEM((N_EXPERT_GROUPS,), jnp.int32),
        ],
        compiler_params=pltpu.CompilerParams(
            dimension_semantics=("arbitrary", "arbitrary"),
            vmem_limit_bytes=VMEM_LIMIT_BYTES),
        name="moe",
    )(x2d, gain, wr_hi, wr_lo, tri, w_gate, w_up, w_down, gain_final)


def kernel(x, lb_logits, norm_mix, w_in, hgrn_norm, tmlp_ln_g, tmlp_ln_b, w_spatial, b_spatial,
           w_out, norm_ffn, w_router_group, w_router_expert, w_gate, w_up, w_down, norm_final):
    depth = w_in.shape[0]
    bsz, seq, _ = x.shape
    p = jax.nn.softmax(lb_logits.astype(F32), axis=0)
    lower_bounds = jnp.cumsum(p, axis=0) - p[0:1]
    tril = jnp.tril(jnp.ones((SUB, SUB), dtype=bool))
    earlier = jnp.tril(jnp.ones((MOE_ROWS, MOE_ROWS), BF16), -1)
    for layer in range(depth):
        w_sp = jnp.where(tril[None], w_spatial[layer], 0.0).astype(BF16)
        b_sp = jnp.repeat(b_spatial[layer].T, GROUP_DIM, axis=1)
        x = _mixer(x, norm_mix[layer][None], w_in[layer].astype(BF16), lower_bounds[layer][None],
                   hgrn_norm[layer][None], tmlp_ln_g[layer][None], tmlp_ln_b[layer][None],
                   w_sp, b_sp, w_out[layer].astype(BF16))
        w_r = jnp.concatenate([w_router_expert[layer], w_router_group[layer]], axis=1)
        w_r = jnp.pad(w_r, ((0, 0), (0, ROUTER_LANES - w_r.shape[1])))
        wr_hi, wr_lo = _split_bf16(w_r)
        x2d = _moe(x.reshape(bsz * seq, D_MODEL), norm_ffn[layer][None], wr_hi, wr_lo, earlier,
                   w_gate[layer].astype(BF16), w_up[layer].astype(BF16),
                   w_down[layer].astype(BF16).reshape(MOE_STEPS, MOE_STEP_EXPERTS * D_EXPERT, D_MODEL),
                   norm_final[None], final_norm=(layer == depth - 1))
        x = x2d.reshape(bsz, seq, D_MODEL)
    return x
```

```python
import functools

import jax
import jax.numpy as jnp
from jax import lax
from jax.experimental import pallas as pl
from jax.experimental.pallas import tpu as pltpu

F32 = jnp.float32
BF16 = jnp.bfloat16

D_MODEL = 1024
N_HEADS = 4
HEAD_DIM = 128
D_HGRN = N_HEADS * HEAD_DIM
N_GROUPS = 4
GROUP_DIM = 128
D_TMLP = N_GROUPS * GROUP_DIM
D_IN = 4 * D_HGRN + 2 * D_TMLP
SUB = 128
N_EXPERT_GROUPS = 4
EXPERTS_PER_GROUP = 8
N_EXPERTS = N_EXPERT_GROUPS * EXPERTS_PER_GROUP
D_EXPERT = 256
ROUTER_LANES = 128
RMS_EPS = 1e-6
LN_EPS = 1e-5
F_FLOOR = 1e-30
HGRN_SAFE_EXP = 60.0
SQRT_HALF = 0.7071067811865476

MIXER_ROWS = 512
ROUTE_ROWS = 1024
SCATTER_ROWS = 2048
COMBINE_ROWS = 1024
EXPERT_TILE = 512
PACKED = D_MODEL // 2
DMA_UNROLL = 8
VMEM_LIMIT_BYTES = 56 * 1024 * 1024


def _dot(a, b):
    return jnp.dot(a, b, preferred_element_type=F32)


def _dot_nt(a, b):
    return lax.dot_general(a, b, (((1,), (1,)), ((), ())), preferred_element_type=F32)


def _dot_tn(a, b):
    return lax.dot_general(a, b, (((0,), (0,)), ((), ())), preferred_element_type=F32)


def _split_bf16(a):
    hi = a.astype(BF16)
    lo = (a - hi.astype(F32)).astype(BF16)
    return hi, lo


def _gelu(a):
    return 0.5 * a * (1.0 + lax.erf(a * SQRT_HALF))


def _silu(a):
    return a * jax.nn.sigmoid(a)


def _boundary_rows(b_ref, r0, m, width):
    pieces = []
    if 2 * m >= 8:
        for s0 in range(0, SUB, 2 * m):
            row = b_ref[pl.ds(r0 + (s0 + m - 1), 1), :]
            pieces.append(jnp.broadcast_to(row, (2 * m, width)))
    else:
        row8 = lax.broadcasted_iota(jnp.int32, (8, width), 0)
        for g0 in range(0, SUB, 8):
            acc = None
            for s0 in range(0, 8, 2 * m):
                row = jnp.broadcast_to(b_ref[pl.ds(r0 + (g0 + s0 + m - 1), 1), :], (8, width))
                acc = row if acc is None else jnp.where(row8 >= s0, row, acc)
            pieces.append(acc)
    return jnp.concatenate(pieces, axis=0)


def _half_middle_rows(b_ref, r0, width):
    half = SUB // 2
    return jnp.concatenate(
        [jnp.broadcast_to(b_ref[pl.ds(r0 + (s0 + half // 2 - 1), 1), :], (half, width))
         for s0 in range(0, SUB, half)], axis=0)


def _hgrn_level(att, q, kk, b, b_ref, r0, m, tx, row):
    ref_pt = _boundary_rows(b_ref, r0, m, D_HGRN)
    decay = jnp.exp(-jnp.abs(b - ref_pt))
    right = jnp.bitwise_and(row, m) != 0
    qt = jnp.where(right, q * decay, 0.0).astype(BF16)
    kt = jnp.where(right, 0.0, kk * decay).astype(BF16)
    same_block = tx < 2 * m
    out = []
    for h in range(N_HEADS):
        sl = slice(h * HEAD_DIM, (h + 1) * HEAD_DIM)
        term = jnp.where(same_block, _dot_nt(qt[:, sl], kt[:, sl]), 0.0)
        out.append(term if att is None else att[h] + term)
    return out


def _hgrn_attention(q, kk, b, b_ref, r0, shared_reference):
    t_idx = lax.broadcasted_iota(jnp.int32, (SUB, SUB), 0)
    s_idx = lax.broadcasted_iota(jnp.int32, (SUB, SUB), 1)
    tx = jnp.bitwise_xor(t_idx, s_idx)
    row = lax.broadcasted_iota(jnp.int32, (SUB, D_HGRN), 0)
    half = SUB // 2
    att = _hgrn_level(None, q, kk, b, b_ref, r0, half, tx, row)
    if shared_reference:
        expo = b - _half_middle_rows(b_ref, r0, D_HGRN)
        qt = (q * jnp.exp(expo)).astype(BF16)
        kt = (kk * jnp.exp(-expo)).astype(BF16)
        keep = (tx < half) & (s_idx <= t_idx)
        for h in range(N_HEADS):
            sl = slice(h * HEAD_DIM, (h + 1) * HEAD_DIM)
            att[h] = att[h] + jnp.where(keep, _dot_nt(qt[:, sl], kt[:, sl]), 0.0)
        return att
    qb = q.astype(BF16)
    kb = kk.astype(BF16)
    for h in range(N_HEADS):
        sl = slice(h * HEAD_DIM, (h + 1) * HEAD_DIM)
        att[h] = att[h] + jnp.where(tx == 0, _dot_nt(qb[:, sl], kb[:, sl]), 0.0)
    m = 1
    while m < half:
        att = _hgrn_level(att, q, kk, b, b_ref, r0, m, tx, row)
        m *= 2
    return att


def _hgrn_sub_chunk(z_ref, kk_ref, b_ref, hn_ref, y_ref, r0, state, shared_reference):
    rows = pl.ds(r0, SUB)
    q = _silu(z_ref[rows, 0:D_HGRN])
    kk = kk_ref[rows, :]
    b = b_ref[rows, :]
    v = z_ref[rows, 2 * D_HGRN:3 * D_HGRN].astype(BF16)
    att = _hgrn_attention(q, kk, b, b_ref, r0, shared_reference)
    b_end = b_ref[pl.ds(r0 + (SUB - 1), 1), :]
    q0 = (q * jnp.exp(b)).astype(BF16)
    k_end = (kk * jnp.exp(b_end - b)).astype(BF16)
    s_decay = jnp.exp(b_end)
    g = _silu(z_ref[rows, 3 * D_HGRN:4 * D_HGRN])
    new_state = []
    for hd in range(N_HEADS):
        sl = slice(hd * HEAD_DIM, (hd + 1) * HEAD_DIM)
        st = state[hd]
        o = _dot(att[hd].astype(BF16), v[:, sl]) + _dot_nt(q0[:, sl], st.astype(BF16))
        new_state.append(st * s_decay[:, sl] + _dot_tn(v[:, sl], k_end[:, sl]))
        oms = jnp.mean(o * o, axis=-1, keepdims=True)
        on = (o * lax.rsqrt(oms + RMS_EPS)) * hn_ref[...]
        y_ref[rows, sl] = (on * g[:, sl]).astype(BF16)
    return new_state


def _mixer_kernel(x_ref, gain_ref, win_ref, lb_ref, hn_ref, lng_ref, lnb_ref, wsp_ref, bsp_ref,
                  wout_ref, o_ref, z_ref, y_ref, b_ref, kk_ref, st_ref):
    @pl.when(pl.program_id(1) == 0)
    def _():
        st_ref[...] = jnp.zeros_like(st_ref)

    x = x_ref[0]
    ms = jnp.mean(x * x, axis=-1, keepdims=True)
    h = (x * lax.rsqrt(ms + RMS_EPS)) * gain_ref[...]
    z_ref[...] = _dot(h.astype(BF16), win_ref[...])

    t_idx = lax.broadcasted_iota(jnp.int32, (SUB, SUB), 0)
    s_idx = lax.broadcasted_iota(jnp.int32, (SUB, SUB), 1)
    tri = (s_idx <= t_idx).astype(BF16)
    n_sub = MIXER_ROWS // SUB

    worst = jnp.zeros((SUB, D_HGRN), F32)
    for c in range(n_sub):
        rows = pl.ds(c * SUB, SUB)
        lb = lb_ref[...]
        fg = lb + (1.0 - lb) * jax.nn.sigmoid(z_ref[rows, D_HGRN:2 * D_HGRN])
        lf_hi, lf_lo = _split_bf16(jnp.log(jnp.maximum(fg, F_FLOOR)))
        kk_ref[rows, :] = 1.0 - fg
        b = _dot(tri, lf_hi) + _dot(tri, lf_lo)
        b_ref[rows, :] = b
        worst = jnp.maximum(worst, jnp.abs(b - _half_middle_rows(b_ref, c * SUB, D_HGRN)))
    shared_ok = jnp.max(worst) <= HGRN_SAFE_EXP

    for c in range(n_sub):
        rows = pl.ds(c * SUB, SUB)
        u = _gelu(z_ref[rows, 4 * D_HGRN:4 * D_HGRN + D_TMLP])
        vv = _gelu(z_ref[rows, 4 * D_HGRN + D_TMLP:D_IN])
        for gi in range(N_GROUPS):
            sl = slice(gi * GROUP_DIM, (gi + 1) * GROUP_DIM)
            vg = vv[:, sl]
            mu = jnp.mean(vg, axis=-1, keepdims=True)
            cen = vg - mu
            var = jnp.mean(cen * cen, axis=-1, keepdims=True)
            vn = (cen * lax.rsqrt(var + LN_EPS)) * lng_ref[:, sl] + lnb_ref[:, sl]
            mixed = _dot(wsp_ref[gi], vn.astype(BF16)) + bsp_ref[:, sl]
            y_ref[rows, D_HGRN + gi * GROUP_DIM:D_HGRN + (gi + 1) * GROUP_DIM] = (
                u[:, sl] * mixed).astype(BF16)

    @pl.when(shared_ok)
    def _():
        state = [st_ref[hd] for hd in range(N_HEADS)]
        for c in range(n_sub):
            state = _hgrn_sub_chunk(z_ref, kk_ref, b_ref, hn_ref, y_ref, c * SUB, state, True)
        for hd in range(N_HEADS):
            st_ref[hd] = state[hd]

    @pl.when(jnp.logical_not(shared_ok))
    def _():
        def sub_chunk(c, carry):
            r0 = pl.multiple_of(c * SUB, SUB)
            state = [st_ref[hd] for hd in range(N_HEADS)]
            state = _hgrn_sub_chunk(z_ref, kk_ref, b_ref, hn_ref, y_ref, r0, state, False)
            for hd in range(N_HEADS):
                st_ref[hd] = state[hd]
            return carry

        lax.fori_loop(0, n_sub, sub_chunk, 0)

    o_ref[0] = x + _dot(y_ref[...], wout_ref[...])


def _mixer(x, gain, w_in, lb, hgrn_norm, ln_g, ln_b, w_sp, b_sp, w_out):
    bsz, seq, _ = x.shape
    assert seq % MIXER_ROWS == 0 and MIXER_ROWS % SUB == 0
    const2 = lambda b, j: (0, 0)
    const3 = lambda b, j: (0, 0, 0)
    return pl.pallas_call(
        _mixer_kernel,
        grid=(bsz, seq // MIXER_ROWS),
        in_specs=[
            pl.BlockSpec((1, MIXER_ROWS, D_MODEL), lambda b, j: (b, j, 0)),
            pl.BlockSpec((1, D_MODEL), const2),
            pl.BlockSpec((D_MODEL, D_IN), const2),
            pl.BlockSpec((1, D_HGRN), const2),
            pl.BlockSpec((1, HEAD_DIM), const2),
            pl.BlockSpec((1, D_TMLP), const2),
            pl.BlockSpec((1, D_TMLP), const2),
            pl.BlockSpec((N_GROUPS, SUB, SUB), const3),
            pl.BlockSpec((SUB, D_TMLP), const2),
            pl.BlockSpec((D_HGRN + D_TMLP, D_MODEL), const2),
        ],
        out_specs=pl.BlockSpec((1, MIXER_ROWS, D_MODEL), lambda b, j: (b, j, 0)),
        out_shape=jax.ShapeDtypeStruct(x.shape, F32),
        scratch_shapes=[
            pltpu.VMEM((MIXER_ROWS, D_IN), F32),
            pltpu.VMEM((MIXER_ROWS, D_HGRN + D_TMLP), BF16),
            pltpu.VMEM((MIXER_ROWS, D_HGRN), F32),
            pltpu.VMEM((MIXER_ROWS, D_HGRN), F32),
            pltpu.VMEM((N_HEADS, HEAD_DIM, HEAD_DIM), F32),
        ],
        compiler_params=pltpu.CompilerParams(
            dimension_semantics=("arbitrary", "arbitrary"),
            vmem_limit_bytes=VMEM_LIMIT_BYTES),
        name="mixer",
    )(x, gain, w_in, lb, hgrn_norm, ln_g, ln_b, w_sp, b_sp, w_out)


def _pack_bf16_pairs(a):
    lo = lax.bitcast_convert_type(a[:, :PACKED].astype(BF16).astype(F32), jnp.uint32)
    hi = lax.bitcast_convert_type(a[:, PACKED:].astype(BF16).astype(F32), jnp.uint32)
    return jnp.bitwise_or(lax.shift_right_logical(lo, jnp.uint32(16)),
                          jnp.bitwise_and(hi, jnp.uint32(0xFFFF0000)))


def _unpack_bf16_pairs(u):
    lo = lax.bitcast_convert_type(lax.shift_left(u, jnp.uint32(16)), F32)
    hi = lax.bitcast_convert_type(jnp.bitwise_and(u, jnp.uint32(0xFFFF0000)), F32)
    return lo, hi


def _route(h, wr_hi_ref, wr_lo_ref):
    h_hi, h_lo = _split_bf16(h)
    logits = _dot(h_hi, wr_hi_ref[...]) + (_dot(h_hi, wr_lo_ref[...]) + _dot(h_lo, wr_hi_ref[...]))
    lane = lax.broadcasted_iota(jnp.int32, logits.shape, 1)
    neg = jnp.float32(-jnp.inf)
    big = jnp.int32(1 << 20)
    is_group = (lane >= N_EXPERTS) & (lane < N_EXPERTS + N_EXPERT_GROUPS)
    gl = jnp.where(is_group, logits, neg)
    gmax = jnp.max(gl, axis=-1, keepdims=True)
    p_sel = 1.0 / jnp.sum(jnp.exp(gl - gmax), axis=-1, keepdims=True)
    g_idx = jnp.min(jnp.where(gl == gmax, lane - N_EXPERTS, big), axis=-1, keepdims=True)
    in_group = (lane < N_EXPERTS) & (jnp.right_shift(lane, 3) == g_idx)
    el = jnp.where(in_group, logits, neg)
    v1 = jnp.max(el, axis=-1, keepdims=True)
    i1 = jnp.min(jnp.where(el == v1, lane, big), axis=-1, keepdims=True)
    el2 = jnp.where(lane == i1, neg, el)
    v2 = jnp.max(el2, axis=-1, keepdims=True)
    i2 = jnp.min(jnp.where(el2 == v2, lane, big), axis=-1, keepdims=True)
    e2 = jnp.exp(v2 - v1)
    w1 = p_sel / (1.0 + e2)
    w2 = p_sel * e2 / (1.0 + e2)
    return i1, i2, w1, w2


def _router_kernel(x_ref, gain_ref, wr_hi_ref, wr_lo_ref, tri_ref, hpk_ref, meta_ref, wts_ref,
                   cnt_out_ref, cnt_ref):
    @pl.when(pl.program_id(0) == 0)
    def _():
        cnt_ref[...] = jnp.zeros_like(cnt_ref)

    x = x_ref[...]
    ms = jnp.mean(x * x, axis=-1, keepdims=True)
    h = (x * lax.rsqrt(ms + RMS_EPS)) * gain_ref[...]
    hpk_ref[...] = _pack_bf16_pairs(h)
    i1, i2, w1, w2 = _route(h, wr_hi_ref, wr_lo_ref)
    lane = lax.broadcasted_iota(jnp.int32, (ROUTE_ROWS, ROUTER_LANES), 1)
    assigned = ((lane == i1) | (lane == i2)).astype(F32)
    earlier = cnt_ref[0:1, :] + _dot(tri_ref[...], assigned.astype(BF16))
    rank1 = jnp.sum(jnp.where(lane == i1, earlier, 0.0), axis=-1, keepdims=True).astype(jnp.int32)
    rank2 = jnp.sum(jnp.where(lane == i2, earlier, 0.0), axis=-1, keepdims=True).astype(jnp.int32)
    meta_ref[...] = jnp.where(lane == 0, i1, jnp.where(lane == 1, i2, jnp.where(
        lane == 2, rank1, jnp.where(lane == 3, rank2, 0))))
    wts_ref[...] = jnp.where(lane == 0, w1, jnp.where(lane == 1, w2, 0.0))
    cnt_ref[...] = cnt_ref[...] + jnp.sum(assigned, axis=0, keepdims=True)
    cnt_out_ref[...] = cnt_ref[...]


def _router(x2d, gain, wr_hi, wr_lo, tri):
    n = x2d.shape[0]
    assert n % ROUTE_ROWS == 0
    const2 = lambda i: (0, 0)
    rows = lambda i: (i, 0)
    return pl.pallas_call(
        _router_kernel,
        grid=(n // ROUTE_ROWS,),
        in_specs=[
            pl.BlockSpec((ROUTE_ROWS, D_MODEL), rows),
            pl.BlockSpec((1, D_MODEL), const2),
            pl.BlockSpec((D_MODEL, ROUTER_LANES), const2),
            pl.BlockSpec((D_MODEL, ROUTER_LANES), const2),
            pl.BlockSpec((ROUTE_ROWS, ROUTE_ROWS), const2),
        ],
        out_specs=[
            pl.BlockSpec((ROUTE_ROWS, PACKED), rows),
            pl.BlockSpec((ROUTE_ROWS, ROUTER_LANES), rows),
            pl.BlockSpec((ROUTE_ROWS, ROUTER_LANES), rows),
            pl.BlockSpec((8, ROUTER_LANES), const2),
        ],
        out_shape=[
            jax.ShapeDtypeStruct((n, PACKED), jnp.uint32),
            jax.ShapeDtypeStruct((n, ROUTER_LANES), jnp.int32),
            jax.ShapeDtypeStruct((n, ROUTER_LANES), F32),
            jax.ShapeDtypeStruct((8, ROUTER_LANES), F32),
        ],
        scratch_shapes=[pltpu.VMEM((8, ROUTER_LANES), F32)],
        compiler_params=pltpu.CompilerParams(
            dimension_semantics=("arbitrary",), vmem_limit_bytes=VMEM_LIMIT_BYTES),
        name="router",
    )(x2d, gain, wr_hi, wr_lo, tri)


def _row_copy(src_ref, src_row, dst_ref, dst_row, sem):
    return pltpu.make_async_copy(src_ref.at[pl.ds(src_row, 1)], dst_ref.at[pl.ds(dst_row, 1)], sem)


def _scatter_kernel(pos1_ref, pos2_ref, hpk_ref, sorted_in_ref, sorted_ref, sem):
    del sorted_in_ref

    def issue(i, carry):
        _row_copy(hpk_ref, i, sorted_ref, pos1_ref[i], sem).start()
        _row_copy(hpk_ref, i, sorted_ref, pos2_ref[i], sem).start()
        return carry

    lax.fori_loop(0, SCATTER_ROWS, issue, 0, unroll=DMA_UNROLL)

    def drain(i, carry):
        _row_copy(hpk_ref, 0, sorted_ref, 0, sem).wait()
        _row_copy(hpk_ref, 0, sorted_ref, 0, sem).wait()
        return carry

    lax.fori_loop(0, SCATTER_ROWS, drain, 0, unroll=DMA_UNROLL)


def _scatter(pos1, pos2, hpk, sorted_zeros):
    n = hpk.shape[0]
    assert n % SCATTER_ROWS == 0
    idx = lambda i: (i,)
    return pl.pallas_call(
        _scatter_kernel,
        grid=(n // SCATTER_ROWS,),
        in_specs=[
            pl.BlockSpec((SCATTER_ROWS,), idx, memory_space=pltpu.SMEM),
            pl.BlockSpec((SCATTER_ROWS,), idx, memory_space=pltpu.SMEM),
            pl.BlockSpec((SCATTER_ROWS, PACKED), lambda i: (i, 0)),
            pl.BlockSpec(memory_space=pl.ANY),
        ],
        out_specs=pl.BlockSpec(memory_space=pl.ANY),
        out_shape=jax.ShapeDtypeStruct(sorted_zeros.shape, jnp.uint32),
        scratch_shapes=[pltpu.SemaphoreType.DMA],
        input_output_aliases={3: 0},
        compiler_params=pltpu.CompilerParams(
            dimension_semantics=("arbitrary",), vmem_limit_bytes=VMEM_LIMIT_BYTES),
        name="scatter_rows",
    )(pos1, pos2, hpk, sorted_zeros)


def _expert_kernel(tile_expert_ref, n_used_ref, lhs_ref, wgu_ref, wd_ref, o_ref):
    del tile_expert_ref
    used = pl.program_id(0) < n_used_ref[0]

    @pl.when(used)
    def _():
        lo, hi = _unpack_bf16_pairs(lhs_ref[...])
        hh = jnp.concatenate([lo.astype(BF16), hi.astype(BF16)], axis=1)
        gu = _dot(hh, wgu_ref[0])
        hid = _silu(gu[:, :D_EXPERT]) * gu[:, D_EXPERT:]
        o_ref[...] = _pack_bf16_pairs(_dot(hid.astype(BF16), wd_ref[0]))

    @pl.when(jnp.logical_not(used))
    def _():
        o_ref[...] = jnp.zeros_like(o_ref)


def _experts(tile_expert, n_used, sorted_rows, w_gu, w_down):
    n_tiles = sorted_rows.shape[0] // EXPERT_TILE
    live = lambda i, te, nu: jnp.minimum(i, nu[0] - 1)
    return pl.pallas_call(
        _expert_kernel,
        grid_spec=pltpu.PrefetchScalarGridSpec(
            num_scalar_prefetch=2,
            grid=(n_tiles,),
            in_specs=[
                pl.BlockSpec((EXPERT_TILE, PACKED), lambda i, te, nu: (live(i, te, nu), 0)),
                pl.BlockSpec((1, D_MODEL, 2 * D_EXPERT), lambda i, te, nu: (te[live(i, te, nu)], 0, 0)),
                pl.BlockSpec((1, D_EXPERT, D_MODEL), lambda i, te, nu: (te[live(i, te, nu)], 0, 0)),
            ],
            out_specs=pl.BlockSpec((EXPERT_TILE, PACKED), lambda i, te, nu: (i, 0)),
        ),
        out_shape=jax.ShapeDtypeStruct(sorted_rows.shape, jnp.uint32),
        compiler_params=pltpu.CompilerParams(
            dimension_semantics=("arbitrary",), vmem_limit_bytes=VMEM_LIMIT_BYTES),
        name="experts",
    )(tile_expert, n_used, sorted_rows, w_gu, w_down)


def _combine_kernel(pos1_ref, pos2_ref, x_ref, wts_ref, ys_ref, gfin_ref, o_ref, buf1_ref, buf2_ref,
                    sem, *, final_norm):
    def issue(i, carry):
        _row_copy(ys_ref, pos1_ref[i], buf1_ref, i, sem).start()
        _row_copy(ys_ref, pos2_ref[i], buf2_ref, i, sem).start()
        return carry

    lax.fori_loop(0, COMBINE_ROWS, issue, 0, unroll=DMA_UNROLL)

    def drain(i, carry):
        _row_copy(ys_ref, 0, buf1_ref, 0, sem).wait()
        _row_copy(ys_ref, 0, buf2_ref, 0, sem).wait()
        return carry

    lax.fori_loop(0, COMBINE_ROWS, drain, 0, unroll=DMA_UNROLL)
    lo1, hi1 = _unpack_bf16_pairs(buf1_ref[...])
    lo2, hi2 = _unpack_bf16_pairs(buf2_ref[...])
    w1 = wts_ref[:, 0:1]
    w2 = wts_ref[:, 1:2]
    x = x_ref[...]
    y = jnp.concatenate([x[:, :PACKED] + (w1 * lo1 + w2 * lo2),
                         x[:, PACKED:] + (w1 * hi1 + w2 * hi2)], axis=1)
    if final_norm:
        ms = jnp.mean(y * y, axis=-1, keepdims=True)
        y = (y * lax.rsqrt(ms + RMS_EPS)) * gfin_ref[...]
    o_ref[...] = y


def _combine(pos1, pos2, x2d, wts, ys, gain_final, final_norm):
    n = x2d.shape[0]
    assert n % COMBINE_ROWS == 0
    idx = lambda i: (i,)
    rows = lambda i: (i, 0)
    return pl.pallas_call(
        functools.partial(_combine_kernel, final_norm=final_norm),
        grid=(n // COMBINE_ROWS,),
        in_specs=[
            pl.BlockSpec((COMBINE_ROWS,), idx, memory_space=pltpu.SMEM),
            pl.BlockSpec((COMBINE_ROWS,), idx, memory_space=pltpu.SMEM),
            pl.BlockSpec((COMBINE_ROWS, D_MODEL), rows),
            pl.BlockSpec((COMBINE_ROWS, ROUTER_LANES), rows),
            pl.BlockSpec(memory_space=pl.ANY),
            pl.BlockSpec((1, D_MODEL), lambda i: (0, 0)),
        ],
        out_specs=pl.BlockSpec((COMBINE_ROWS, D_MODEL), rows),
        out_shape=jax.ShapeDtypeStruct(x2d.shape, F32),
        scratch_shapes=[
            pltpu.VMEM((COMBINE_ROWS, PACKED), jnp.uint32),
            pltpu.VMEM((COMBINE_ROWS, PACKED), jnp.uint32),
            pltpu.SemaphoreType.DMA,
        ],
        compiler_params=pltpu.CompilerParams(
            dimension_semantics=("arbitrary",), vmem_limit_bytes=VMEM_LIMIT_BYTES),
        name="combine",
    )(pos1, pos2, x2d, wts, ys, gain_final)


def _moe(x2d, gain, wr_hi, wr_lo, tri, w_gu, w_down, gain_final, final_norm):
    n = x2d.shape[0]
    max_tiles = (2 * n) // EXPERT_TILE + N_EXPERTS
    hpk, meta, wts, cnt = _router(x2d, gain, wr_hi, wr_lo, tri)
    counts = cnt[0, :N_EXPERTS].astype(jnp.int32)
    tiles_per = (counts + (EXPERT_TILE - 1)) // EXPERT_TILE
    tile_end = jnp.cumsum(tiles_per)
    row_start = (tile_end - tiles_per) * EXPERT_TILE
    tile_ids = jnp.arange(max_tiles, dtype=jnp.int32)
    tile_expert = jnp.minimum(
        jnp.sum((tile_ids[:, None] >= tile_end[None, :]).astype(jnp.int32), axis=1), N_EXPERTS - 1)
    n_used = tile_end[-1:].astype(jnp.int32)
    pos1 = row_start[meta[:, 0]] + meta[:, 2]
    pos2 = row_start[meta[:, 1]] + meta[:, 3]
    sorted_rows = _scatter(pos1, pos2, hpk, jnp.zeros((max_tiles * EXPERT_TILE, PACKED), jnp.uint32))
    ys = _experts(tile_expert, n_used, sorted_rows, w_gu, w_down)
    return _combine(pos1, pos2, x2d, wts, ys, gain_final, final_norm)


def kernel(x, lb_logits, norm_mix, w_in, hgrn_norm, tmlp_ln_g, tmlp_ln_b, w_spatial, b_spatial,
           w_out, norm_ffn, w_router_group, w_router_expert, w_gate, w_up, w_down, norm_final):
    depth = w_in.shape[0]
    bsz, seq, _ = x.shape
    p = jax.nn.softmax(lb_logits.astype(F32), axis=0)
    lower_bounds = jnp.cumsum(p, axis=0) - p[0:1]
    tril = jnp.tril(jnp.ones((SUB, SUB), dtype=bool))
    earlier = jnp.tril(jnp.ones((ROUTE_ROWS, ROUTE_ROWS), BF16), -1)
    for layer in range(depth):
        w_sp = jnp.where(tril[None], w_spatial[layer], 0.0).astype(BF16)
        b_sp = jnp.repeat(b_spatial[layer].T, GROUP_DIM, axis=1)
        x = _mixer(x, norm_mix[layer][None], w_in[layer].astype(BF16), lower_bounds[layer][None],
                   hgrn_norm[layer][None], tmlp_ln_g[layer][None], tmlp_ln_b[layer][None],
                   w_sp, b_sp, w_out[layer].astype(BF16))
        w_r = jnp.concatenate([w_router_expert[layer], w_router_group[layer]], axis=1)
        w_r = jnp.pad(w_r, ((0, 0), (0, ROUTER_LANES - w_r.shape[1])))
        wr_hi, wr_lo = _split_bf16(w_r)
        w_gu = jnp.concatenate([w_gate[layer], w_up[layer]], axis=2).astype(BF16)
        x2d = _moe(x.reshape(bsz * seq, D_MODEL), norm_ffn[layer][None], wr_hi, wr_lo, earlier,
                   w_gu, w_down[layer].astype(BF16), norm_final[None],
                   final_norm=(layer == depth - 1))
        x = x2d.reshape(bsz, seq, D_MODEL)
    return x
```

```python
import functools

import jax
import jax.numpy as jnp
from jax import lax
from jax.experimental import pallas as pl
from jax.experimental.pallas import tpu as pltpu

F32 = jnp.float32
BF16 = jnp.bfloat16

D_MODEL = 1024
N_HEADS = 4
HEAD_DIM = 128
D_HGRN = N_HEADS * HEAD_DIM
N_GROUPS = 4
GROUP_DIM = 128
D_TMLP = N_GROUPS * GROUP_DIM
D_IN = 4 * D_HGRN + 2 * D_TMLP
SUB = 128
N_EXPERT_GROUPS = 4
EXPERTS_PER_GROUP = 8
N_EXPERTS = N_EXPERT_GROUPS * EXPERTS_PER_GROUP
D_EXPERT = 256
ROUTER_LANES = 128
RMS_EPS = 1e-6
LN_EPS = 1e-5
F_FLOOR = 1e-30
HGRN_SAFE_EXP = 60.0
SQRT_HALF = 0.7071067811865476

MIXER_ROWS = 512
ROUTE_ROWS = 1024
SCATTER_ROWS = 2048
COMBINE_ROWS = 1024
EXPERT_TILE = 512
PACKED = D_MODEL // 2
DMA_UNROLL = 8
VMEM_LIMIT_BYTES = 56 * 1024 * 1024


def _dot(a, b):
    return jnp.dot(a, b, preferred_element_type=F32)


def _dot_nt(a, b):
    return lax.dot_general(a, b, (((1,), (1,)), ((), ())), preferred_element_type=F32)


def _dot_tn(a, b):
    return lax.dot_general(a, b, (((0,), (0,)), ((), ())), preferred_element_type=F32)


def _split_bf16(a):
    hi = a.astype(BF16)
    lo = (a - hi.astype(F32)).astype(BF16)
    return hi, lo


def _gelu(a):
    return 0.5 * a * (1.0 + lax.erf(a * SQRT_HALF))


def _silu(a):
    return a * jax.nn.sigmoid(a)


def _boundary_rows(b_ref, r0, m, width):
    pieces = []
    if 2 * m >= 8:
        for s0 in range(0, SUB, 2 * m):
            row = b_ref[pl.ds(r0 + (s0 + m - 1), 1), :]
            pieces.append(jnp.broadcast_to(row, (2 * m, width)))
    else:
        row8 = lax.broadcasted_iota(jnp.int32, (8, width), 0)
        for g0 in range(0, SUB, 8):
            acc = None
            for s0 in range(0, 8, 2 * m):
                row = jnp.broadcast_to(b_ref[pl.ds(r0 + (g0 + s0 + m - 1), 1), :], (8, width))
                acc = row if acc is None else jnp.where(row8 >= s0, row, acc)
            pieces.append(acc)
    return jnp.concatenate(pieces, axis=0)


def _half_middle_rows(b_ref, r0, width):
    half = SUB // 2
    return jnp.concatenate(
        [jnp.broadcast_to(b_ref[pl.ds(r0 + (s0 + half // 2 - 1), 1), :], (half, width))
         for s0 in range(0, SUB, half)], axis=0)


def _hgrn_level(att, q, kk, b, b_ref, r0, m, tx, row):
    ref_pt = _boundary_rows(b_ref, r0, m, D_HGRN)
    decay = jnp.exp(-jnp.abs(b - ref_pt))
    right = jnp.bitwise_and(row, m) != 0
    qt = jnp.where(right, q * decay, 0.0).astype(BF16)
    kt = jnp.where(right, 0.0, kk * decay).astype(BF16)
    same_block = tx < 2 * m
    out = []
    for h in range(N_HEADS):
        sl = slice(h * HEAD_DIM, (h + 1) * HEAD_DIM)
        term = jnp.where(same_block, _dot_nt(qt[:, sl], kt[:, sl]), 0.0)
        out.append(term if att is None else att[h] + term)
    return out


def _hgrn_attention(q, kk, b, b_ref, r0, shared_reference):
    t_idx = lax.broadcasted_iota(jnp.int32, (SUB, SUB), 0)
    s_idx = lax.broadcasted_iota(jnp.int32, (SUB, SUB), 1)
    tx = jnp.bitwise_xor(t_idx, s_idx)
    row = lax.broadcasted_iota(jnp.int32, (SUB, D_HGRN), 0)
    half = SUB // 2
    att = _hgrn_level(None, q, kk, b, b_ref, r0, half, tx, row)
    if shared_reference:
        expo = b - _half_middle_rows(b_ref, r0, D_HGRN)
        qt = (q * jnp.exp(expo)).astype(BF16)
        kt = (kk * jnp.exp(-expo)).astype(BF16)
        keep = (tx < half) & (s_idx <= t_idx)
        for h in range(N_HEADS):
            sl = slice(h * HEAD_DIM, (h + 1) * HEAD_DIM)
            att[h] = att[h] + jnp.where(keep, _dot_nt(qt[:, sl], kt[:, sl]), 0.0)
        return att
    qb = q.astype(BF16)
    kb = kk.astype(BF16)
    for h in range(N_HEADS):
        sl = slice(h * HEAD_DIM, (h + 1) * HEAD_DIM)
        att[h] = att[h] + jnp.where(tx == 0, _dot_nt(qb[:, sl], kb[:, sl]), 0.0)
    m = 1
    while m < half:
        att = _hgrn_level(att, q, kk, b, b_ref, r0, m, tx, row)
        m *= 2
    return att


def _hgrn_sub_chunk(z_ref, kk_ref, b_ref, hn_ref, y_ref, r0, state, shared_reference):
    rows = pl.ds(r0, SUB)
    q = _silu(z_ref[rows, 0:D_HGRN])
    kk = kk_ref[rows, :]
    b = b_ref[rows, :]
    v = z_ref[rows, 2 * D_HGRN:3 * D_HGRN].astype(BF16)
    att = _hgrn_attention(q, kk, b, b_ref, r0, shared_reference)
    b_end = b_ref[pl.ds(r0 + (SUB - 1), 1), :]
    q0 = (q * jnp.exp(b)).astype(BF16)
    k_end = (kk * jnp.exp(b_end - b)).astype(BF16)
    s_decay = jnp.exp(b_end)
    g = _silu(z_ref[rows, 3 * D_HGRN:4 * D_HGRN])
    new_state = []
    for hd in range(N_HEADS):
        sl = slice(hd * HEAD_DIM, (hd + 1) * HEAD_DIM)
        st = state[hd]
        o = _dot(att[hd].astype(BF16), v[:, sl]) + _dot_nt(q0[:, sl], st.astype(BF16))
        new_state.append(st * s_decay[:, sl] + _dot_tn(v[:, sl], k_end[:, sl]))
        oms = jnp.mean(o * o, axis=-1, keepdims=True)
        on = (o * lax.rsqrt(oms + RMS_EPS)) * hn_ref[...]
        y_ref[rows, sl] = (on * g[:, sl]).astype(BF16)
    return new_state


def _mixer_kernel(x_ref, gain_ref, win_ref, lb_ref, hn_ref, lng_ref, lnb_ref, wsp_ref, bsp_ref,
                  wout_ref, o_ref, z_ref, y_ref, b_ref, kk_ref, st_ref):
    @pl.when(pl.program_id(1) == 0)
    def _():
        st_ref[...] = jnp.zeros_like(st_ref)

    x = x_ref[0]
    ms = jnp.mean(x * x, axis=-1, keepdims=True)
    h = (x * lax.rsqrt(ms + RMS_EPS)) * gain_ref[...]
    z_ref[...] = _dot(h.astype(BF16), win_ref[...])

    t_idx = lax.broadcasted_iota(jnp.int32, (SUB, SUB), 0)
    s_idx = lax.broadcasted_iota(jnp.int32, (SUB, SUB), 1)
    tri = (s_idx <= t_idx).astype(BF16)
    n_sub = MIXER_ROWS // SUB

    worst = jnp.zeros((SUB, D_HGRN), F32)
    for c in range(n_sub):
        rows = pl.ds(c * SUB, SUB)
        lb = lb_ref[...]
        fg = lb + (1.0 - lb) * jax.nn.sigmoid(z_ref[rows, D_HGRN:2 * D_HGRN])
        lf_hi, lf_lo = _split_bf16(jnp.log(jnp.maximum(fg, F_FLOOR)))
        kk_ref[rows, :] = 1.0 - fg
        b = _dot(tri, lf_hi) + _dot(tri, lf_lo)
        b_ref[rows, :] = b
        worst = jnp.maximum(worst, jnp.abs(b - _half_middle_rows(b_ref, c * SUB, D_HGRN)))
    shared_ok = jnp.max(worst) <= HGRN_SAFE_EXP

    for c in range(n_sub):
        rows = pl.ds(c * SUB, SUB)
        u = _gelu(z_ref[rows, 4 * D_HGRN:4 * D_HGRN + D_TMLP])
        vv = _gelu(z_ref[rows, 4 * D_HGRN + D_TMLP:D_IN])
        for gi in range(N_GROUPS):
            sl = slice(gi * GROUP_DIM, (gi + 1) * GROUP_DIM)
            vg = vv[:, sl]
            mu = jnp.mean(vg, axis=-1, keepdims=True)
            cen = vg - mu
            var = jnp.mean(cen * cen, axis=-1, keepdims=True)
            vn = (cen * lax.rsqrt(var + LN_EPS)) * lng_ref[:, sl] + lnb_ref[:, sl]
            mixed = _dot(wsp_ref[gi], vn.astype(BF16)) + bsp_ref[:, sl]
            y_ref[rows, D_HGRN + gi * GROUP_DIM:D_HGRN + (gi + 1) * GROUP_DIM] = (
                u[:, sl] * mixed).astype(BF16)

    @pl.when(shared_ok)
    def _():
        state = [st_ref[hd] for hd in range(N_HEADS)]
        for c in range(n_sub):
            state = _hgrn_sub_chunk(z_ref, kk_ref, b_ref, hn_ref, y_ref, c * SUB, state, True)
        for hd in range(N_HEADS):
            st_ref[hd] = state[hd]

    @pl.when(jnp.logical_not(shared_ok))
    def _():
        def sub_chunk(c, carry):
            r0 = pl.multiple_of(c * SUB, SUB)
            state = [st_ref[hd] for hd in range(N_HEADS)]
            state = _hgrn_sub_chunk(z_ref, kk_ref, b_ref, hn_ref, y_ref, r0, state, False)
            for hd in range(N_HEADS):
                st_ref[hd] = state[hd]
            return carry

        lax.fori_loop(0, n_sub, sub_chunk, 0)

    o_ref[0] = x + _dot(y_ref[...], wout_ref[...])


def _mixer(x, gain, w_in, lb, hgrn_norm, ln_g, ln_b, w_sp, b_sp, w_out):
    bsz, seq, _ = x.shape
    assert seq % MIXER_ROWS == 0 and MIXER_ROWS % SUB == 0
    const2 = lambda b, j: (0, 0)
    const3 = lambda b, j: (0, 0, 0)
    return pl.pallas_call(
        _mixer_kernel,
        grid=(bsz, seq // MIXER_ROWS),
        in_specs=[
            pl.BlockSpec((1, MIXER_ROWS, D_MODEL), lambda b, j: (b, j, 0)),
            pl.BlockSpec((1, D_MODEL), const2),
            pl.BlockSpec((D_MODEL, D_IN), const2),
            pl.BlockSpec((1, D_HGRN), const2),
            pl.BlockSpec((1, HEAD_DIM), const2),
            pl.BlockSpec((1, D_TMLP), const2),
            pl.BlockSpec((1, D_TMLP), const2),
            pl.BlockSpec((N_GROUPS, SUB, SUB), const3),
            pl.BlockSpec((SUB, D_TMLP), const2),
            pl.BlockSpec((D_HGRN + D_TMLP, D_MODEL), const2),
        ],
        out_specs=pl.BlockSpec((1, MIXER_ROWS, D_MODEL), lambda b, j: (b, j, 0)),
        out_shape=jax.ShapeDtypeStruct(x.shape, F32),
        scratch_shapes=[
            pltpu.VMEM((MIXER_ROWS, D_IN), F32),
            pltpu.VMEM((MIXER_ROWS, D_HGRN + D_TMLP), BF16),
            pltpu.VMEM((MIXER_ROWS, D_HGRN), F32),
            pltpu.VMEM((MIXER_ROWS, D_HGRN), F32),
            pltpu.VMEM((N_HEADS, HEAD_DIM, HEAD_DIM), F32),
        ],
        compiler_params=pltpu.CompilerParams(
            dimension_semantics=("arbitrary", "arbitrary"),
            vmem_limit_bytes=VMEM_LIMIT_BYTES),
        name="mixer",
    )(x, gain, w_in, lb, hgrn_norm, ln_g, ln_b, w_sp, b_sp, w_out)


def _pack_bf16_pairs(a):
    lo = lax.bitcast_convert_type(a[:, :PACKED].astype(BF16).astype(F32), jnp.uint32)
    hi = lax.bitcast_convert_type(a[:, PACKED:].astype(BF16).astype(F32), jnp.uint32)
    return jnp.bitwise_or(lax.shift_right_logical(lo, jnp.uint32(16)),
                          jnp.bitwise_and(hi, jnp.uint32(0xFFFF0000)))


def _unpack_bf16_pairs(u):
    lo = lax.bitcast_convert_type(lax.shift_left(u, jnp.uint32(16)), F32)
    hi = lax.bitcast_convert_type(jnp.bitwise_and(u, jnp.uint32(0xFFFF0000)), F32)
    return lo, hi


def _route(h, wr_hi_ref, wr_lo_ref):
    h_hi, h_lo = _split_bf16(h)
    logits = _dot(h_hi, wr_hi_ref[...]) + (_dot(h_hi, wr_lo_ref[...]) + _dot(h_lo, wr_hi_ref[...]))
    lane = lax.broadcasted_iota(jnp.int32, logits.shape, 1)
    neg = jnp.float32(-jnp.inf)
    big = jnp.int32(1 << 20)
    is_group = (lane >= N_EXPERTS) & (lane < N_EXPERTS + N_EXPERT_GROUPS)
    gl = jnp.where(is_group, logits, neg)
    gmax = jnp.max(gl, axis=-1, keepdims=True)
    p_sel = 1.0 / jnp.sum(jnp.exp(gl - gmax), axis=-1, keepdims=True)
    g_idx = jnp.min(jnp.where(gl == gmax, lane - N_EXPERTS, big), axis=-1, keepdims=True)
    in_group = (lane < N_EXPERTS) & (jnp.right_shift(lane, 3) == g_idx)
    el = jnp.where(in_group, logits, neg)
    v1 = jnp.max(el, axis=-1, keepdims=True)
    i1 = jnp.min(jnp.where(el == v1, lane, big), axis=-1, keepdims=True)
    el2 = jnp.where(lane == i1, neg, el)
    v2 = jnp.max(el2, axis=-1, keepdims=True)
    i2 = jnp.min(jnp.where(el2 == v2, lane, big), axis=-1, keepdims=True)
    e2 = jnp.exp(v2 - v1)
    w1 = p_sel / (1.0 + e2)
    w2 = p_sel * e2 / (1.0 + e2)
    return i1, i2, w1, w2


def _router_kernel(x_ref, gain_ref, wr_hi_ref, wr_lo_ref, tri_ref, hpk_ref, meta_ref, wts_ref,
                   cnt_out_ref, cnt_ref):
    @pl.when(pl.program_id(0) == 0)
    def _():
        cnt_ref[...] = jnp.zeros_like(cnt_ref)

    x = x_ref[...]
    ms = jnp.mean(x * x, axis=-1, keepdims=True)
    h = (x * lax.rsqrt(ms + RMS_EPS)) * gain_ref[...]
    hpk_ref[...] = _pack_bf16_pairs(h)
    i1, i2, w1, w2 = _route(h, wr_hi_ref, wr_lo_ref)
    lane = lax.broadcasted_iota(jnp.int32, (ROUTE_ROWS, ROUTER_LANES), 1)
    assigned = ((lane == i1) | (lane == i2)).astype(F32)
    earlier = cnt_ref[0:1, :] + _dot(tri_ref[...], assigned.astype(BF16))
    rank1 = jnp.sum(jnp.where(lane == i1, earlier, 0.0), axis=-1, keepdims=True).astype(jnp.int32)
    rank2 = jnp.sum(jnp.where(lane == i2, earlier, 0.0), axis=-1, keepdims=True).astype(jnp.int32)
    meta_ref[...] = jnp.where(lane == 0, i1, jnp.where(lane == 1, i2, jnp.where(
        lane == 2, rank1, jnp.where(lane == 3, rank2, 0))))
    wts_ref[...] = jnp.where(lane == 0, w1, jnp.where(lane == 1, w2, 0.0))
    cnt_ref[...] = cnt_ref[...] + jnp.sum(assigned, axis=0, keepdims=True)
    cnt_out_ref[...] = cnt_ref[...]


def _router(x2d, gain, wr_hi, wr_lo, tri):
    n = x2d.shape[0]
    assert n % ROUTE_ROWS == 0
    const2 = lambda i: (0, 0)
    rows = lambda i: (i, 0)
    return pl.pallas_call(
        _router_kernel,
        grid=(n // ROUTE_ROWS,),
        in_specs=[
            pl.BlockSpec((ROUTE_ROWS, D_MODEL), rows),
            pl.BlockSpec((1, D_MODEL), const2),
            pl.BlockSpec((D_MODEL, ROUTER_LANES), const2),
            pl.BlockSpec((D_MODEL, ROUTER_LANES), const2),
            pl.BlockSpec((ROUTE_ROWS, ROUTE_ROWS), const2),
        ],
        out_specs=[
            pl.BlockSpec((ROUTE_ROWS, PACKED), rows),
            pl.BlockSpec((ROUTE_ROWS, ROUTER_LANES), rows),
            pl.BlockSpec((ROUTE_ROWS, ROUTER_LANES), rows),
            pl.BlockSpec((8, ROUTER_LANES), const2),
        ],
        out_shape=[
            jax.ShapeDtypeStruct((n, PACKED), jnp.uint32),
            jax.ShapeDtypeStruct((n, ROUTER_LANES), jnp.int32),
            jax.ShapeDtypeStruct((n, ROUTER_LANES), F32),
            jax.ShapeDtypeStruct((8, ROUTER_LANES), F32),
        ],
        scratch_shapes=[pltpu.VMEM((8, ROUTER_LANES), F32)],
        compiler_params=pltpu.CompilerParams(
            dimension_semantics=("arbitrary",), vmem_limit_bytes=VMEM_LIMIT_BYTES),
        name="router",
    )(x2d, gain, wr_hi, wr_lo, tri)


def _positions_kernel(meta_ref, row_start_ref, pos_ref):
    meta = meta_ref[...]
    lane = lax.broadcasted_iota(jnp.int32, meta.shape, 1)
    starts = row_start_ref[...]
    base1 = jnp.sum(jnp.where(lane == meta[:, 0:1], starts, 0), axis=-1, keepdims=True)
    base2 = jnp.sum(jnp.where(lane == meta[:, 1:2], starts, 0), axis=-1, keepdims=True)
    pos_ref[...] = jnp.where(lane == 0, base1 + meta[:, 2:3],
                             jnp.where(lane == 1, base2 + meta[:, 3:4], 0))


def _positions(meta, row_start_lanes):
    n = meta.shape[0]
    return pl.pallas_call(
        _positions_kernel,
        grid=(n // ROUTE_ROWS,),
        in_specs=[pl.BlockSpec((ROUTE_ROWS, ROUTER_LANES), lambda i: (i, 0)),
                  pl.BlockSpec((1, ROUTER_LANES), lambda i: (0, 0))],
        out_specs=pl.BlockSpec((ROUTE_ROWS, ROUTER_LANES), lambda i: (i, 0)),
        out_shape=jax.ShapeDtypeStruct(meta.shape, jnp.int32),
        compiler_params=pltpu.CompilerParams(
            dimension_semantics=("arbitrary",), vmem_limit_bytes=VMEM_LIMIT_BYTES),
        name="positions",
    )(meta, row_start_lanes)


def _row_copy(src_ref, src_row, dst_ref, dst_row, sem):
    return pltpu.make_async_copy(src_ref.at[pl.ds(src_row, 1)], dst_ref.at[pl.ds(dst_row, 1)], sem)


def _scatter_kernel(pos1_ref, pos2_ref, hpk_ref, sorted_in_ref, sorted_ref, sem):
    del sorted_in_ref

    def issue(i, carry):
        _row_copy(hpk_ref, i, sorted_ref, pos1_ref[i], sem).start(priority=0)
        _row_copy(hpk_ref, i, sorted_ref, pos2_ref[i], sem).start(priority=1)
        return carry

    lax.fori_loop(0, SCATTER_ROWS, issue, 0, unroll=DMA_UNROLL)

    def drain(i, carry):
        _row_copy(hpk_ref, 0, sorted_ref, 0, sem).wait()
        _row_copy(hpk_ref, 0, sorted_ref, 0, sem).wait()
        return carry

    lax.fori_loop(0, SCATTER_ROWS, drain, 0, unroll=DMA_UNROLL)


def _scatter(pos1, pos2, hpk, sorted_zeros):
    n = hpk.shape[0]
    assert n % SCATTER_ROWS == 0
    idx = lambda i: (i,)
    return pl.pallas_call(
        _scatter_kernel,
        grid=(n // SCATTER_ROWS,),
        in_specs=[
            pl.BlockSpec((SCATTER_ROWS,), idx, memory_space=pltpu.SMEM),
            pl.BlockSpec((SCATTER_ROWS,), idx, memory_space=pltpu.SMEM),
            pl.BlockSpec((SCATTER_ROWS, PACKED), lambda i: (i, 0)),
            pl.BlockSpec(memory_space=pl.ANY),
        ],
        out_specs=pl.BlockSpec(memory_space=pl.ANY),
        out_shape=jax.ShapeDtypeStruct(sorted_zeros.shape, jnp.uint32),
        scratch_shapes=[pltpu.SemaphoreType.DMA],
        input_output_aliases={3: 0},
        compiler_params=pltpu.CompilerParams(
            dimension_semantics=("arbitrary",), vmem_limit_bytes=VMEM_LIMIT_BYTES),
        name="scatter_rows",
    )(pos1, pos2, hpk, sorted_zeros)


def _expert_kernel(tile_expert_ref, n_used_ref, lhs_ref, wgu_ref, wd_ref, o_ref):
    del tile_expert_ref
    used = pl.program_id(0) < n_used_ref[0]

    @pl.when(used)
    def _():
        lo, hi = _unpack_bf16_pairs(lhs_ref[...])
        hh = jnp.concatenate([lo.astype(BF16), hi.astype(BF16)], axis=1)
        gu = _dot(hh, wgu_ref[0])
        hid = _silu(gu[:, :D_EXPERT]) * gu[:, D_EXPERT:]
        o_ref[...] = _pack_bf16_pairs(_dot(hid.astype(BF16), wd_ref[0]))

    @pl.when(jnp.logical_not(used))
    def _():
        o_ref[...] = jnp.zeros_like(o_ref)


def _experts(tile_expert, n_used, sorted_rows, w_gu, w_down):
    n_tiles = sorted_rows.shape[0] // EXPERT_TILE
    live = lambda i, te, nu: jnp.minimum(i, nu[0] - 1)
    return pl.pallas_call(
        _expert_kernel,
        grid_spec=pltpu.PrefetchScalarGridSpec(
            num_scalar_prefetch=2,
            grid=(n_tiles,),
            in_specs=[
                pl.BlockSpec((EXPERT_TILE, PACKED), lambda i, te, nu: (live(i, te, nu), 0)),
                pl.BlockSpec((1, D_MODEL, 2 * D_EXPERT), lambda i, te, nu: (te[live(i, te, nu)], 0, 0)),
                pl.BlockSpec((1, D_EXPERT, D_MODEL), lambda i, te, nu: (te[live(i, te, nu)], 0, 0)),
            ],
            out_specs=pl.BlockSpec((EXPERT_TILE, PACKED), lambda i, te, nu: (i, 0)),
        ),
        out_shape=jax.ShapeDtypeStruct(sorted_rows.shape, jnp.uint32),
        compiler_params=pltpu.CompilerParams(
            dimension_semantics=("arbitrary",), vmem_limit_bytes=VMEM_LIMIT_BYTES),
        name="experts",
    )(tile_expert, n_used, sorted_rows, w_gu, w_down)


def _combine_kernel(pos1_ref, pos2_ref, x_ref, wts_ref, ys_ref, gfin_ref, o_ref, buf1_ref, buf2_ref,
                    sem, *, final_norm):
    def issue(i, carry):
        _row_copy(ys_ref, pos1_ref[i], buf1_ref, i, sem).start(priority=0)
        _row_copy(ys_ref, pos2_ref[i], buf2_ref, i, sem).start(priority=1)
        return carry

    lax.fori_loop(0, COMBINE_ROWS, issue, 0, unroll=DMA_UNROLL)

    def drain(i, carry):
        _row_copy(ys_ref, 0, buf1_ref, 0, sem).wait()
        _row_copy(ys_ref, 0, buf2_ref, 0, sem).wait()
        return carry

    lax.fori_loop(0, COMBINE_ROWS, drain, 0, unroll=DMA_UNROLL)
    lo1, hi1 = _unpack_bf16_pairs(buf1_ref[...])
    lo2, hi2 = _unpack_bf16_pairs(buf2_ref[...])
    w1 = wts_ref[:, 0:1]
    w2 = wts_ref[:, 1:2]
    x = x_ref[...]
    y = jnp.concatenate([x[:, :PACKED] + (w1 * lo1 + w2 * lo2),
                         x[:, PACKED:] + (w1 * hi1 + w2 * hi2)], axis=1)
    if final_norm:
        ms = jnp.mean(y * y, axis=-1, keepdims=True)
        y = (y * lax.rsqrt(ms + RMS_EPS)) * gfin_ref[...]
    o_ref[...] = y


def _combine(pos1, pos2, x2d, wts, ys, gain_final, final_norm):
    n = x2d.shape[0]
    assert n % COMBINE_ROWS == 0
    idx = lambda i: (i,)
    rows = lambda i: (i, 0)
    return pl.pallas_call(
        functools.partial(_combine_kernel, final_norm=final_norm),
        grid=(n // COMBINE_ROWS,),
        in_specs=[
            pl.BlockSpec((COMBINE_ROWS,), idx, memory_space=pltpu.SMEM),
            pl.BlockSpec((COMBINE_ROWS,), idx, memory_space=pltpu.SMEM),
            pl.BlockSpec((COMBINE_ROWS, D_MODEL), rows),
            pl.BlockSpec((COMBINE_ROWS, ROUTER_LANES), rows),
            pl.BlockSpec(memory_space=pl.ANY),
            pl.BlockSpec((1, D_MODEL), lambda i: (0, 0)),
        ],
        out_specs=pl.BlockSpec((COMBINE_ROWS, D_MODEL), rows),
        out_shape=jax.ShapeDtypeStruct(x2d.shape, F32),
        scratch_shapes=[
            pltpu.VMEM((COMBINE_ROWS, PACKED), jnp.uint32),
            pltpu.VMEM((COMBINE_ROWS, PACKED), jnp.uint32),
            pltpu.SemaphoreType.DMA,
        ],
        compiler_params=pltpu.CompilerParams(
            dimension_semantics=("arbitrary",), vmem_limit_bytes=VMEM_LIMIT_BYTES),
        name="combine",
    )(pos1, pos2, x2d, wts, ys, gain_final)


def _moe(x2d, gain, wr_hi, wr_lo, tri, w_gu, w_down, gain_final, final_norm):
    n = x2d.shape[0]
    max_tiles = (2 * n) // EXPERT_TILE + N_EXPERTS
    hpk, meta, wts, cnt = _router(x2d, gain, wr_hi, wr_lo, tri)
    counts = cnt[0, :N_EXPERTS].astype(jnp.int32)
    tiles_per = (counts + (EXPERT_TILE - 1)) // EXPERT_TILE
    tile_end = jnp.cumsum(tiles_per)
    row_start = (tile_end - tiles_per) * EXPERT_TILE
    tile_ids = jnp.arange(max_tiles, dtype=jnp.int32)
    tile_expert = jnp.minimum(
        jnp.sum((tile_ids[:, None] >= tile_end[None, :]).astype(jnp.int32), axis=1), N_EXPERTS - 1)
    n_used = tile_end[-1:].astype(jnp.int32)
    pos = _positions(meta, jnp.pad(row_start, (0, ROUTER_LANES - N_EXPERTS))[None])
    pos1 = pos[:, 0]
    pos2 = pos[:, 1]
    sorted_rows = _scatter(pos1, pos2, hpk, jnp.zeros((max_tiles * EXPERT_TILE, PACKED), jnp.uint32))
    ys = _experts(tile_expert, n_used, sorted_rows, w_gu, w_down)
    return _combine(pos1, pos2, x2d, wts, ys, gain_final, final_norm)


def kernel(x, lb_logits, norm_mix, w_in, hgrn_norm, tmlp_ln_g, tmlp_ln_b, w_spatial, b_spatial,
           w_out, norm_ffn, w_router_group, w_router_expert, w_gate, w_up, w_down, norm_final):
    depth = w_in.shape[0]
    bsz, seq, _ = x.shape
    p = jax.nn.softmax(lb_logits.astype(F32), axis=0)
    lower_bounds = jnp.cumsum(p, axis=0) - p[0:1]
    tril = jnp.tril(jnp.ones((SUB, SUB), dtype=bool))
    earlier = jnp.tril(jnp.ones((ROUTE_ROWS, ROUTE_ROWS), BF16), -1)
    for layer in range(depth):
        w_sp = jnp.where(tril[None], w_spatial[layer], 0.0).astype(BF16)
        b_sp = jnp.repeat(b_spatial[layer].T, GROUP_DIM, axis=1)
        x = _mixer(x, norm_mix[layer][None], w_in[layer].astype(BF16), lower_bounds[layer][None],
                   hgrn_norm[layer][None], tmlp_ln_g[layer][None], tmlp_ln_b[layer][None],
                   w_sp, b_sp, w_out[layer].astype(BF16))
        w_r = jnp.concatenate([w_router_expert[layer], w_router_group[layer]], axis=1)
        w_r = jnp.pad(w_r, ((0, 0), (0, ROUTER_LANES - w_r.shape[1])))
        wr_hi, wr_lo = _split_bf16(w_r)
        w_gu = jnp.concatenate([w_gate[layer], w_up[layer]], axis=2).astype(BF16)
        x2d = _moe(x.reshape(bsz * seq, D_MODEL), norm_ffn[layer][None], wr_hi, wr_lo, earlier,
                   w_gu, w_down[layer].astype(BF16), norm_final[None],
                   final_norm=(layer == depth - 1))
        x = x2d.reshape(bsz, seq, D_MODEL)
    return x
```

```python
import functools

import jax
import jax.numpy as jnp
from jax import lax
from jax.experimental import pallas as pl
from jax.experimental.pallas import tpu as pltpu

F32 = jnp.float32
BF16 = jnp.bfloat16

D_MODEL = 1024
N_HEADS = 4
HEAD_DIM = 128
D_HGRN = N_HEADS * HEAD_DIM
N_GROUPS = 4
GROUP_DIM = 128
D_TMLP = N_GROUPS * GROUP_DIM
D_IN = 4 * D_HGRN + 2 * D_TMLP
SUB = 128
N_EXPERT_GROUPS = 4
EXPERTS_PER_GROUP = 8
N_EXPERTS = N_EXPERT_GROUPS * EXPERTS_PER_GROUP
D_EXPERT = 256
ROUTER_LANES = 128
RMS_EPS = 1e-6
LN_EPS = 1e-5
F_FLOOR = 1e-30
HGRN_SAFE_EXP = 60.0
SQRT_HALF = 0.7071067811865476

MIXER_ROWS = 512
ROUTE_ROWS = 1024
SCATTER_ROWS = 2048
COMBINE_ROWS = 1024
EXPERT_TILE = 512
PACKED = D_MODEL // 2
DMA_UNROLL = 8
VMEM_LIMIT_BYTES = 56 * 1024 * 1024


def _dot(a, b):
    return jnp.dot(a, b, preferred_element_type=F32)


def _dot_nt(a, b):
    return lax.dot_general(a, b, (((1,), (1,)), ((), ())), preferred_element_type=F32)


def _dot_tn(a, b):
    return lax.dot_general(a, b, (((0,), (0,)), ((), ())), preferred_element_type=F32)


def _split_bf16(a):
    hi = a.astype(BF16)
    lo = (a - hi.astype(F32)).astype(BF16)
    return hi, lo


def _gelu(a):
    return 0.5 * a * (1.0 + lax.erf(a * SQRT_HALF))


def _silu(a):
    return a * jax.nn.sigmoid(a)


def _boundary_rows(b_ref, r0, m, width):
    pieces = []
    if 2 * m >= 8:
        for s0 in range(0, SUB, 2 * m):
            row = b_ref[pl.ds(r0 + (s0 + m - 1), 1), :]
            pieces.append(jnp.broadcast_to(row, (2 * m, width)))
    else:
        row8 = lax.broadcasted_iota(jnp.int32, (8, width), 0)
        for g0 in range(0, SUB, 8):
            acc = None
            for s0 in range(0, 8, 2 * m):
                row = jnp.broadcast_to(b_ref[pl.ds(r0 + (g0 + s0 + m - 1), 1), :], (8, width))
                acc = row if acc is None else jnp.where(row8 >= s0, row, acc)
            pieces.append(acc)
    return jnp.concatenate(pieces, axis=0)


def _half_middle_rows(b_ref, r0, width):
    half = SUB // 2
    return jnp.concatenate(
        [jnp.broadcast_to(b_ref[pl.ds(r0 + (s0 + half // 2 - 1), 1), :], (half, width))
         for s0 in range(0, SUB, half)], axis=0)


def _hgrn_level(att, q, kk, b, b_ref, r0, m, tx, row):
    ref_pt = _boundary_rows(b_ref, r0, m, D_HGRN)
    decay = jnp.exp(-jnp.abs(b - ref_pt))
    right = jnp.bitwise_and(row, m) != 0
    qt = jnp.where(right, q * decay, 0.0).astype(BF16)
    kt = jnp.where(right, 0.0, kk * decay).astype(BF16)
    same_block = tx < 2 * m
    out = []
    for h in range(N_HEADS):
        sl = slice(h * HEAD_DIM, (h + 1) * HEAD_DIM)
        term = jnp.where(same_block, _dot_nt(qt[:, sl], kt[:, sl]), 0.0)
        out.append(term if att is None else att[h] + term)
    return out


def _hgrn_attention(q, kk, b, b_ref, r0, shared_reference):
    t_idx = lax.broadcasted_iota(jnp.int32, (SUB, SUB), 0)
    s_idx = lax.broadcasted_iota(jnp.int32, (SUB, SUB), 1)
    tx = jnp.bitwise_xor(t_idx, s_idx)
    row = lax.broadcasted_iota(jnp.int32, (SUB, D_HGRN), 0)
    half = SUB // 2
    att = _hgrn_level(None, q, kk, b, b_ref, r0, half, tx, row)
    if shared_reference:
        expo = b - _half_middle_rows(b_ref, r0, D_HGRN)
        qt = (q * jnp.exp(expo)).astype(BF16)
        kt = (kk * jnp.exp(-expo)).astype(BF16)
        keep = (tx < half) & (s_idx <= t_idx)
        for h in range(N_HEADS):
            sl = slice(h * HEAD_DIM, (h + 1) * HEAD_DIM)
            att[h] = att[h] + jnp.where(keep, _dot_nt(qt[:, sl], kt[:, sl]), 0.0)
        return att
    qb = q.astype(BF16)
    kb = kk.astype(BF16)
    for h in range(N_HEADS):
        sl = slice(h * HEAD_DIM, (h + 1) * HEAD_DIM)
        att[h] = att[h] + jnp.where(tx == 0, _dot_nt(qb[:, sl], kb[:, sl]), 0.0)
    m = 1
    while m < half:
        att = _hgrn_level(att, q, kk, b, b_ref, r0, m, tx, row)
        m *= 2
    return att


def _hgrn_sub_chunk(z_ref, kk_ref, b_ref, hn_ref, y_ref, r0, state, shared_reference):
    rows = pl.ds(r0, SUB)
    q = _silu(z_ref[rows, 0:D_HGRN])
    kk = kk_ref[rows, :]
    b = b_ref[rows, :]
    v = z_ref[rows, 2 * D_HGRN:3 * D_HGRN].astype(BF16)
    att = _hgrn_attention(q, kk, b, b_ref, r0, shared_reference)
    b_end = b_ref[pl.ds(r0 + (SUB - 1), 1), :]
    q0 = (q * jnp.exp(b)).astype(BF16)
    k_end = (kk * jnp.exp(b_end - b)).astype(BF16)
    s_decay = jnp.exp(b_end)
    g = _silu(z_ref[rows, 3 * D_HGRN:4 * D_HGRN])
    new_state = []
    for hd in range(N_HEADS):
        sl = slice(hd * HEAD_DIM, (hd + 1) * HEAD_DIM)
        st = state[hd]
        o = _dot(att[hd].astype(BF16), v[:, sl]) + _dot_nt(q0[:, sl], st.astype(BF16))
        new_state.append(st * s_decay[:, sl] + _dot_tn(v[:, sl], k_end[:, sl]))
        oms = jnp.mean(o * o, axis=-1, keepdims=True)
        on = (o * lax.rsqrt(oms + RMS_EPS)) * hn_ref[...]
        y_ref[rows, sl] = (on * g[:, sl]).astype(BF16)
    return new_state


def _mixer_kernel(x_ref, gain_ref, win_ref, lb_ref, hn_ref, lng_ref, lnb_ref, wsp_ref, bsp_ref,
                  wout_ref, o_ref, z_ref, y_ref, b_ref, kk_ref, st_ref):
    @pl.when(pl.program_id(1) == 0)
    def _():
        st_ref[...] = jnp.zeros_like(st_ref)

    x = x_ref[0]
    ms = jnp.mean(x * x, axis=-1, keepdims=True)
    h = (x * lax.rsqrt(ms + RMS_EPS)) * gain_ref[...]
    z_ref[...] = _dot(h.astype(BF16), win_ref[...])

    t_idx = lax.broadcasted_iota(jnp.int32, (SUB, SUB), 0)
    s_idx = lax.broadcasted_iota(jnp.int32, (SUB, SUB), 1)
    tri = (s_idx <= t_idx).astype(BF16)
    n_sub = MIXER_ROWS // SUB

    worst = jnp.zeros((SUB, D_HGRN), F32)
    for c in range(n_sub):
        rows = pl.ds(c * SUB, SUB)
        lb = lb_ref[...]
        fg = lb + (1.0 - lb) * jax.nn.sigmoid(z_ref[rows, D_HGRN:2 * D_HGRN])
        lf_hi, lf_lo = _split_bf16(jnp.log(jnp.maximum(fg, F_FLOOR)))
        kk_ref[rows, :] = 1.0 - fg
        b = _dot(tri, lf_hi) + _dot(tri, lf_lo)
        b_ref[rows, :] = b
        worst = jnp.maximum(worst, jnp.abs(b - _half_middle_rows(b_ref, c * SUB, D_HGRN)))
    shared_ok = jnp.max(worst) <= HGRN_SAFE_EXP

    for c in range(n_sub):
        rows = pl.ds(c * SUB, SUB)
        u = _gelu(z_ref[rows, 4 * D_HGRN:4 * D_HGRN + D_TMLP])
        vv = _gelu(z_ref[rows, 4 * D_HGRN + D_TMLP:D_IN])
        for gi in range(N_GROUPS):
            sl = slice(gi * GROUP_DIM, (gi + 1) * GROUP_DIM)
            vg = vv[:, sl]
            mu = jnp.mean(vg, axis=-1, keepdims=True)
            cen = vg - mu
            var = jnp.mean(cen * cen, axis=-1, keepdims=True)
            vn = (cen * lax.rsqrt(var + LN_EPS)) * lng_ref[:, sl] + lnb_ref[:, sl]
            mixed = _dot(wsp_ref[gi], vn.astype(BF16)) + bsp_ref[:, sl]
            y_ref[rows, D_HGRN + gi * GROUP_DIM:D_HGRN + (gi + 1) * GROUP_DIM] = (
                u[:, sl] * mixed).astype(BF16)

    @pl.when(shared_ok)
    def _():
        state = [st_ref[hd] for hd in range(N_HEADS)]
        for c in range(n_sub):
            state = _hgrn_sub_chunk(z_ref, kk_ref, b_ref, hn_ref, y_ref, c * SUB, state, True)
        for hd in range(N_HEADS):
            st_ref[hd] = state[hd]

    @pl.when(jnp.logical_not(shared_ok))
    def _():
        def sub_chunk(c, carry):
            r0 = pl.multiple_of(c * SUB, SUB)
            state = [st_ref[hd] for hd in range(N_HEADS)]
            state = _hgrn_sub_chunk(z_ref, kk_ref, b_ref, hn_ref, y_ref, r0, state, False)
            for hd in range(N_HEADS):
                st_ref[hd] = state[hd]
            return carry

        lax.fori_loop(0, n_sub, sub_chunk, 0)

    o_ref[0] = x + _dot(y_ref[...], wout_ref[...])


def _mixer(x, gain, w_in, lb, hgrn_norm, ln_g, ln_b, w_sp, b_sp, w_out):
    bsz, seq, _ = x.shape
    assert seq % MIXER_ROWS == 0 and MIXER_ROWS % SUB == 0
    const2 = lambda b, j: (0, 0)
    const3 = lambda b, j: (0, 0, 0)
    return pl.pallas_call(
        _mixer_kernel,
        grid=(bsz, seq // MIXER_ROWS),
        in_specs=[
            pl.BlockSpec((1, MIXER_ROWS, D_MODEL), lambda b, j: (b, j, 0)),
            pl.BlockSpec((1, D_MODEL), const2),
            pl.BlockSpec((D_MODEL, D_IN), const2),
            pl.BlockSpec((1, D_HGRN), const2),
            pl.BlockSpec((1, HEAD_DIM), const2),
            pl.BlockSpec((1, D_TMLP), const2),
            pl.BlockSpec((1, D_TMLP), const2),
            pl.BlockSpec((N_GROUPS, SUB, SUB), const3),
            pl.BlockSpec((SUB, D_TMLP), const2),
            pl.BlockSpec((D_HGRN + D_TMLP, D_MODEL), const2),
        ],
        out_specs=pl.BlockSpec((1, MIXER_ROWS, D_MODEL), lambda b, j: (b, j, 0)),
        out_shape=jax.ShapeDtypeStruct(x.shape, F32),
        scratch_shapes=[
            pltpu.VMEM((MIXER_ROWS, D_IN), F32),
            pltpu.VMEM((MIXER_ROWS, D_HGRN + D_TMLP), BF16),
            pltpu.VMEM((MIXER_ROWS, D_HGRN), F32),
            pltpu.VMEM((MIXER_ROWS, D_HGRN), F32),
            pltpu.VMEM((N_HEADS, HEAD_DIM, HEAD_DIM), F32),
        ],
        compiler_params=pltpu.CompilerParams(
            dimension_semantics=("arbitrary", "arbitrary"),
            vmem_limit_bytes=VMEM_LIMIT_BYTES),
        name="mixer",
    )(x, gain, w_in, lb, hgrn_norm, ln_g, ln_b, w_sp, b_sp, w_out)


def _pack_bf16_pairs(a):
    lo = lax.bitcast_convert_type(a[:, :PACKED].astype(BF16).astype(F32), jnp.uint32)
    hi = lax.bitcast_convert_type(a[:, PACKED:].astype(BF16).astype(F32), jnp.uint32)
    return jnp.bitwise_or(lax.shift_right_logical(lo, jnp.uint32(16)),
                          jnp.bitwise_and(hi, jnp.uint32(0xFFFF0000)))


def _unpack_bf16_pairs(u):
    lo = lax.bitcast_convert_type(lax.shift_left(u, jnp.uint32(16)), F32)
    hi = lax.bitcast_convert_type(jnp.bitwise_and(u, jnp.uint32(0xFFFF0000)), F32)
    return lo, hi


def _route(h, wr_hi_ref, wr_lo_ref):
    h_hi, h_lo = _split_bf16(h)
    logits = _dot(h_hi, wr_hi_ref[...]) + (_dot(h_hi, wr_lo_ref[...]) + _dot(h_lo, wr_hi_ref[...]))
    lane = lax.broadcasted_iota(jnp.int32, logits.shape, 1)
    neg = jnp.float32(-jnp.inf)
    big = jnp.int32(1 << 20)
    is_group = (lane >= N_EXPERTS) & (lane < N_EXPERTS + N_EXPERT_GROUPS)
    gl = jnp.where(is_group, logits, neg)
    gmax = jnp.max(gl, axis=-1, keepdims=True)
    p_sel = 1.0 / jnp.sum(jnp.exp(gl - gmax), axis=-1, keepdims=True)
    g_idx = jnp.min(jnp.where(gl == gmax, lane - N_EXPERTS, big), axis=-1, keepdims=True)
    in_group = (lane < N_EXPERTS) & (jnp.right_shift(lane, 3) == g_idx)
    el = jnp.where(in_group, logits, neg)
    v1 = jnp.max(el, axis=-1, keepdims=True)
    i1 = jnp.min(jnp.where(el == v1, lane, big), axis=-1, keepdims=True)
    el2 = jnp.where(lane == i1, neg, el)
    v2 = jnp.max(el2, axis=-1, keepdims=True)
    i2 = jnp.min(jnp.where(el2 == v2, lane, big), axis=-1, keepdims=True)
    e2 = jnp.exp(v2 - v1)
    w1 = p_sel / (1.0 + e2)
    w2 = p_sel * e2 / (1.0 + e2)
    return i1, i2, w1, w2


def _router_kernel(x_ref, gain_ref, wr_hi_ref, wr_lo_ref, tri_ref, hpk_ref, meta_ref, wts_ref,
                   cnt_out_ref, cnt_ref):
    @pl.when(pl.program_id(0) == 0)
    def _():
        cnt_ref[...] = jnp.zeros_like(cnt_ref)

    x = x_ref[...]
    ms = jnp.mean(x * x, axis=-1, keepdims=True)
    h = (x * lax.rsqrt(ms + RMS_EPS)) * gain_ref[...]
    hpk_ref[...] = _pack_bf16_pairs(h)
    i1, i2, w1, w2 = _route(h, wr_hi_ref, wr_lo_ref)
    lane = lax.broadcasted_iota(jnp.int32, (ROUTE_ROWS, ROUTER_LANES), 1)
    assigned = ((lane == i1) | (lane == i2)).astype(F32)
    earlier = cnt_ref[0:1, :] + _dot(tri_ref[...], assigned.astype(BF16))
    rank1 = jnp.sum(jnp.where(lane == i1, earlier, 0.0), axis=-1, keepdims=True).astype(jnp.int32)
    rank2 = jnp.sum(jnp.where(lane == i2, earlier, 0.0), axis=-1, keepdims=True).astype(jnp.int32)
    meta_ref[...] = jnp.where(lane == 0, i1, jnp.where(lane == 1, i2, jnp.where(
        lane == 2, rank1, jnp.where(lane == 3, rank2, 0))))
    wts_ref[...] = jnp.where(lane == 0, w1, jnp.where(lane == 1, w2, 0.0))
    cnt_ref[...] = cnt_ref[...] + jnp.sum(assigned, axis=0, keepdims=True)
    cnt_out_ref[...] = cnt_ref[...]


def _router(x2d, gain, wr_hi, wr_lo, tri):
    n = x2d.shape[0]
    assert n % ROUTE_ROWS == 0
    const2 = lambda i: (0, 0)
    rows = lambda i: (i, 0)
    return pl.pallas_call(
        _router_kernel,
        grid=(n // ROUTE_ROWS,),
        in_specs=[
            pl.BlockSpec((ROUTE_ROWS, D_MODEL), rows),
            pl.BlockSpec((1, D_MODEL), const2),
            pl.BlockSpec((D_MODEL, ROUTER_LANES), const2),
            pl.BlockSpec((D_MODEL, ROUTER_LANES), const2),
            pl.BlockSpec((ROUTE_ROWS, ROUTE_ROWS), const2),
        ],
        out_specs=[
            pl.BlockSpec((ROUTE_ROWS, PACKED), rows),
            pl.BlockSpec((ROUTE_ROWS, ROUTER_LANES), rows),
            pl.BlockSpec((ROUTE_ROWS, ROUTER_LANES), rows),
            pl.BlockSpec((8, ROUTER_LANES), const2),
        ],
        out_shape=[
            jax.ShapeDtypeStruct((n, PACKED), jnp.uint32),
            jax.ShapeDtypeStruct((n, ROUTER_LANES), jnp.int32),
            jax.ShapeDtypeStruct((n, ROUTER_LANES), F32),
            jax.ShapeDtypeStruct((8, ROUTER_LANES), F32),
        ],
        scratch_shapes=[pltpu.VMEM((8, ROUTER_LANES), F32)],
        compiler_params=pltpu.CompilerParams(
            dimension_semantics=("arbitrary",), vmem_limit_bytes=VMEM_LIMIT_BYTES),
        name="router",
    )(x2d, gain, wr_hi, wr_lo, tri)


def _positions_kernel(meta_ref, row_start_ref, pos_ref):
    meta = meta_ref[...]
    lane = lax.broadcasted_iota(jnp.int32, meta.shape, 1)
    starts = row_start_ref[...]
    base1 = jnp.sum(jnp.where(lane == meta[:, 0:1], starts, 0), axis=-1, keepdims=True)
    base2 = jnp.sum(jnp.where(lane == meta[:, 1:2], starts, 0), axis=-1, keepdims=True)
    pos_ref[...] = jnp.where(lane == 0, base1 + meta[:, 2:3],
                             jnp.where(lane == 1, base2 + meta[:, 3:4], 0))


def _positions(meta, row_start_lanes):
    n = meta.shape[0]
    return pl.pallas_call(
        _positions_kernel,
        grid=(n // ROUTE_ROWS,),
        in_specs=[pl.BlockSpec((ROUTE_ROWS, ROUTER_LANES), lambda i: (i, 0)),
                  pl.BlockSpec((1, ROUTER_LANES), lambda i: (0, 0))],
        out_specs=pl.BlockSpec((ROUTE_ROWS, ROUTER_LANES), lambda i: (i, 0)),
        out_shape=jax.ShapeDtypeStruct(meta.shape, jnp.int32),
        compiler_params=pltpu.CompilerParams(
            dimension_semantics=("arbitrary",), vmem_limit_bytes=VMEM_LIMIT_BYTES),
        name="positions",
    )(meta, row_start_lanes)


def _row_copy(src_ref, src_row, dst_ref, dst_row, sem):
    return pltpu.make_async_copy(src_ref.at[pl.ds(src_row, 1)], dst_ref.at[pl.ds(dst_row, 1)], sem)


def _scatter_kernel(pos1_ref, pos2_ref, hpk_ref, sorted_ref, sem):
    def issue(i, carry):
        _row_copy(hpk_ref, i, sorted_ref, pos1_ref[i], sem).start(priority=0)
        _row_copy(hpk_ref, i, sorted_ref, pos2_ref[i], sem).start(priority=1)
        return carry

    lax.fori_loop(0, SCATTER_ROWS, issue, 0, unroll=DMA_UNROLL)

    def drain(i, carry):
        _row_copy(hpk_ref, 0, sorted_ref, 0, sem).wait()
        _row_copy(hpk_ref, 0, sorted_ref, 0, sem).wait()
        return carry

    lax.fori_loop(0, SCATTER_ROWS, drain, 0, unroll=DMA_UNROLL)


def _scatter(pos1, pos2, hpk, n_rows):
    n = hpk.shape[0]
    assert n % SCATTER_ROWS == 0
    idx = lambda i: (i,)
    return pl.pallas_call(
        _scatter_kernel,
        grid=(n // SCATTER_ROWS,),
        in_specs=[
            pl.BlockSpec((SCATTER_ROWS,), idx, memory_space=pltpu.SMEM),
            pl.BlockSpec((SCATTER_ROWS,), idx, memory_space=pltpu.SMEM),
            pl.BlockSpec((SCATTER_ROWS, PACKED), lambda i: (i, 0)),
        ],
        out_specs=pl.BlockSpec(memory_space=pl.ANY),
        out_shape=jax.ShapeDtypeStruct((n_rows, PACKED), jnp.uint32),
        scratch_shapes=[pltpu.SemaphoreType.DMA],
        compiler_params=pltpu.CompilerParams(
            dimension_semantics=("arbitrary",), vmem_limit_bytes=VMEM_LIMIT_BYTES),
        name="scatter_rows",
    )(pos1, pos2, hpk)


def _expert_kernel(tile_expert_ref, tile_rows_ref, n_used_ref, lhs_ref, wg_ref, wu_ref, wd_ref,
                   o_ref):
    del tile_expert_ref, n_used_ref
    n_valid = tile_rows_ref[pl.program_id(0)]

    @pl.when(n_valid > 0)
    def _():
        valid = lax.broadcasted_iota(jnp.int32, (EXPERT_TILE, 1), 0) < n_valid
        lo, hi = _unpack_bf16_pairs(lhs_ref[...])
        hh = jnp.concatenate([jnp.where(valid, lo, 0.0).astype(BF16),
                              jnp.where(valid, hi, 0.0).astype(BF16)], axis=1)
        w_gu = jnp.concatenate([wg_ref[0].astype(BF16), wu_ref[0].astype(BF16)], axis=1)
        gu = _dot(hh, w_gu)
        hid = _silu(gu[:, :D_EXPERT]) * gu[:, D_EXPERT:]
        o_ref[...] = _pack_bf16_pairs(_dot(hid.astype(BF16), wd_ref[0].astype(BF16)))

    @pl.when(n_valid <= 0)
    def _():
        o_ref[...] = jnp.zeros_like(o_ref)


def _experts(tile_expert, tile_rows, n_used, sorted_rows, w_gate, w_up, w_down):
    n_tiles = sorted_rows.shape[0] // EXPERT_TILE
    live = lambda i, nu: jnp.minimum(i, nu[0] - 1)
    expert = lambda i, te, tr, nu: (te[live(i, nu)], 0, 0)
    return pl.pallas_call(
        _expert_kernel,
        grid_spec=pltpu.PrefetchScalarGridSpec(
            num_scalar_prefetch=3,
            grid=(n_tiles,),
            in_specs=[
                pl.BlockSpec((EXPERT_TILE, PACKED), lambda i, te, tr, nu: (live(i, nu), 0)),
                pl.BlockSpec((1, D_MODEL, D_EXPERT), expert),
                pl.BlockSpec((1, D_MODEL, D_EXPERT), expert),
                pl.BlockSpec((1, D_EXPERT, D_MODEL), expert),
            ],
            out_specs=pl.BlockSpec((EXPERT_TILE, PACKED), lambda i, te, tr, nu: (i, 0)),
        ),
        out_shape=jax.ShapeDtypeStruct(sorted_rows.shape, jnp.uint32),
        compiler_params=pltpu.CompilerParams(
            dimension_semantics=("arbitrary",), vmem_limit_bytes=VMEM_LIMIT_BYTES),
        name="experts",
    )(tile_expert, tile_rows, n_used, sorted_rows, w_gate, w_up, w_down)


def _combine_kernel(pos1_ref, pos2_ref, x_ref, wts_ref, ys_ref, gfin_ref, o_ref, buf1_ref, buf2_ref,
                    sem, *, final_norm):
    def issue(i, carry):
        _row_copy(ys_ref, pos1_ref[i], buf1_ref, i, sem).start(priority=0)
        _row_copy(ys_ref, pos2_ref[i], buf2_ref, i, sem).start(priority=1)
        return carry

    lax.fori_loop(0, COMBINE_ROWS, issue, 0, unroll=DMA_UNROLL)

    def drain(i, carry):
        _row_copy(ys_ref, 0, buf1_ref, 0, sem).wait()
        _row_copy(ys_ref, 0, buf2_ref, 0, sem).wait()
        return carry

    lax.fori_loop(0, COMBINE_ROWS, drain, 0, unroll=DMA_UNROLL)
    lo1, hi1 = _unpack_bf16_pairs(buf1_ref[...])
    lo2, hi2 = _unpack_bf16_pairs(buf2_ref[...])
    w1 = wts_ref[:, 0:1]
    w2 = wts_ref[:, 1:2]
    x = x_ref[...]
    y = jnp.concatenate([x[:, :PACKED] + (w1 * lo1 + w2 * lo2),
                         x[:, PACKED:] + (w1 * hi1 + w2 * hi2)], axis=1)
    if final_norm:
        ms = jnp.mean(y * y, axis=-1, keepdims=True)
        y = (y * lax.rsqrt(ms + RMS_EPS)) * gfin_ref[...]
    o_ref[...] = y


def _combine(pos1, pos2, x2d, wts, ys, gain_final, final_norm):
    n = x2d.shape[0]
    assert n % COMBINE_ROWS == 0
    idx = lambda i: (i,)
    rows = lambda i: (i, 0)
    return pl.pallas_call(
        functools.partial(_combine_kernel, final_norm=final_norm),
        grid=(n // COMBINE_ROWS,),
        in_specs=[
            pl.BlockSpec((COMBINE_ROWS,), idx, memory_space=pltpu.SMEM),
            pl.BlockSpec((COMBINE_ROWS,), idx, memory_space=pltpu.SMEM),
            pl.BlockSpec((COMBINE_ROWS, D_MODEL), rows),
            pl.BlockSpec((COMBINE_ROWS, ROUTER_LANES), rows),
            pl.BlockSpec(memory_space=pl.ANY),
            pl.BlockSpec((1, D_MODEL), lambda i: (0, 0)),
        ],
        out_specs=pl.BlockSpec((COMBINE_ROWS, D_MODEL), rows),
        out_shape=jax.ShapeDtypeStruct(x2d.shape, F32),
        scratch_shapes=[
            pltpu.VMEM((COMBINE_ROWS, PACKED), jnp.uint32),
            pltpu.VMEM((COMBINE_ROWS, PACKED), jnp.uint32),
            pltpu.SemaphoreType.DMA,
        ],
        compiler_params=pltpu.CompilerParams(
            dimension_semantics=("arbitrary",), vmem_limit_bytes=VMEM_LIMIT_BYTES),
        name="combine",
    )(pos1, pos2, x2d, wts, ys, gain_final)


def _moe(x2d, gain, wr_hi, wr_lo, tri, w_gate, w_up, w_down, gain_final, final_norm):
    n = x2d.shape[0]
    max_tiles = (2 * n) // EXPERT_TILE + N_EXPERTS
    hpk, meta, wts, cnt = _router(x2d, gain, wr_hi, wr_lo, tri)
    counts = cnt[0, :N_EXPERTS].astype(jnp.int32)
    tiles_per = (counts + (EXPERT_TILE - 1)) // EXPERT_TILE
    tile_end = jnp.cumsum(tiles_per)
    tile_start = tile_end - tiles_per
    row_start = tile_start * EXPERT_TILE
    tile_ids = jnp.arange(max_tiles, dtype=jnp.int32)
    tile_expert = jnp.minimum(
        jnp.sum((tile_ids[:, None] >= tile_end[None, :]).astype(jnp.int32), axis=1), N_EXPERTS - 1)
    n_used = tile_end[-1:].astype(jnp.int32)
    rows_left = counts[tile_expert] - (tile_ids - tile_start[tile_expert]) * EXPERT_TILE
    tile_rows = jnp.where(tile_ids < n_used[0], jnp.clip(rows_left, 0, EXPERT_TILE), 0)
    pos = _positions(meta, jnp.pad(row_start, (0, ROUTER_LANES - N_EXPERTS))[None])
    pos1 = pos[:, 0]
    pos2 = pos[:, 1]
    sorted_rows = _scatter(pos1, pos2, hpk, max_tiles * EXPERT_TILE)
    ys = _experts(tile_expert, tile_rows.astype(jnp.int32), n_used, sorted_rows, w_gate, w_up, w_down)
    return _combine(pos1, pos2, x2d, wts, ys, gain_final, final_norm)


def kernel(x, lb_logits, norm_mix, w_in, hgrn_norm, tmlp_ln_g, tmlp_ln_b, w_spatial, b_spatial,
           w_out, norm_ffn, w_router_group, w_router_expert, w_gate, w_up, w_down, norm_final):
    depth = w_in.shape[0]
    bsz, seq, _ = x.shape
    p = jax.nn.softmax(lb_logits.astype(F32), axis=0)
    lower_bounds = jnp.cumsum(p, axis=0) - p[0:1]
    tril = jnp.tril(jnp.ones((SUB, SUB), dtype=bool))
    earlier = jnp.tril(jnp.ones((ROUTE_ROWS, ROUTE_ROWS), BF16), -1)
    for layer in range(depth):
        w_sp = jnp.where(tril[None], w_spatial[layer], 0.0).astype(BF16)
        b_sp = jnp.repeat(b_spatial[layer].T, GROUP_DIM, axis=1)
        x = _mixer(x, norm_mix[layer][None], w_in[layer].astype(BF16), lower_bounds[layer][None],
                   hgrn_norm[layer][None], tmlp_ln_g[layer][None], tmlp_ln_b[layer][None],
                   w_sp, b_sp, w_out[layer].astype(BF16))
        w_r = jnp.concatenate([w_router_expert[layer], w_router_group[layer]], axis=1)
        w_r = jnp.pad(w_r, ((0, 0), (0, ROUTER_LANES - w_r.shape[1])))
        wr_hi, wr_lo = _split_bf16(w_r)
        x2d = _moe(x.reshape(bsz * seq, D_MODEL), norm_ffn[layer][None], wr_hi, wr_lo, earlier,
                   w_gate[layer], w_up[layer], w_down[layer], norm_final[None],
                   final_norm=(layer == depth - 1))
        x = x2d.reshape(bsz, seq, D_MODEL)
    return x
```

```python
import functools

import jax
import jax.numpy as jnp
from jax import lax
from jax.experimental import pallas as pl
from jax.experimental.pallas import tpu as pltpu

F32 = jnp.float32
BF16 = jnp.bfloat16

D_MODEL = 1024
N_HEADS = 4
HEAD_DIM = 128
D_HGRN = N_HEADS * HEAD_DIM
N_GROUPS = 4
GROUP_DIM = 128
D_TMLP = N_GROUPS * GROUP_DIM
D_IN = 4 * D_HGRN + 2 * D_TMLP
SUB = 128
N_EXPERT_GROUPS = 4
EXPERTS_PER_GROUP = 8
N_EXPERTS = N_EXPERT_GROUPS * EXPERTS_PER_GROUP
D_EXPERT = 256
ROUTER_LANES = 128
RMS_EPS = 1e-6
LN_EPS = 1e-5
F_FLOOR = 1e-30
HGRN_SAFE_EXP = 60.0
SQRT_HALF = 0.7071067811865476

MIXER_ROWS = 512
ROUTE_ROWS = 1024
SCATTER_ROWS = 2048
COMBINE_ROWS = 1024
EXPERT_TILE = 512
PACKED = D_MODEL // 2
DMA_UNROLL = 8
VMEM_LIMIT_BYTES = 56 * 1024 * 1024


def _dot(a, b):
    return jnp.dot(a, b, preferred_element_type=F32)


def _dot_nt(a, b):
    return lax.dot_general(a, b, (((1,), (1,)), ((), ())), preferred_element_type=F32)


def _dot_tn(a, b):
    return lax.dot_general(a, b, (((0,), (0,)), ((), ())), preferred_element_type=F32)


def _split_bf16(a):
    hi = a.astype(BF16)
    lo = (a - hi.astype(F32)).astype(BF16)
    return hi, lo


def _gelu(a):
    return 0.5 * a * (1.0 + lax.erf(a * SQRT_HALF))


def _silu(a):
    return a * jax.nn.sigmoid(a)


def _boundary_rows(b_ref, r0, m, width):
    pieces = []
    if 2 * m >= 8:
        for s0 in range(0, SUB, 2 * m):
            row = b_ref[pl.ds(r0 + (s0 + m - 1), 1), :]
            pieces.append(jnp.broadcast_to(row, (2 * m, width)))
    else:
        row8 = lax.broadcasted_iota(jnp.int32, (8, width), 0)
        for g0 in range(0, SUB, 8):
            acc = None
            for s0 in range(0, 8, 2 * m):
                row = jnp.broadcast_to(b_ref[pl.ds(r0 + (g0 + s0 + m - 1), 1), :], (8, width))
                acc = row if acc is None else jnp.where(row8 >= s0, row, acc)
            pieces.append(acc)
    return jnp.concatenate(pieces, axis=0)


def _half_middle_rows(b_ref, r0, width):
    half = SUB // 2
    return jnp.concatenate(
        [jnp.broadcast_to(b_ref[pl.ds(r0 + (s0 + half // 2 - 1), 1), :], (half, width))
         for s0 in range(0, SUB, half)], axis=0)


def _hgrn_level(att, q, kk, b, b_ref, r0, m, tx, row):
    ref_pt = _boundary_rows(b_ref, r0, m, D_HGRN)
    decay = jnp.exp(-jnp.abs(b - ref_pt))
    right = jnp.bitwise_and(row, m) != 0
    qt = jnp.where(right, q * decay, 0.0).astype(BF16)
    kt = jnp.where(right, 0.0, kk * decay).astype(BF16)
    same_block = tx < 2 * m
    out = []
    for h in range(N_HEADS):
        sl = slice(h * HEAD_DIM, (h + 1) * HEAD_DIM)
        term = jnp.where(same_block, _dot_nt(qt[:, sl], kt[:, sl]), 0.0)
        out.append(term if att is None else att[h] + term)
    return out


def _hgrn_attention(q, kk, b, b_ref, r0, shared_reference):
    t_idx = lax.broadcasted_iota(jnp.int32, (SUB, SUB), 0)
    s_idx = lax.broadcasted_iota(jnp.int32, (SUB, SUB), 1)
    tx = jnp.bitwise_xor(t_idx, s_idx)
    row = lax.broadcasted_iota(jnp.int32, (SUB, D_HGRN), 0)
    half = SUB // 2
    att = _hgrn_level(None, q, kk, b, b_ref, r0, half, tx, row)
    if shared_reference:
        expo = b - _half_middle_rows(b_ref, r0, D_HGRN)
        qt = (q * jnp.exp(expo)).astype(BF16)
        kt = (kk * jnp.exp(-expo)).astype(BF16)
        keep = (tx < half) & (s_idx <= t_idx)
        for h in range(N_HEADS):
            sl = slice(h * HEAD_DIM, (h + 1) * HEAD_DIM)
            att[h] = att[h] + jnp.where(keep, _dot_nt(qt[:, sl], kt[:, sl]), 0.0)
        return att
    qb = q.astype(BF16)
    kb = kk.astype(BF16)
    for h in range(N_HEADS):
        sl = slice(h * HEAD_DIM, (h + 1) * HEAD_DIM)
        att[h] = att[h] + jnp.where(tx == 0, _dot_nt(qb[:, sl], kb[:, sl]), 0.0)
    m = 1
    while m < half:
        att = _hgrn_level(att, q, kk, b, b_ref, r0, m, tx, row)
        m *= 2
    return att


def _hgrn_sub_chunk(z_ref, kk_ref, b_ref, hn_ref, y_ref, r0, state, shared_reference):
    rows = pl.ds(r0, SUB)
    q = _silu(z_ref[rows, 0:D_HGRN])
    kk = kk_ref[rows, :]
    b = b_ref[rows, :]
    v = z_ref[rows, 2 * D_HGRN:3 * D_HGRN].astype(BF16)
    att = _hgrn_attention(q, kk, b, b_ref, r0, shared_reference)
    b_end = b_ref[pl.ds(r0 + (SUB - 1), 1), :]
    q0 = (q * jnp.exp(b)).astype(BF16)
    k_end = (kk * jnp.exp(b_end - b)).astype(BF16)
    s_decay = jnp.exp(b_end)
    g = _silu(z_ref[rows, 3 * D_HGRN:4 * D_HGRN])
    new_state = []
    for hd in range(N_HEADS):
        sl = slice(hd * HEAD_DIM, (hd + 1) * HEAD_DIM)
        st = state[hd]
        o = _dot(att[hd].astype(BF16), v[:, sl]) + _dot_nt(q0[:, sl], st.astype(BF16))
        new_state.append(st * s_decay[:, sl] + _dot_tn(v[:, sl], k_end[:, sl]))
        oms = jnp.mean(o * o, axis=-1, keepdims=True)
        on = (o * lax.rsqrt(oms + RMS_EPS)) * hn_ref[...]
        y_ref[rows, sl] = (on * g[:, sl]).astype(BF16)
    return new_state


def _mixer_kernel(x_ref, gain_ref, win_ref, lb_ref, hn_ref, lng_ref, lnb_ref, wsp_ref, bsp_ref,
                  wout_ref, o_ref, z_ref, y_ref, b_ref, kk_ref, st_ref):
    @pl.when(pl.program_id(1) == 0)
    def _():
        st_ref[...] = jnp.zeros_like(st_ref)

    x = x_ref[0]
    ms = jnp.mean(x * x, axis=-1, keepdims=True)
    h = (x * lax.rsqrt(ms + RMS_EPS)) * gain_ref[...]
    z_ref[...] = _dot(h.astype(BF16), win_ref[...])

    t_idx = lax.broadcasted_iota(jnp.int32, (SUB, SUB), 0)
    s_idx = lax.broadcasted_iota(jnp.int32, (SUB, SUB), 1)
    tri = (s_idx <= t_idx).astype(BF16)
    n_sub = MIXER_ROWS // SUB

    worst = jnp.zeros((SUB, D_HGRN), F32)
    for c in range(n_sub):
        rows = pl.ds(c * SUB, SUB)
        lb = lb_ref[...]
        fg = lb + (1.0 - lb) * jax.nn.sigmoid(z_ref[rows, D_HGRN:2 * D_HGRN])
        lf_hi, lf_lo = _split_bf16(jnp.log(jnp.maximum(fg, F_FLOOR)))
        kk_ref[rows, :] = 1.0 - fg
        b = _dot(tri, lf_hi) + _dot(tri, lf_lo)
        b_ref[rows, :] = b
        worst = jnp.maximum(worst, jnp.abs(b - _half_middle_rows(b_ref, c * SUB, D_HGRN)))
    shared_ok = jnp.max(worst) <= HGRN_SAFE_EXP

    for c in range(n_sub):
        rows = pl.ds(c * SUB, SUB)
        u = _gelu(z_ref[rows, 4 * D_HGRN:4 * D_HGRN + D_TMLP])
        vv = _gelu(z_ref[rows, 4 * D_HGRN + D_TMLP:D_IN])
        for gi in range(N_GROUPS):
            sl = slice(gi * GROUP_DIM, (gi + 1) * GROUP_DIM)
            vg = vv[:, sl]
            mu = jnp.mean(vg, axis=-1, keepdims=True)
            cen = vg - mu
            var = jnp.mean(cen * cen, axis=-1, keepdims=True)
            vn = (cen * lax.rsqrt(var + LN_EPS)) * lng_ref[:, sl] + lnb_ref[:, sl]
            mixed = _dot(wsp_ref[gi], vn.astype(BF16)) + bsp_ref[:, sl]
            y_ref[rows, D_HGRN + gi * GROUP_DIM:D_HGRN + (gi + 1) * GROUP_DIM] = (
                u[:, sl] * mixed).astype(BF16)

    @pl.when(shared_ok)
    def _():
        state = [st_ref[hd] for hd in range(N_HEADS)]
        for c in range(n_sub):
            state = _hgrn_sub_chunk(z_ref, kk_ref, b_ref, hn_ref, y_ref, c * SUB, state, True)
        for hd in range(N_HEADS):
            st_ref[hd] = state[hd]

    @pl.when(jnp.logical_not(shared_ok))
    def _():
        def sub_chunk(c, carry):
            r0 = pl.multiple_of(c * SUB, SUB)
            state = [st_ref[hd] for hd in range(N_HEADS)]
            state = _hgrn_sub_chunk(z_ref, kk_ref, b_ref, hn_ref, y_ref, r0, state, False)
            for hd in range(N_HEADS):
                st_ref[hd] = state[hd]
            return carry

        lax.fori_loop(0, n_sub, sub_chunk, 0)

    o_ref[0] = x + _dot(y_ref[...], wout_ref[...])


def _mixer(x, gain, w_in, lb, hgrn_norm, ln_g, ln_b, w_sp, b_sp, w_out):
    bsz, seq, _ = x.shape
    assert seq % MIXER_ROWS == 0 and MIXER_ROWS % SUB == 0
    const2 = lambda b, j: (0, 0)
    const3 = lambda b, j: (0, 0, 0)
    return pl.pallas_call(
        _mixer_kernel,
        grid=(bsz, seq // MIXER_ROWS),
        in_specs=[
            pl.BlockSpec((1, MIXER_ROWS, D_MODEL), lambda b, j: (b, j, 0)),
            pl.BlockSpec((1, D_MODEL), const2),
            pl.BlockSpec((D_MODEL, D_IN), const2),
            pl.BlockSpec((1, D_HGRN), const2),
            pl.BlockSpec((1, HEAD_DIM), const2),
            pl.BlockSpec((1, D_TMLP), const2),
            pl.BlockSpec((1, D_TMLP), const2),
            pl.BlockSpec((N_GROUPS, SUB, SUB), const3),
            pl.BlockSpec((SUB, D_TMLP), const2),
            pl.BlockSpec((D_HGRN + D_TMLP, D_MODEL), const2),
        ],
        out_specs=pl.BlockSpec((1, MIXER_ROWS, D_MODEL), lambda b, j: (b, j, 0)),
        out_shape=jax.ShapeDtypeStruct(x.shape, F32),
        scratch_shapes=[
            pltpu.VMEM((MIXER_ROWS, D_IN), F32),
            pltpu.VMEM((MIXER_ROWS, D_HGRN + D_TMLP), BF16),
            pltpu.VMEM((MIXER_ROWS, D_HGRN), F32),
            pltpu.VMEM((MIXER_ROWS, D_HGRN), F32),
            pltpu.VMEM((N_HEADS, HEAD_DIM, HEAD_DIM), F32),
        ],
        compiler_params=pltpu.CompilerParams(
            dimension_semantics=("arbitrary", "arbitrary"),
            vmem_limit_bytes=VMEM_LIMIT_BYTES),
        name="mixer",
    )(x, gain, w_in, lb, hgrn_norm, ln_g, ln_b, w_sp, b_sp, w_out)


def _pack_bf16_pairs(a):
    lo = lax.bitcast_convert_type(a[:, :PACKED].astype(BF16).astype(F32), jnp.uint32)
    hi = lax.bitcast_convert_type(a[:, PACKED:].astype(BF16).astype(F32), jnp.uint32)
    return jnp.bitwise_or(lax.shift_right_logical(lo, jnp.uint32(16)),
                          jnp.bitwise_and(hi, jnp.uint32(0xFFFF0000)))


def _unpack_bf16_pairs(u):
    lo = lax.bitcast_convert_type(lax.shift_left(u, jnp.uint32(16)), F32)
    hi = lax.bitcast_convert_type(jnp.bitwise_and(u, jnp.uint32(0xFFFF0000)), F32)
    return lo, hi


def _router_kernel(x_ref, gain_ref, wr_hi_ref, wr_lo_ref, later_ref, hpk_ref, meta_ref, wts_ref,
                   cnt_out_ref, cnt_ref):
    @pl.when(pl.program_id(0) == 0)
    def _():
        cnt_ref[...] = jnp.zeros_like(cnt_ref)

    x = x_ref[...]
    ms = jnp.mean(x * x, axis=-1, keepdims=True)
    h = (x * lax.rsqrt(ms + RMS_EPS)) * gain_ref[...]
    hpk_ref[...] = _pack_bf16_pairs(h)
    h_hi, h_lo = _split_bf16(h)
    logits = _dot(h_hi, wr_hi_ref[...]) + (_dot(h_hi, wr_lo_ref[...]) + _dot(h_lo, wr_hi_ref[...]))
    lt = logits.T
    sub = lax.broadcasted_iota(jnp.int32, (EXPERTS_PER_GROUP, ROUTE_ROWS), 0)
    neg = jnp.float32(-jnp.inf)
    big = jnp.int32(1 << 20)
    gl = jnp.where(sub < N_EXPERT_GROUPS, lt[N_EXPERTS:N_EXPERTS + EXPERTS_PER_GROUP], neg)
    gmax = jnp.max(gl, axis=0, keepdims=True)
    p_sel = 1.0 / jnp.sum(jnp.exp(gl - gmax), axis=0, keepdims=True)
    g_idx = jnp.min(jnp.where(gl == gmax, sub, big), axis=0, keepdims=True)
    el = lt[(N_EXPERT_GROUPS - 1) * EXPERTS_PER_GROUP:N_EXPERTS]
    for g in range(N_EXPERT_GROUPS - 2, -1, -1):
        el = jnp.where(g_idx == g, lt[g * EXPERTS_PER_GROUP:(g + 1) * EXPERTS_PER_GROUP], el)
    v1 = jnp.max(el, axis=0, keepdims=True)
    i1 = jnp.min(jnp.where(el == v1, sub, big), axis=0, keepdims=True)
    el2 = jnp.where(sub == i1, neg, el)
    v2 = jnp.max(el2, axis=0, keepdims=True)
    i2 = jnp.min(jnp.where(el2 == v2, sub, big), axis=0, keepdims=True)
    e2x = jnp.exp(v2 - v1)
    w1 = p_sel / (1.0 + e2x)
    w2 = p_sel * e2x / (1.0 + e2x)
    e1 = g_idx * EXPERTS_PER_GROUP + i1
    e2 = g_idx * EXPERTS_PER_GROUP + i2

    expert = lax.broadcasted_iota(jnp.int32, (N_EXPERTS, ROUTE_ROWS), 0)
    assigned = ((expert == e1) | (expert == e2)).astype(F32)
    earlier = cnt_ref[:, 0:1] + _dot(assigned.astype(BF16), later_ref[...])
    rank1 = jnp.sum(jnp.where(expert == e1, earlier, 0.0), axis=0, keepdims=True).astype(jnp.int32)
    rank2 = jnp.sum(jnp.where(expert == e2, earlier, 0.0), axis=0, keepdims=True).astype(jnp.int32)
    meta_ref[...] = jnp.where(sub == 0, e1, jnp.where(sub == 1, e2, jnp.where(
        sub == 2, rank1, jnp.where(sub == 3, rank2, 0))))
    cnt_ref[...] = cnt_ref[...] + jnp.sum(assigned, axis=1, keepdims=True)
    cnt_out_ref[...] = cnt_ref[...]

    w_rows = jnp.where(sub == 0, w1, jnp.where(sub == 1, w2, 0.0))
    pick = (lax.broadcasted_iota(jnp.int32, (EXPERTS_PER_GROUP, ROUTER_LANES), 0)
            == lax.broadcasted_iota(jnp.int32, (EXPERTS_PER_GROUP, ROUTER_LANES), 1))
    pick = pick.astype(F32).astype(BF16)
    p_hi = w_rows.astype(BF16)
    res = w_rows - p_hi.astype(F32)
    p_mid = res.astype(BF16)
    p_lo = (res - p_mid.astype(F32)).astype(BF16)
    wts_ref[...] = _dot_tn(p_hi, pick) + (_dot_tn(p_mid, pick) + _dot_tn(p_lo, pick))


def _router(x2d, gain, wr_hi, wr_lo, later):
    n = x2d.shape[0]
    assert n % ROUTE_ROWS == 0
    const2 = lambda i: (0, 0)
    rows = lambda i: (i, 0)
    return pl.pallas_call(
        _router_kernel,
        grid=(n // ROUTE_ROWS,),
        in_specs=[
            pl.BlockSpec((ROUTE_ROWS, D_MODEL), rows),
            pl.BlockSpec((1, D_MODEL), const2),
            pl.BlockSpec((D_MODEL, ROUTER_LANES), const2),
            pl.BlockSpec((D_MODEL, ROUTER_LANES), const2),
            pl.BlockSpec((ROUTE_ROWS, ROUTE_ROWS), const2),
        ],
        out_specs=[
            pl.BlockSpec((ROUTE_ROWS, PACKED), rows),
            pl.BlockSpec((EXPERTS_PER_GROUP, ROUTE_ROWS), lambda i: (0, i)),
            pl.BlockSpec((ROUTE_ROWS, ROUTER_LANES), rows),
            pl.BlockSpec((N_EXPERTS, ROUTER_LANES), const2),
        ],
        out_shape=[
            jax.ShapeDtypeStruct((n, PACKED), jnp.uint32),
            jax.ShapeDtypeStruct((EXPERTS_PER_GROUP, n), jnp.int32),
            jax.ShapeDtypeStruct((n, ROUTER_LANES), F32),
            jax.ShapeDtypeStruct((N_EXPERTS, ROUTER_LANES), F32),
        ],
        scratch_shapes=[pltpu.VMEM((N_EXPERTS, ROUTER_LANES), F32)],
        compiler_params=pltpu.CompilerParams(
            dimension_semantics=("arbitrary",), vmem_limit_bytes=VMEM_LIMIT_BYTES),
        name="router",
    )(x2d, gain, wr_hi, wr_lo, later)


def _positions_kernel(meta_ref, row_start_ref, pos_ref):
    meta = meta_ref[...]
    expert = lax.broadcasted_iota(jnp.int32, (N_EXPERTS, ROUTE_ROWS), 0)
    starts = row_start_ref[:, 0:1]
    base1 = jnp.sum(jnp.where(expert == meta[0:1], starts, 0), axis=0, keepdims=True)
    base2 = jnp.sum(jnp.where(expert == meta[1:2], starts, 0), axis=0, keepdims=True)
    sub = lax.broadcasted_iota(jnp.int32, meta.shape, 0)
    pos_ref[...] = jnp.where(sub == 0, base1 + meta[2:3], jnp.where(sub == 1, base2 + meta[3:4], 0))


def _positions(meta, row_start_lanes):
    n = meta.shape[1]
    return pl.pallas_call(
        _positions_kernel,
        grid=(n // ROUTE_ROWS,),
        in_specs=[pl.BlockSpec((EXPERTS_PER_GROUP, ROUTE_ROWS), lambda i: (0, i)),
                  pl.BlockSpec((N_EXPERTS, ROUTER_LANES), lambda i: (0, 0))],
        out_specs=pl.BlockSpec((EXPERTS_PER_GROUP, ROUTE_ROWS), lambda i: (0, i)),
        out_shape=jax.ShapeDtypeStruct(meta.shape, jnp.int32),
        compiler_params=pltpu.CompilerParams(
            dimension_semantics=("arbitrary",), vmem_limit_bytes=VMEM_LIMIT_BYTES),
        name="positions",
    )(meta, row_start_lanes)


def _row_copy(src_ref, src_row, dst_ref, dst_row, sem):
    return pltpu.make_async_copy(src_ref.at[pl.ds(src_row, 1)], dst_ref.at[pl.ds(dst_row, 1)], sem)


def _scatter_kernel(pos1_ref, pos2_ref, hpk_ref, sorted_ref, sem):
    def issue(i, carry):
        _row_copy(hpk_ref, i, sorted_ref, pos1_ref[i], sem).start(priority=0)
        _row_copy(hpk_ref, i, sorted_ref, pos2_ref[i], sem).start(priority=1)
        return carry

    lax.fori_loop(0, SCATTER_ROWS, issue, 0, unroll=DMA_UNROLL)

    def drain(i, carry):
        _row_copy(hpk_ref, 0, sorted_ref, 0, sem).wait()
        _row_copy(hpk_ref, 0, sorted_ref, 0, sem).wait()
        return carry

    lax.fori_loop(0, SCATTER_ROWS, drain, 0, unroll=DMA_UNROLL)


def _scatter(pos1, pos2, hpk, n_rows):
    n = hpk.shape[0]
    assert n % SCATTER_ROWS == 0
    idx = lambda i: (i,)
    return pl.pallas_call(
        _scatter_kernel,
        grid=(n // SCATTER_ROWS,),
        in_specs=[
            pl.BlockSpec((SCATTER_ROWS,), idx, memory_space=pltpu.SMEM),
            pl.BlockSpec((SCATTER_ROWS,), idx, memory_space=pltpu.SMEM),
            pl.BlockSpec((SCATTER_ROWS, PACKED), lambda i: (i, 0)),
        ],
        out_specs=pl.BlockSpec(memory_space=pl.ANY),
        out_shape=jax.ShapeDtypeStruct((n_rows, PACKED), jnp.uint32),
        scratch_shapes=[pltpu.SemaphoreType.DMA],
        compiler_params=pltpu.CompilerParams(
            dimension_semantics=("arbitrary",), vmem_limit_bytes=VMEM_LIMIT_BYTES),
        name="scatter_rows",
    )(pos1, pos2, hpk)


def _expert_kernel(tile_expert_ref, tile_rows_ref, n_used_ref, lhs_ref, wg_ref, wu_ref, wd_ref,
                   o_ref):
    del tile_expert_ref, n_used_ref
    n_valid = tile_rows_ref[pl.program_id(0)]

    @pl.when(n_valid > 0)
    def _():
        valid = lax.broadcasted_iota(jnp.int32, (EXPERT_TILE, 1), 0) < n_valid
        lo, hi = _unpack_bf16_pairs(lhs_ref[...])
        hh = jnp.concatenate([jnp.where(valid, lo, 0.0).astype(BF16),
                              jnp.where(valid, hi, 0.0).astype(BF16)], axis=1)
        w_gu = jnp.concatenate([wg_ref[0, 0].astype(BF16), wu_ref[0, 0].astype(BF16)], axis=1)
        gu = _dot(hh, w_gu)
        hid = _silu(gu[:, :D_EXPERT]) * gu[:, D_EXPERT:]
        o_ref[...] = _pack_bf16_pairs(_dot(hid.astype(BF16), wd_ref[0, 0].astype(BF16)))

    @pl.when(n_valid <= 0)
    def _():
        o_ref[...] = jnp.zeros_like(o_ref)


def _experts(tile_expert, tile_rows, n_used, sorted_rows, w_gate, w_up, w_down, layer):
    n_tiles = sorted_rows.shape[0] // EXPERT_TILE
    live = lambda i, nu: jnp.minimum(i, nu[0] - 1)
    expert = lambda i, te, tr, nu: (layer, te[live(i, nu)], 0, 0)
    return pl.pallas_call(
        _expert_kernel,
        grid_spec=pltpu.PrefetchScalarGridSpec(
            num_scalar_prefetch=3,
            grid=(n_tiles,),
            in_specs=[
                pl.BlockSpec((EXPERT_TILE, PACKED), lambda i, te, tr, nu: (live(i, nu), 0)),
                pl.BlockSpec((1, 1, D_MODEL, D_EXPERT), expert),
                pl.BlockSpec((1, 1, D_MODEL, D_EXPERT), expert),
                pl.BlockSpec((1, 1, D_EXPERT, D_MODEL), expert),
            ],
            out_specs=pl.BlockSpec((EXPERT_TILE, PACKED), lambda i, te, tr, nu: (i, 0)),
        ),
        out_shape=jax.ShapeDtypeStruct(sorted_rows.shape, jnp.uint32),
        compiler_params=pltpu.CompilerParams(
            dimension_semantics=("arbitrary",), vmem_limit_bytes=VMEM_LIMIT_BYTES),
        name="experts",
    )(tile_expert, tile_rows, n_used, sorted_rows, w_gate, w_up, w_down)


def _combine_kernel(pos1_ref, pos2_ref, x_ref, wts_ref, ys_ref, gfin_ref, o_ref, buf1_ref, buf2_ref,
                    sem, *, final_norm):
    def issue(i, carry):
        _row_copy(ys_ref, pos1_ref[i], buf1_ref, i, sem).start(priority=0)
        _row_copy(ys_ref, pos2_ref[i], buf2_ref, i, sem).start(priority=1)
        return carry

    lax.fori_loop(0, COMBINE_ROWS, issue, 0, unroll=DMA_UNROLL)

    def drain(i, carry):
        _row_copy(ys_ref, 0, buf1_ref, 0, sem).wait()
        _row_copy(ys_ref, 0, buf2_ref, 0, sem).wait()
        return carry

    lax.fori_loop(0, COMBINE_ROWS, drain, 0, unroll=DMA_UNROLL)
    lo1, hi1 = _unpack_bf16_pairs(buf1_ref[...])
    lo2, hi2 = _unpack_bf16_pairs(buf2_ref[...])
    w1 = wts_ref[:, 0:1]
    w2 = wts_ref[:, 1:2]
    x = x_ref[...]
    y = jnp.concatenate([x[:, :PACKED] + (w1 * lo1 + w2 * lo2),
                         x[:, PACKED:] + (w1 * hi1 + w2 * hi2)], axis=1)
    if final_norm:
        ms = jnp.mean(y * y, axis=-1, keepdims=True)
        y = (y * lax.rsqrt(ms + RMS_EPS)) * gfin_ref[...]
    o_ref[...] = y


def _combine(pos1, pos2, x2d, wts, ys, gain_final, final_norm):
    n = x2d.shape[0]
    assert n % COMBINE_ROWS == 0
    idx = lambda i: (i,)
    rows = lambda i: (i, 0)
    return pl.pallas_call(
        functools.partial(_combine_kernel, final_norm=final_norm),
        grid=(n // COMBINE_ROWS,),
        in_specs=[
            pl.BlockSpec((COMBINE_ROWS,), idx, memory_space=pltpu.SMEM),
            pl.BlockSpec((COMBINE_ROWS,), idx, memory_space=pltpu.SMEM),
            pl.BlockSpec((COMBINE_ROWS, D_MODEL), rows),
            pl.BlockSpec((COMBINE_ROWS, ROUTER_LANES), rows),
            pl.BlockSpec(memory_space=pl.ANY),
            pl.BlockSpec((1, D_MODEL), lambda i: (0, 0)),
        ],
        out_specs=pl.BlockSpec((COMBINE_ROWS, D_MODEL), rows),
        out_shape=jax.ShapeDtypeStruct(x2d.shape, F32),
        scratch_shapes=[
            pltpu.VMEM((COMBINE_ROWS, PACKED), jnp.uint32),
            pltpu.VMEM((COMBINE_ROWS, PACKED), jnp.uint32),
            pltpu.SemaphoreType.DMA,
        ],
        compiler_params=pltpu.CompilerParams(
            dimension_semantics=("arbitrary",), vmem_limit_bytes=VMEM_LIMIT_BYTES),
        name="combine",
    )(pos1, pos2, x2d, wts, ys, gain_final)


def _moe(x2d, gain, wr_hi, wr_lo, later, w_gate, w_up, w_down, layer, gain_final, final_norm):
    n = x2d.shape[0]
    max_tiles = (2 * n) // EXPERT_TILE + N_EXPERTS
    hpk, meta, wts, cnt = _router(x2d, gain, wr_hi, wr_lo, later)
    counts = cnt[:, 0].astype(jnp.int32)
    tiles_per = (counts + (EXPERT_TILE - 1)) // EXPERT_TILE
    tile_end = jnp.cumsum(tiles_per)
    tile_start = tile_end - tiles_per
    row_start = tile_start * EXPERT_TILE
    tile_ids = jnp.arange(max_tiles, dtype=jnp.int32)
    tile_expert = jnp.minimum(
        jnp.sum((tile_ids[:, None] >= tile_end[None, :]).astype(jnp.int32), axis=1), N_EXPERTS - 1)
    n_used = tile_end[-1:].astype(jnp.int32)
    rows_left = counts[tile_expert] - (tile_ids - tile_start[tile_expert]) * EXPERT_TILE
    tile_rows = jnp.where(tile_ids < n_used[0], jnp.clip(rows_left, 0, EXPERT_TILE), 0)
    pos = _positions(meta, jnp.broadcast_to(row_start[:, None], (N_EXPERTS, ROUTER_LANES)))
    sorted_rows = _scatter(pos[0], pos[1], hpk, max_tiles * EXPERT_TILE)
    ys = _experts(tile_expert, tile_rows.astype(jnp.int32), n_used, sorted_rows, w_gate, w_up, w_down,
                  layer)
    return _combine(pos[0], pos[1], x2d, wts, ys, gain_final, final_norm)


def kernel(x, lb_logits, norm_mix, w_in, hgrn_norm, tmlp_ln_g, tmlp_ln_b, w_spatial, b_spatial,
           w_out, norm_ffn, w_router_group, w_router_expert, w_gate, w_up, w_down, norm_final):
    depth = w_in.shape[0]
    bsz, seq, _ = x.shape
    p = jax.nn.softmax(lb_logits.astype(F32), axis=0)
    lower_bounds = jnp.cumsum(p, axis=0) - p[0:1]
    tril = jnp.tril(jnp.ones((SUB, SUB), dtype=bool))
    later = jnp.triu(jnp.ones((ROUTE_ROWS, ROUTE_ROWS), BF16), 1)
    for layer in range(depth):
        w_sp = jnp.where(tril[None], w_spatial[layer], 0.0).astype(BF16)
        b_sp = jnp.repeat(b_spatial[layer].T, GROUP_DIM, axis=1)
        x = _mixer(x, norm_mix[layer][None], w_in[layer].astype(BF16), lower_bounds[layer][None],
                   hgrn_norm[layer][None], tmlp_ln_g[layer][None], tmlp_ln_b[layer][None],
                   w_sp, b_sp, w_out[layer].astype(BF16))
        w_r = jnp.concatenate([w_router_expert[layer], w_router_group[layer]], axis=1)
        w_r = jnp.pad(w_r, ((0, 0), (0, ROUTER_LANES - w_r.shape[1])))
        wr_hi, wr_lo = _split_bf16(w_r)
        x2d = _moe(x.reshape(bsz * seq, D_MODEL), norm_ffn[layer][None], wr_hi, wr_lo, later,
                   w_gate, w_up, w_down, layer, norm_final[None], final_norm=(layer == depth - 1))
        x = x2d.reshape(bsz, seq, D_MODEL)
    return x
```

```python
import functools

import jax
import jax.numpy as jnp
from jax import lax
from jax.experimental import pallas as pl
from jax.experimental.pallas import tpu as pltpu

F32 = jnp.float32
BF16 = jnp.bfloat16

D_MODEL = 1024
N_HEADS = 4
HEAD_DIM = 128
D_HGRN = N_HEADS * HEAD_DIM
N_GROUPS = 4
GROUP_DIM = 128
D_TMLP = N_GROUPS * GROUP_DIM
D_IN = 4 * D_HGRN + 2 * D_TMLP
SUB = 128
N_EXPERT_GROUPS = 4
EXPERTS_PER_GROUP = 8
N_EXPERTS = N_EXPERT_GROUPS * EXPERTS_PER_GROUP
D_EXPERT = 256
ROUTER_LANES = 128
RMS_EPS = 1e-6
LN_EPS = 1e-5
F_FLOOR = 1e-30
HGRN_SAFE_EXP = 60.0
SQRT_HALF = 0.7071067811865476

MIXER_ROWS = 512
PROJ_PIECE = 256
ROUTE_ROWS = 1024
SCATTER_ROWS = 2048
COMBINE_ROWS = 1024
EXPERT_TILE = 512
PACKED = D_MODEL // 2
DMA_UNROLL = 8
VMEM_LIMIT_BYTES = 56 * 1024 * 1024


def _dot(a, b):
    return jnp.dot(a, b, preferred_element_type=F32)


def _dot_nt(a, b):
    return lax.dot_general(a, b, (((1,), (1,)), ((), ())), preferred_element_type=F32)


def _dot_tn(a, b):
    return lax.dot_general(a, b, (((0,), (0,)), ((), ())), preferred_element_type=F32)


def _split_bf16(a):
    hi = a.astype(BF16)
    lo = (a - hi.astype(F32)).astype(BF16)
    return hi, lo


def _gelu(a):
    return 0.5 * a * (1.0 + lax.erf(a * SQRT_HALF))


def _silu(a):
    return a * jax.nn.sigmoid(a)


def _boundary_rows(b_ref, r0, m, width):
    pieces = []
    if 2 * m >= 8:
        for s0 in range(0, SUB, 2 * m):
            row = b_ref[pl.ds(r0 + (s0 + m - 1), 1), :]
            pieces.append(jnp.broadcast_to(row, (2 * m, width)))
    else:
        row8 = lax.broadcasted_iota(jnp.int32, (8, width), 0)
        for g0 in range(0, SUB, 8):
            acc = None
            for s0 in range(0, 8, 2 * m):
                row = jnp.broadcast_to(b_ref[pl.ds(r0 + (g0 + s0 + m - 1), 1), :], (8, width))
                acc = row if acc is None else jnp.where(row8 >= s0, row, acc)
            pieces.append(acc)
    return jnp.concatenate(pieces, axis=0)


def _half_middle_rows(b_ref, r0, width):
    half = SUB // 2
    return jnp.concatenate(
        [jnp.broadcast_to(b_ref[pl.ds(r0 + (s0 + half // 2 - 1), 1), :], (half, width))
         for s0 in range(0, SUB, half)], axis=0)


def _hgrn_level(att, q, kk, b, b_ref, r0, m, tx, row):
    ref_pt = _boundary_rows(b_ref, r0, m, D_HGRN)
    decay = jnp.exp(-jnp.abs(b - ref_pt))
    right = jnp.bitwise_and(row, m) != 0
    qt = jnp.where(right, q * decay, 0.0).astype(BF16)
    kt = jnp.where(right, 0.0, kk * decay).astype(BF16)
    same_block = tx < 2 * m
    out = []
    for h in range(N_HEADS):
        sl = slice(h * HEAD_DIM, (h + 1) * HEAD_DIM)
        term = jnp.where(same_block, _dot_nt(qt[:, sl], kt[:, sl]), 0.0)
        out.append(term if att is None else att[h] + term)
    return out


def _hgrn_attention(q, kk, b, b_ref, r0, shared_reference):
    t_idx = lax.broadcasted_iota(jnp.int32, (SUB, SUB), 0)
    s_idx = lax.broadcasted_iota(jnp.int32, (SUB, SUB), 1)
    tx = jnp.bitwise_xor(t_idx, s_idx)
    row = lax.broadcasted_iota(jnp.int32, (SUB, D_HGRN), 0)
    half = SUB // 2
    att = _hgrn_level(None, q, kk, b, b_ref, r0, half, tx, row)
    if shared_reference:
        expo = b - _half_middle_rows(b_ref, r0, D_HGRN)
        qt = (q * jnp.exp(expo)).astype(BF16)
        kt = (kk * jnp.exp(-expo)).astype(BF16)
        keep = (tx < half) & (s_idx <= t_idx)
        for h in range(N_HEADS):
            sl = slice(h * HEAD_DIM, (h + 1) * HEAD_DIM)
            att[h] = att[h] + jnp.where(keep, _dot_nt(qt[:, sl], kt[:, sl]), 0.0)
        return att
    qb = q.astype(BF16)
    kb = kk.astype(BF16)
    for h in range(N_HEADS):
        sl = slice(h * HEAD_DIM, (h + 1) * HEAD_DIM)
        att[h] = att[h] + jnp.where(tx == 0, _dot_nt(qb[:, sl], kb[:, sl]), 0.0)
    m = 1
    while m < half:
        att = _hgrn_level(att, q, kk, b, b_ref, r0, m, tx, row)
        m *= 2
    return att


def _hgrn_sub_chunk(z_ref, kk_ref, b_ref, hn_ref, y_ref, r0, state, shared_reference, per_head=None):
    rows = pl.ds(r0, SUB)
    q = _silu(z_ref[rows, 0:D_HGRN])
    kk = kk_ref[rows, :]
    b = b_ref[rows, :]
    v = z_ref[rows, 2 * D_HGRN:3 * D_HGRN].astype(BF16)
    att = _hgrn_attention(q, kk, b, b_ref, r0, shared_reference)
    b_end = b_ref[pl.ds(r0 + (SUB - 1), 1), :]
    q0 = (q * jnp.exp(b)).astype(BF16)
    k_end = (kk * jnp.exp(b_end - b)).astype(BF16)
    s_decay = jnp.exp(b_end)
    g = _silu(z_ref[rows, 3 * D_HGRN:4 * D_HGRN])
    new_state = []
    for hd in range(N_HEADS):
        if per_head is not None:
            per_head(hd)
        sl = slice(hd * HEAD_DIM, (hd + 1) * HEAD_DIM)
        st = state[hd]
        o = _dot(att[hd].astype(BF16), v[:, sl]) + _dot_nt(q0[:, sl], st.astype(BF16))
        new_state.append(st * s_decay[:, sl] + _dot_tn(v[:, sl], k_end[:, sl]))
        oms = jnp.mean(o * o, axis=-1, keepdims=True)
        on = (o * lax.rsqrt(oms + RMS_EPS)) * hn_ref[...]
        y_ref[rows, sl] = (on * g[:, sl]).astype(BF16)
    return new_state


def _mixer_tile(z_ref, zn_ref, h_next, win_ref, lb_ref, hn_ref, lng_ref, lnb_ref, wsp_ref, bsp_ref,
                y_ref, b_ref, kk_ref, st_ref):
    def project(first, count):
        for p in range(first, first + count):
            cols = slice(p * PROJ_PIECE, (p + 1) * PROJ_PIECE)
            zn_ref[:, cols] = _dot(h_next, win_ref[:, cols])

    n_pieces = D_IN // PROJ_PIECE
    early = n_pieces // 3

    t_idx = lax.broadcasted_iota(jnp.int32, (SUB, SUB), 0)
    s_idx = lax.broadcasted_iota(jnp.int32, (SUB, SUB), 1)
    tri = (s_idx <= t_idx).astype(BF16)
    n_sub = MIXER_ROWS // SUB

    worst = jnp.zeros((SUB, D_HGRN), F32)
    for c in range(n_sub):
        rows = pl.ds(c * SUB, SUB)
        lb = lb_ref[...]
        fg = lb + (1.0 - lb) * jax.nn.sigmoid(z_ref[rows, D_HGRN:2 * D_HGRN])
        lf_hi, lf_lo = _split_bf16(jnp.log(jnp.maximum(fg, F_FLOOR)))
        kk_ref[rows, :] = 1.0 - fg
        b = _dot(tri, lf_hi) + _dot(tri, lf_lo)
        b_ref[rows, :] = b
        worst = jnp.maximum(worst, jnp.abs(b - _half_middle_rows(b_ref, c * SUB, D_HGRN)))
    shared_ok = jnp.max(worst) <= HGRN_SAFE_EXP

    for c in range(n_sub):
        project(c * early // n_sub, (c + 1) * early // n_sub - c * early // n_sub)
        rows = pl.ds(c * SUB, SUB)
        u = _gelu(z_ref[rows, 4 * D_HGRN:4 * D_HGRN + D_TMLP])
        vv = _gelu(z_ref[rows, 4 * D_HGRN + D_TMLP:D_IN])
        for gi in range(N_GROUPS):
            sl = slice(gi * GROUP_DIM, (gi + 1) * GROUP_DIM)
            vg = vv[:, sl]
            mu = jnp.mean(vg, axis=-1, keepdims=True)
            cen = vg - mu
            var = jnp.mean(cen * cen, axis=-1, keepdims=True)
            vn = (cen * lax.rsqrt(var + LN_EPS)) * lng_ref[:, sl] + lnb_ref[:, sl]
            mixed = _dot(wsp_ref[gi], vn.astype(BF16)) + bsp_ref[:, sl]
            y_ref[rows, D_HGRN + gi * GROUP_DIM:D_HGRN + (gi + 1) * GROUP_DIM] = (
                u[:, sl] * mixed).astype(BF16)

    late = n_pieces - early

    @pl.when(shared_ok)
    def _():
        state = [st_ref[hd] for hd in range(N_HEADS)]
        stride = (n_sub * N_HEADS) // late

        def piece_for_slot(slot):
            if slot % stride == 0:
                project(early + slot // stride, 1)

        for c in range(n_sub):
            state = _hgrn_sub_chunk(z_ref, kk_ref, b_ref, hn_ref, y_ref, c * SUB, state, True,
                                    lambda hd, c=c: piece_for_slot(c * N_HEADS + hd))
        for hd in range(N_HEADS):
            st_ref[hd] = state[hd]

    @pl.when(jnp.logical_not(shared_ok))
    def _():
        project(early, late)

        def sub_chunk(c, carry):
            r0 = pl.multiple_of(c * SUB, SUB)
            state = [st_ref[hd] for hd in range(N_HEADS)]
            state = _hgrn_sub_chunk(z_ref, kk_ref, b_ref, hn_ref, y_ref, r0, state, False)
            for hd in range(N_HEADS):
                st_ref[hd] = state[hd]
            return carry

        lax.fori_loop(0, n_sub, sub_chunk, 0)


def _mixer_kernel(x_ref, xn_ref, gain_ref, win_ref, lb_ref, hn_ref, lng_ref, lnb_ref, wsp_ref, bsp_ref,
                  wout_ref, o_ref, za_ref, zb_ref, y_ref, b_ref, kk_ref, st_ref):
    def normed(xv):
        ms = jnp.mean(xv * xv, axis=-1, keepdims=True)
        return ((xv * lax.rsqrt(ms + RMS_EPS)) * gain_ref[...]).astype(BF16)

    @pl.when(pl.program_id(1) == 0)
    def _():
        st_ref[...] = jnp.zeros_like(st_ref)

    @pl.when((pl.program_id(0) == 0) & (pl.program_id(1) == 0))
    def _():
        za_ref[...] = _dot(normed(x_ref[0, 0:MIXER_ROWS, :]), win_ref[...])

    tile_refs = (win_ref, lb_ref, hn_ref, lng_ref, lnb_ref, wsp_ref, bsp_ref)
    _mixer_tile(za_ref, zb_ref, normed(x_ref[0, MIXER_ROWS:2 * MIXER_ROWS, :]), *tile_refs,
                y_ref, b_ref, kk_ref, st_ref)
    o_ref[0, 0:MIXER_ROWS, :] = x_ref[0, 0:MIXER_ROWS, :] + _dot(y_ref[...], wout_ref[...])
    _mixer_tile(zb_ref, za_ref, normed(xn_ref[0]), *tile_refs, y_ref, b_ref, kk_ref, st_ref)
    o_ref[0, MIXER_ROWS:2 * MIXER_ROWS, :] = (
        x_ref[0, MIXER_ROWS:2 * MIXER_ROWS, :] + _dot(y_ref[...], wout_ref[...]))


def _mixer(x, gain, w_in, lb, hgrn_norm, ln_g, ln_b, w_sp, b_sp, w_out):
    bsz, seq, _ = x.shape
    step_rows = 2 * MIXER_ROWS
    assert seq % step_rows == 0 and MIXER_ROWS % SUB == 0
    steps = seq // step_rows
    const2 = lambda b, j: (0, 0)
    const3 = lambda b, j: (0, 0, 0)
    once = pl.Buffered(1)

    def next_tile_a(b, j):
        nxt = jnp.minimum(b * steps + j + 1, bsz * steps - 1)
        return (nxt // steps, 2 * (nxt % steps), 0)

    return pl.pallas_call(
        _mixer_kernel,
        grid=(bsz, steps),
        in_specs=[
            pl.BlockSpec((1, step_rows, D_MODEL), lambda b, j: (b, j, 0)),
            pl.BlockSpec((1, MIXER_ROWS, D_MODEL), next_tile_a),
            pl.BlockSpec((1, D_MODEL), const2),
            pl.BlockSpec((D_MODEL, D_IN), const2, pipeline_mode=once),
            pl.BlockSpec((1, D_HGRN), const2),
            pl.BlockSpec((1, HEAD_DIM), const2),
            pl.BlockSpec((1, D_TMLP), const2),
            pl.BlockSpec((1, D_TMLP), const2),
            pl.BlockSpec((N_GROUPS, SUB, SUB), const3),
            pl.BlockSpec((SUB, D_TMLP), const2),
            pl.BlockSpec((D_HGRN + D_TMLP, D_MODEL), const2, pipeline_mode=once),
        ],
        out_specs=pl.BlockSpec((1, step_rows, D_MODEL), lambda b, j: (b, j, 0)),
        out_shape=jax.ShapeDtypeStruct(x.shape, F32),
        scratch_shapes=[
            pltpu.VMEM((MIXER_ROWS, D_IN), F32),
            pltpu.VMEM((MIXER_ROWS, D_IN), F32),
            pltpu.VMEM((MIXER_ROWS, D_HGRN + D_TMLP), BF16),
            pltpu.VMEM((MIXER_ROWS, D_HGRN), F32),
            pltpu.VMEM((MIXER_ROWS, D_HGRN), F32),
            pltpu.VMEM((N_HEADS, HEAD_DIM, HEAD_DIM), F32),
        ],
        compiler_params=pltpu.CompilerParams(
            dimension_semantics=("arbitrary", "arbitrary"),
            vmem_limit_bytes=VMEM_LIMIT_BYTES),
        name="mixer",
    )(x, x, gain, w_in, lb, hgrn_norm, ln_g, ln_b, w_sp, b_sp, w_out)


def _pack_bf16_pairs(a):
    lo = lax.bitcast_convert_type(a[:, :PACKED].astype(BF16).astype(F32), jnp.uint32)
    hi = lax.bitcast_convert_type(a[:, PACKED:].astype(BF16).astype(F32), jnp.uint32)
    return jnp.bitwise_or(lax.shift_right_logical(lo, jnp.uint32(16)),
                          jnp.bitwise_and(hi, jnp.uint32(0xFFFF0000)))


def _unpack_bf16_pairs(u):
    lo = lax.bitcast_convert_type(lax.shift_left(u, jnp.uint32(16)), F32)
    hi = lax.bitcast_convert_type(jnp.bitwise_and(u, jnp.uint32(0xFFFF0000)), F32)
    return lo, hi


def _router_kernel(x_ref, gain_ref, wr_hi_ref, wr_lo_ref, later_ref, hpk_ref, meta_ref, wts_ref,
                   cnt_out_ref, cnt_ref):
    @pl.when(pl.program_id(0) == 0)
    def _():
        cnt_ref[...] = jnp.zeros_like(cnt_ref)

    x = x_ref[...]
    ms = jnp.mean(x * x, axis=-1, keepdims=True)
    h = (x * lax.rsqrt(ms + RMS_EPS)) * gain_ref[...]
    hpk_ref[...] = _pack_bf16_pairs(h)
    h_hi, h_lo = _split_bf16(h)
    logits = _dot(h_hi, wr_hi_ref[...]) + (_dot(h_hi, wr_lo_ref[...]) + _dot(h_lo, wr_hi_ref[...]))
    lt = logits.T
    sub = lax.broadcasted_iota(jnp.int32, (EXPERTS_PER_GROUP, ROUTE_ROWS), 0)
    neg = jnp.float32(-jnp.inf)
    big = jnp.int32(1 << 20)
    gl = jnp.where(sub < N_EXPERT_GROUPS, lt[N_EXPERTS:N_EXPERTS + EXPERTS_PER_GROUP], neg)
    gmax = jnp.max(gl, axis=0, keepdims=True)
    p_sel = 1.0 / jnp.sum(jnp.exp(gl - gmax), axis=0, keepdims=True)
    g_idx = jnp.min(jnp.where(gl == gmax, sub, big), axis=0, keepdims=True)
    el = lt[(N_EXPERT_GROUPS - 1) * EXPERTS_PER_GROUP:N_EXPERTS]
    for g in range(N_EXPERT_GROUPS - 2, -1, -1):
        el = jnp.where(g_idx == g, lt[g * EXPERTS_PER_GROUP:(g + 1) * EXPERTS_PER_GROUP], el)
    v1 = jnp.max(el, axis=0, keepdims=True)
    i1 = jnp.min(jnp.where(el == v1, sub, big), axis=0, keepdims=True)
    el2 = jnp.where(sub == i1, neg, el)
    v2 = jnp.max(el2, axis=0, keepdims=True)
    i2 = jnp.min(jnp.where(el2 == v2, sub, big), axis=0, keepdims=True)
    e2x = jnp.exp(v2 - v1)
    w1 = p_sel / (1.0 + e2x)
    w2 = p_sel * e2x / (1.0 + e2x)
    e1 = g_idx * EXPERTS_PER_GROUP + i1
    e2 = g_idx * EXPERTS_PER_GROUP + i2

    expert = lax.broadcasted_iota(jnp.int32, (N_EXPERTS, ROUTE_ROWS), 0)
    assigned = ((expert == e1) | (expert == e2)).astype(F32)
    earlier = cnt_ref[:, 0:1] + _dot(assigned.astype(BF16), later_ref[...])
    rank1 = jnp.sum(jnp.where(expert == e1, earlier, 0.0), axis=0, keepdims=True).astype(jnp.int32)
    rank2 = jnp.sum(jnp.where(expert == e2, earlier, 0.0), axis=0, keepdims=True).astype(jnp.int32)
    meta_ref[...] = jnp.where(sub == 0, e1, jnp.where(sub == 1, e2, jnp.where(
        sub == 2, rank1, jnp.where(sub == 3, rank2, 0))))
    cnt_ref[...] = cnt_ref[...] + jnp.sum(assigned, axis=1, keepdims=True)
    cnt_out_ref[...] = cnt_ref[...]

    w_rows = jnp.where(sub == 0, w1, jnp.where(sub == 1, w2, 0.0))
    pick = (lax.broadcasted_iota(jnp.int32, (EXPERTS_PER_GROUP, ROUTER_LANES), 0)
            == lax.broadcasted_iota(jnp.int32, (EXPERTS_PER_GROUP, ROUTER_LANES), 1))
    pick = pick.astype(F32).astype(BF16)
    p_hi = w_rows.astype(BF16)
    res = w_rows - p_hi.astype(F32)
    p_mid = res.astype(BF16)
    p_lo = (res - p_mid.astype(F32)).astype(BF16)
    wts_ref[...] = _dot_tn(p_hi, pick) + (_dot_tn(p_mid, pick) + _dot_tn(p_lo, pick))


def _router(x2d, gain, wr_hi, wr_lo, later):
    n = x2d.shape[0]
    assert n % ROUTE_ROWS == 0
    const2 = lambda i: (0, 0)
    rows = lambda i: (i, 0)
    return pl.pallas_call(
        _router_kernel,
        grid=(n // ROUTE_ROWS,),
        in_specs=[
            pl.BlockSpec((ROUTE_ROWS, D_MODEL), rows),
            pl.BlockSpec((1, D_MODEL), const2),
            pl.BlockSpec((D_MODEL, ROUTER_LANES), const2),
            pl.BlockSpec((D_MODEL, ROUTER_LANES), const2),
            pl.BlockSpec((ROUTE_ROWS, ROUTE_ROWS), const2),
        ],
        out_specs=[
            pl.BlockSpec((ROUTE_ROWS, PACKED), rows),
            pl.BlockSpec((EXPERTS_PER_GROUP, ROUTE_ROWS), lambda i: (0, i)),
            pl.BlockSpec((ROUTE_ROWS, ROUTER_LANES), rows),
            pl.BlockSpec((N_EXPERTS, ROUTER_LANES), const2),
        ],
        out_shape=[
            jax.ShapeDtypeStruct((n, PACKED), jnp.uint32),
            jax.ShapeDtypeStruct((EXPERTS_PER_GROUP, n), jnp.int32),
            jax.ShapeDtypeStruct((n, ROUTER_LANES), F32),
            jax.ShapeDtypeStruct((N_EXPERTS, ROUTER_LANES), F32),
        ],
        scratch_shapes=[pltpu.VMEM((N_EXPERTS, ROUTER_LANES), F32)],
        compiler_params=pltpu.CompilerParams(
            dimension_semantics=("arbitrary",), vmem_limit_bytes=VMEM_LIMIT_BYTES),
        name="router",
    )(x2d, gain, wr_hi, wr_lo, later)


def _positions_kernel(meta_ref, row_start_ref, pos_ref):
    meta = meta_ref[...]
    expert = lax.broadcasted_iota(jnp.int32, (N_EXPERTS, ROUTE_ROWS), 0)
    starts = row_start_ref[:, 0:1]
    base1 = jnp.sum(jnp.where(expert == meta[0:1], starts, 0), axis=0, keepdims=True)
    base2 = jnp.sum(jnp.where(expert == meta[1:2], starts, 0), axis=0, keepdims=True)
    sub = lax.broadcasted_iota(jnp.int32, meta.shape, 0)
    pos_ref[...] = jnp.where(sub == 0, base1 + meta[2:3], jnp.where(sub == 1, base2 + meta[3:4], 0))


def _positions(meta, row_start_lanes):
    n = meta.shape[1]
    return pl.pallas_call(
        _positions_kernel,
        grid=(n // ROUTE_ROWS,),
        in_specs=[pl.BlockSpec((EXPERTS_PER_GROUP, ROUTE_ROWS), lambda i: (0, i)),
                  pl.BlockSpec((N_EXPERTS, ROUTER_LANES), lambda i: (0, 0))],
        out_specs=pl.BlockSpec((EXPERTS_PER_GROUP, ROUTE_ROWS), lambda i: (0, i)),
        out_shape=jax.ShapeDtypeStruct(meta.shape, jnp.int32),
        compiler_params=pltpu.CompilerParams(
            dimension_semantics=("arbitrary",), vmem_limit_bytes=VMEM_LIMIT_BYTES),
        name="positions",
    )(meta, row_start_lanes)


def _row_copy(src_ref, src_row, dst_ref, dst_row, sem):
    return pltpu.make_async_copy(src_ref.at[pl.ds(src_row, 1)], dst_ref.at[pl.ds(dst_row, 1)], sem)


def _scatter_kernel(pos1_ref, pos2_ref, hpk_ref, sorted_ref, sem):
    def issue(i, carry):
        _row_copy(hpk_ref, i, sorted_ref, pos1_ref[i], sem).start(priority=0)
        _row_copy(hpk_ref, i, sorted_ref, pos2_ref[i], sem).start(priority=1)
        return carry

    lax.fori_loop(0, SCATTER_ROWS, issue, 0, unroll=DMA_UNROLL)

    def drain(i, carry):
        _row_copy(hpk_ref, 0, sorted_ref, 0, sem).wait()
        _row_copy(hpk_ref, 0, sorted_ref, 0, sem).wait()
        return carry

    lax.fori_loop(0, SCATTER_ROWS, drain, 0, unroll=DMA_UNROLL)


def _scatter(pos1, pos2, hpk, n_rows):
    n = hpk.shape[0]
    assert n % SCATTER_ROWS == 0
    idx = lambda i: (i,)
    return pl.pallas_call(
        _scatter_kernel,
        grid=(n // SCATTER_ROWS,),
        in_specs=[
            pl.BlockSpec((SCATTER_ROWS,), idx, memory_space=pltpu.SMEM),
            pl.BlockSpec((SCATTER_ROWS,), idx, memory_space=pltpu.SMEM),
            pl.BlockSpec((SCATTER_ROWS, PACKED), lambda i: (i, 0)),
        ],
        out_specs=pl.BlockSpec(memory_space=pl.ANY),
        out_shape=jax.ShapeDtypeStruct((n_rows, PACKED), jnp.uint32),
        scratch_shapes=[pltpu.SemaphoreType.DMA],
        compiler_params=pltpu.CompilerParams(
            dimension_semantics=("arbitrary",), vmem_limit_bytes=VMEM_LIMIT_BYTES),
        name="scatter_rows",
    )(pos1, pos2, hpk)


def _expert_kernel(tile_expert_ref, tile_rows_ref, n_used_ref, lhs_ref, wg_ref, wu_ref, wd_ref,
                   o_ref):
    del tile_expert_ref, n_used_ref
    n_valid = tile_rows_ref[pl.program_id(0)]

    @pl.when(n_valid > 0)
    def _():
        valid = lax.broadcasted_iota(jnp.int32, (EXPERT_TILE, 1), 0) < n_valid
        lo, hi = _unpack_bf16_pairs(lhs_ref[...])
        hh = jnp.concatenate([jnp.where(valid, lo, 0.0).astype(BF16),
                              jnp.where(valid, hi, 0.0).astype(BF16)], axis=1)
        w_gu = jnp.concatenate([wg_ref[0, 0].astype(BF16), wu_ref[0, 0].astype(BF16)], axis=1)
        gu = _dot(hh, w_gu)
        hid = _silu(gu[:, :D_EXPERT]) * gu[:, D_EXPERT:]
        o_ref[...] = _pack_bf16_pairs(_dot(hid.astype(BF16), wd_ref[0, 0].astype(BF16)))

    @pl.when(n_valid <= 0)
    def _():
        o_ref[...] = jnp.zeros_like(o_ref)


def _experts(tile_expert, tile_rows, n_used, sorted_rows, w_gate, w_up, w_down, layer):
    n_tiles = sorted_rows.shape[0] // EXPERT_TILE
    live = lambda i, nu: jnp.minimum(i, nu[0] - 1)
    expert = lambda i, te, tr, nu: (layer, te[live(i, nu)], 0, 0)
    return pl.pallas_call(
        _expert_kernel,
        grid_spec=pltpu.PrefetchScalarGridSpec(
            num_scalar_prefetch=3,
            grid=(n_tiles,),
            in_specs=[
                pl.BlockSpec((EXPERT_TILE, PACKED), lambda i, te, tr, nu: (live(i, nu), 0)),
                pl.BlockSpec((1, 1, D_MODEL, D_EXPERT), expert),
                pl.BlockSpec((1, 1, D_MODEL, D_EXPERT), expert),
                pl.BlockSpec((1, 1, D_EXPERT, D_MODEL), expert),
            ],
            out_specs=pl.BlockSpec((EXPERT_TILE, PACKED), lambda i, te, tr, nu: (i, 0)),
        ),
        out_shape=jax.ShapeDtypeStruct(sorted_rows.shape, jnp.uint32),
        compiler_params=pltpu.CompilerParams(
            dimension_semantics=("arbitrary",), vmem_limit_bytes=VMEM_LIMIT_BYTES),
        name="experts",
    )(tile_expert, tile_rows, n_used, sorted_rows, w_gate, w_up, w_down)


def _combine_kernel(pos1_ref, pos2_ref, x_ref, wts_ref, ys_ref, gfin_ref, o_ref, buf1_ref, buf2_ref,
                    sem, *, final_norm):
    def issue(i, carry):
        _row_copy(ys_ref, pos1_ref[i], buf1_ref, i, sem).start(priority=0)
        _row_copy(ys_ref, pos2_ref[i], buf2_ref, i, sem).start(priority=1)
        return carry

    lax.fori_loop(0, COMBINE_ROWS, issue, 0, unroll=DMA_UNROLL)

    def drain(i, carry):
        _row_copy(ys_ref, 0, buf1_ref, 0, sem).wait()
        _row_copy(ys_ref, 0, buf2_ref, 0, sem).wait()
        return carry

    lax.fori_loop(0, COMBINE_ROWS, drain, 0, unroll=DMA_UNROLL)
    lo1, hi1 = _unpack_bf16_pairs(buf1_ref[...])
    lo2, hi2 = _unpack_bf16_pairs(buf2_ref[...])
    w1 = wts_ref[:, 0:1]
    w2 = wts_ref[:, 1:2]
    x = x_ref[...]
    y = jnp.concatenate([x[:, :PACKED] + (w1 * lo1 + w2 * lo2),
                         x[:, PACKED:] + (w1 * hi1 + w2 * hi2)], axis=1)
    if final_norm:
        ms = jnp.mean(y * y, axis=-1, keepdims=True)
        y = (y * lax.rsqrt(ms + RMS_EPS)) * gfin_ref[...]
    o_ref[...] = y


def _combine(pos1, pos2, x2d, wts, ys, gain_final, final_norm):
    n = x2d.shape[0]
    assert n % COMBINE_ROWS == 0
    idx = lambda i: (i,)
    rows = lambda i: (i, 0)
    return pl.pallas_call(
        functools.partial(_combine_kernel, final_norm=final_norm),
        grid=(n // COMBINE_ROWS,),
        in_specs=[
            pl.BlockSpec((COMBINE_ROWS,), idx, memory_space=pltpu.SMEM),
            pl.BlockSpec((COMBINE_ROWS,), idx, memory_space=pltpu.SMEM),
            pl.BlockSpec((COMBINE_ROWS, D_MODEL), rows),
            pl.BlockSpec((COMBINE_ROWS, ROUTER_LANES), rows),
            pl.BlockSpec(memory_space=pl.ANY),
            pl.BlockSpec((1, D_MODEL), lambda i: (0, 0)),
        ],
        out_specs=pl.BlockSpec((COMBINE_ROWS, D_MODEL), rows),
        out_shape=jax.ShapeDtypeStruct(x2d.shape, F32),
        scratch_shapes=[
            pltpu.VMEM((COMBINE_ROWS, PACKED), jnp.uint32),
            pltpu.VMEM((COMBINE_ROWS, PACKED), jnp.uint32),
            pltpu.SemaphoreType.DMA,
        ],
        compiler_params=pltpu.CompilerParams(
            dimension_semantics=("arbitrary",), vmem_limit_bytes=VMEM_LIMIT_BYTES),
        name="combine",
    )(pos1, pos2, x2d, wts, ys, gain_final)


def _moe(x2d, gain, wr_hi, wr_lo, later, w_gate, w_up, w_down, layer, gain_final, final_norm):
    n = x2d.shape[0]
    max_tiles = (2 * n) // EXPERT_TILE + N_EXPERTS
    hpk, meta, wts, cnt = _router(x2d, gain, wr_hi, wr_lo, later)
    counts = cnt[:, 0].astype(jnp.int32)
    tiles_per = (counts + (EXPERT_TILE - 1)) // EXPERT_TILE
    tile_end = jnp.cumsum(tiles_per)
    tile_start = tile_end - tiles_per
    row_start = tile_start * EXPERT_TILE
    tile_ids = jnp.arange(max_tiles, dtype=jnp.int32)
    tile_expert = jnp.minimum(
        jnp.sum((tile_ids[:, None] >= tile_end[None, :]).astype(jnp.int32), axis=1), N_EXPERTS - 1)
    n_used = tile_end[-1:].astype(jnp.int32)
    rows_left = counts[tile_expert] - (tile_ids - tile_start[tile_expert]) * EXPERT_TILE
    tile_rows = jnp.where(tile_ids < n_used[0], jnp.clip(rows_left, 0, EXPERT_TILE), 0)
    pos = _positions(meta, jnp.broadcast_to(row_start[:, None], (N_EXPERTS, ROUTER_LANES)))
    sorted_rows = _scatter(pos[0], pos[1], hpk, max_tiles * EXPERT_TILE)
    ys = _experts(tile_expert, tile_rows.astype(jnp.int32), n_used, sorted_rows, w_gate, w_up, w_down,
                  layer)
    return _combine(pos[0], pos[1], x2d, wts, ys, gain_final, final_norm)


def kernel(x, lb_logits, norm_mix, w_in, hgrn_norm, tmlp_ln_g, tmlp_ln_b, w_spatial, b_spatial,
           w_out, norm_ffn, w_router_group, w_router_expert, w_gate, w_up, w_down, norm_final):
    depth = w_in.shape[0]
    bsz, seq, _ = x.shape
    p = jax.nn.softmax(lb_logits.astype(F32), axis=0)
    lower_bounds = jnp.cumsum(p, axis=0) - p[0:1]
    tril = jnp.tril(jnp.ones((SUB, SUB), dtype=bool))
    later = jnp.triu(jnp.ones((ROUTE_ROWS, ROUTE_ROWS), BF16), 1)
    for layer in range(depth):
        w_sp = jnp.where(tril[None], w_spatial[layer], 0.0).astype(BF16)
        b_sp = jnp.repeat(b_spatial[layer].T, GROUP_DIM, axis=1)
        x = _mixer(x, norm_mix[layer][None], w_in[layer].astype(BF16), lower_bounds[layer][None],
                   hgrn_norm[layer][None], tmlp_ln_g[layer][None], tmlp_ln_b[layer][None],
                   w_sp, b_sp, w_out[layer].astype(BF16))
        w_r = jnp.concatenate([w_router_expert[layer], w_router_group[layer]], axis=1)
        w_r = jnp.pad(w_r, ((0, 0), (0, ROUTER_LANES - w_r.shape[1])))
        wr_hi, wr_lo = _split_bf16(w_r)
        x2d = _moe(x.reshape(bsz * seq, D_MODEL), norm_ffn[layer][None], wr_hi, wr_lo, later,
                   w_gate, w_up, w_down, layer, norm_final[None], final_norm=(layer == depth - 1))
        x = x2d.reshape(bsz, seq, D_MODEL)
    return x
```

```python
import functools

import jax
import jax.numpy as jnp
from jax import lax
from jax.experimental import pallas as pl
from jax.experimental.pallas import tpu as pltpu

F32 = jnp.float32
BF16 = jnp.bfloat16

D_MODEL = 1024
N_HEADS = 4
HEAD_DIM = 128
D_HGRN = N_HEADS * HEAD_DIM
N_GROUPS = 4
GROUP_DIM = 128
D_TMLP = N_GROUPS * GROUP_DIM
D_IN = 4 * D_HGRN + 2 * D_TMLP
SUB = 128
N_EXPERT_GROUPS = 4
EXPERTS_PER_GROUP = 8
N_EXPERTS = N_EXPERT_GROUPS * EXPERTS_PER_GROUP
D_EXPERT = 256
ROUTER_LANES = 128
RMS_EPS = 1e-6
LN_EPS = 1e-5
F_FLOOR = 1e-30
HGRN_SAFE_EXP = 60.0
SQRT_HALF = 0.7071067811865476

MIXER_ROWS = 512
PROJ_PIECE = 256
ROUTE_ROWS = 1024
SCATTER_ROWS = 4096
COMBINE_ROWS = 2048
EXPERT_TILE = 512
PACKED = D_MODEL // 2
DMA_UNROLL = 8
VMEM_LIMIT_BYTES = 56 * 1024 * 1024


def _dot(a, b):
    return jnp.dot(a, b, preferred_element_type=F32)


def _dot_nt(a, b):
    return lax.dot_general(a, b, (((1,), (1,)), ((), ())), preferred_element_type=F32)


def _dot_tn(a, b):
    return lax.dot_general(a, b, (((0,), (0,)), ((), ())), preferred_element_type=F32)


def _split_bf16(a):
    hi = a.astype(BF16)
    lo = (a - hi.astype(F32)).astype(BF16)
    return hi, lo


def _gelu(a):
    return 0.5 * a * (1.0 + lax.erf(a * SQRT_HALF))


def _silu(a):
    return a * jax.nn.sigmoid(a)


def _boundary_rows(b_ref, r0, m, width):
    pieces = []
    if 2 * m >= 8:
        for s0 in range(0, SUB, 2 * m):
            row = b_ref[pl.ds(r0 + (s0 + m - 1), 1), :]
            pieces.append(jnp.broadcast_to(row, (2 * m, width)))
    else:
        row8 = lax.broadcasted_iota(jnp.int32, (8, width), 0)
        for g0 in range(0, SUB, 8):
            acc = None
            for s0 in range(0, 8, 2 * m):
                row = jnp.broadcast_to(b_ref[pl.ds(r0 + (g0 + s0 + m - 1), 1), :], (8, width))
                acc = row if acc is None else jnp.where(row8 >= s0, row, acc)
            pieces.append(acc)
    return jnp.concatenate(pieces, axis=0)


def _half_middle_rows(b_ref, r0, width):
    half = SUB // 2
    return jnp.concatenate(
        [jnp.broadcast_to(b_ref[pl.ds(r0 + (s0 + half // 2 - 1), 1), :], (half, width))
         for s0 in range(0, SUB, half)], axis=0)


def _hgrn_level(att, q, kk, b, b_ref, r0, m, tx, row):
    ref_pt = _boundary_rows(b_ref, r0, m, D_HGRN)
    decay = jnp.exp(-jnp.abs(b - ref_pt))
    right = jnp.bitwise_and(row, m) != 0
    qt = jnp.where(right, q * decay, 0.0).astype(BF16)
    kt = jnp.where(right, 0.0, kk * decay).astype(BF16)
    same_block = tx < 2 * m
    out = []
    for h in range(N_HEADS):
        sl = slice(h * HEAD_DIM, (h + 1) * HEAD_DIM)
        term = jnp.where(same_block, _dot_nt(qt[:, sl], kt[:, sl]), 0.0)
        out.append(term if att is None else att[h] + term)
    return out


def _hgrn_attention(q, kk, b, b_ref, r0, shared_reference):
    t_idx = lax.broadcasted_iota(jnp.int32, (SUB, SUB), 0)
    s_idx = lax.broadcasted_iota(jnp.int32, (SUB, SUB), 1)
    tx = jnp.bitwise_xor(t_idx, s_idx)
    row = lax.broadcasted_iota(jnp.int32, (SUB, D_HGRN), 0)
    half = SUB // 2
    att = _hgrn_level(None, q, kk, b, b_ref, r0, half, tx, row)
    if shared_reference:
        expo = b - _half_middle_rows(b_ref, r0, D_HGRN)
        qt = (q * jnp.exp(expo)).astype(BF16)
        kt = (kk * jnp.exp(-expo)).astype(BF16)
        keep = (tx < half) & (s_idx <= t_idx)
        for h in range(N_HEADS):
            sl = slice(h * HEAD_DIM, (h + 1) * HEAD_DIM)
            att[h] = att[h] + jnp.where(keep, _dot_nt(qt[:, sl], kt[:, sl]), 0.0)
        return att
    qb = q.astype(BF16)
    kb = kk.astype(BF16)
    for h in range(N_HEADS):
        sl = slice(h * HEAD_DIM, (h + 1) * HEAD_DIM)
        att[h] = att[h] + jnp.where(tx == 0, _dot_nt(qb[:, sl], kb[:, sl]), 0.0)
    m = 1
    while m < half:
        att = _hgrn_level(att, q, kk, b, b_ref, r0, m, tx, row)
        m *= 2
    return att


def _hgrn_sub_chunk(z_ref, kk_ref, b_ref, hn_ref, y_ref, r0, state, shared_reference, per_head=None):
    rows = pl.ds(r0, SUB)
    q = _silu(z_ref[rows, 0:D_HGRN])
    kk = kk_ref[rows, :]
    b = b_ref[rows, :]
    v = z_ref[rows, 2 * D_HGRN:3 * D_HGRN].astype(BF16)
    att = _hgrn_attention(q, kk, b, b_ref, r0, shared_reference)
    b_end = b_ref[pl.ds(r0 + (SUB - 1), 1), :]
    q0 = (q * jnp.exp(b)).astype(BF16)
    k_end = (kk * jnp.exp(b_end - b)).astype(BF16)
    s_decay = jnp.exp(b_end)
    g = _silu(z_ref[rows, 3 * D_HGRN:4 * D_HGRN])
    new_state = []
    for hd in range(N_HEADS):
        if per_head is not None:
            per_head(hd)
        sl = slice(hd * HEAD_DIM, (hd + 1) * HEAD_DIM)
        st = state[hd]
        o = _dot(att[hd].astype(BF16), v[:, sl]) + _dot_nt(q0[:, sl], st.astype(BF16))
        new_state.append(st * s_decay[:, sl] + _dot_tn(v[:, sl], k_end[:, sl]))
        oms = jnp.mean(o * o, axis=-1, keepdims=True)
        on = (o * lax.rsqrt(oms + RMS_EPS)) * hn_ref[...]
        y_ref[rows, sl] = (on * g[:, sl]).astype(BF16)
    return new_state


def _mixer_tile(z_ref, zn_ref, h_next, win_ref, lb_ref, hn_ref, lng_ref, lnb_ref, wsp_ref, bsp_ref,
                y_ref, b_ref, kk_ref, st_ref):
    def project(first, count):
        for p in range(first, first + count):
            cols = slice(p * PROJ_PIECE, (p + 1) * PROJ_PIECE)
            zn_ref[:, cols] = _dot(h_next, win_ref[:, cols])

    n_pieces = D_IN // PROJ_PIECE
    early = n_pieces // 3

    t_idx = lax.broadcasted_iota(jnp.int32, (SUB, SUB), 0)
    s_idx = lax.broadcasted_iota(jnp.int32, (SUB, SUB), 1)
    tri = (s_idx <= t_idx).astype(BF16)
    n_sub = MIXER_ROWS // SUB

    worst = jnp.zeros((SUB, D_HGRN), F32)
    for c in range(n_sub):
        rows = pl.ds(c * SUB, SUB)
        lb = lb_ref[...]
        fg = lb + (1.0 - lb) * jax.nn.sigmoid(z_ref[rows, D_HGRN:2 * D_HGRN])
        lf_hi, lf_lo = _split_bf16(jnp.log(jnp.maximum(fg, F_FLOOR)))
        kk_ref[rows, :] = 1.0 - fg
        b = _dot(tri, lf_hi) + _dot(tri, lf_lo)
        b_ref[rows, :] = b
        worst = jnp.maximum(worst, jnp.abs(b - _half_middle_rows(b_ref, c * SUB, D_HGRN)))
    shared_ok = jnp.max(worst) <= HGRN_SAFE_EXP

    for c in range(n_sub):
        project(c * early // n_sub, (c + 1) * early // n_sub - c * early // n_sub)
        rows = pl.ds(c * SUB, SUB)
        u = _gelu(z_ref[rows, 4 * D_HGRN:4 * D_HGRN + D_TMLP])
        vv = _gelu(z_ref[rows, 4 * D_HGRN + D_TMLP:D_IN])
        for gi in range(N_GROUPS):
            sl = slice(gi * GROUP_DIM, (gi + 1) * GROUP_DIM)
            vg = vv[:, sl]
            mu = jnp.mean(vg, axis=-1, keepdims=True)
            cen = vg - mu
            var = jnp.mean(cen * cen, axis=-1, keepdims=True)
            vn = (cen * lax.rsqrt(var + LN_EPS)) * lng_ref[:, sl] + lnb_ref[:, sl]
            mixed = _dot(wsp_ref[gi], vn.astype(BF16)) + bsp_ref[:, sl]
            y_ref[rows, D_HGRN + gi * GROUP_DIM:D_HGRN + (gi + 1) * GROUP_DIM] = (
                u[:, sl] * mixed).astype(BF16)

    late = n_pieces - early

    @pl.when(shared_ok)
    def _():
        state = [st_ref[hd] for hd in range(N_HEADS)]
        stride = (n_sub * N_HEADS) // late

        def piece_for_slot(slot):
            if slot % stride == 0:
                project(early + slot // stride, 1)

        for c in range(n_sub):
            state = _hgrn_sub_chunk(z_ref, kk_ref, b_ref, hn_ref, y_ref, c * SUB, state, True,
                                    lambda hd, c=c: piece_for_slot(c * N_HEADS + hd))
        for hd in range(N_HEADS):
            st_ref[hd] = state[hd]

    @pl.when(jnp.logical_not(shared_ok))
    def _():
        project(early, late)

        def sub_chunk(c, carry):
            r0 = pl.multiple_of(c * SUB, SUB)
            state = [st_ref[hd] for hd in range(N_HEADS)]
            state = _hgrn_sub_chunk(z_ref, kk_ref, b_ref, hn_ref, y_ref, r0, state, False)
            for hd in range(N_HEADS):
                st_ref[hd] = state[hd]
            return carry

        lax.fori_loop(0, n_sub, sub_chunk, 0)


def _mixer_kernel(x_ref, xn_ref, gain_ref, win_ref, lb_ref, hn_ref, lng_ref, lnb_ref, wsp_ref, bsp_ref,
                  wout_ref, o_ref, za_ref, zb_ref, y_ref, b_ref, kk_ref, st_ref):
    def normed(xv):
        ms = jnp.mean(xv * xv, axis=-1, keepdims=True)
        return ((xv * lax.rsqrt(ms + RMS_EPS)) * gain_ref[...]).astype(BF16)

    @pl.when(pl.program_id(1) == 0)
    def _():
        st_ref[...] = jnp.zeros_like(st_ref)

    @pl.when((pl.program_id(0) == 0) & (pl.program_id(1) == 0))
    def _():
        za_ref[...] = _dot(normed(x_ref[0, 0:MIXER_ROWS, :]), win_ref[...])

    tile_refs = (win_ref, lb_ref, hn_ref, lng_ref, lnb_ref, wsp_ref, bsp_ref)
    _mixer_tile(za_ref, zb_ref, normed(x_ref[0, MIXER_ROWS:2 * MIXER_ROWS, :]), *tile_refs,
                y_ref, b_ref, kk_ref, st_ref)
    o_ref[0, 0:MIXER_ROWS, :] = x_ref[0, 0:MIXER_ROWS, :] + _dot(y_ref[...], wout_ref[...])
    _mixer_tile(zb_ref, za_ref, normed(xn_ref[0]), *tile_refs, y_ref, b_ref, kk_ref, st_ref)
    o_ref[0, MIXER_ROWS:2 * MIXER_ROWS, :] = (
        x_ref[0, MIXER_ROWS:2 * MIXER_ROWS, :] + _dot(y_ref[...], wout_ref[...]))


def _mixer(x, gain, w_in, lb, hgrn_norm, ln_g, ln_b, w_sp, b_sp, w_out):
    bsz, seq, _ = x.shape
    step_rows = 2 * MIXER_ROWS
    assert seq % step_rows == 0 and MIXER_ROWS % SUB == 0
    steps = seq // step_rows
    const2 = lambda b, j: (0, 0)
    const3 = lambda b, j: (0, 0, 0)
    once = pl.Buffered(1)

    def next_tile_a(b, j):
        nxt = jnp.minimum(b * steps + j + 1, bsz * steps - 1)
        return (nxt // steps, 2 * (nxt % steps), 0)

    return pl.pallas_call(
        _mixer_kernel,
        grid=(bsz, steps),
        in_specs=[
            pl.BlockSpec((1, step_rows, D_MODEL), lambda b, j: (b, j, 0)),
            pl.BlockSpec((1, MIXER_ROWS, D_MODEL), next_tile_a),
            pl.BlockSpec((1, D_MODEL), const2),
            pl.BlockSpec((D_MODEL, D_IN), const2, pipeline_mode=once),
            pl.BlockSpec((1, D_HGRN), const2),
            pl.BlockSpec((1, HEAD_DIM), const2),
            pl.BlockSpec((1, D_TMLP), const2),
            pl.BlockSpec((1, D_TMLP), const2),
            pl.BlockSpec((N_GROUPS, SUB, SUB), const3),
            pl.BlockSpec((SUB, D_TMLP), const2),
            pl.BlockSpec((D_HGRN + D_TMLP, D_MODEL), const2, pipeline_mode=once),
        ],
        out_specs=pl.BlockSpec((1, step_rows, D_MODEL), lambda b, j: (b, j, 0)),
        out_shape=jax.ShapeDtypeStruct(x.shape, F32),
        scratch_shapes=[
            pltpu.VMEM((MIXER_ROWS, D_IN), F32),
            pltpu.VMEM((MIXER_ROWS, D_IN), F32),
            pltpu.VMEM((MIXER_ROWS, D_HGRN + D_TMLP), BF16),
            pltpu.VMEM((MIXER_ROWS, D_HGRN), F32),
            pltpu.VMEM((MIXER_ROWS, D_HGRN), F32),
            pltpu.VMEM((N_HEADS, HEAD_DIM, HEAD_DIM), F32),
        ],
        compiler_params=pltpu.CompilerParams(
            dimension_semantics=("arbitrary", "arbitrary"),
            vmem_limit_bytes=VMEM_LIMIT_BYTES),
        name="mixer",
    )(x, x, gain, w_in, lb, hgrn_norm, ln_g, ln_b, w_sp, b_sp, w_out)


def _pack_bf16_pairs(a):
    lo = lax.bitcast_convert_type(a[:, :PACKED].astype(BF16).astype(F32), jnp.uint32)
    hi = lax.bitcast_convert_type(a[:, PACKED:].astype(BF16).astype(F32), jnp.uint32)
    return jnp.bitwise_or(lax.shift_right_logical(lo, jnp.uint32(16)),
                          jnp.bitwise_and(hi, jnp.uint32(0xFFFF0000)))


def _unpack_bf16_pairs(u):
    lo = lax.bitcast_convert_type(lax.shift_left(u, jnp.uint32(16)), F32)
    hi = lax.bitcast_convert_type(jnp.bitwise_and(u, jnp.uint32(0xFFFF0000)), F32)
    return lo, hi


def _router_kernel(x_ref, gain_ref, wr_hi_ref, wr_lo_ref, later_ref, hpk_ref, meta_ref, wts_ref,
                   cnt_out_ref, cnt_ref):
    @pl.when(pl.program_id(0) == 0)
    def _():
        cnt_ref[...] = jnp.zeros_like(cnt_ref)

    x = x_ref[...]
    ms = jnp.mean(x * x, axis=-1, keepdims=True)
    h = (x * lax.rsqrt(ms + RMS_EPS)) * gain_ref[...]
    hpk_ref[...] = _pack_bf16_pairs(h)
    h_hi, h_lo = _split_bf16(h)
    logits = _dot(h_hi, wr_hi_ref[...]) + (_dot(h_hi, wr_lo_ref[...]) + _dot(h_lo, wr_hi_ref[...]))
    lt = logits.T
    sub = lax.broadcasted_iota(jnp.int32, (EXPERTS_PER_GROUP, ROUTE_ROWS), 0)
    neg = jnp.float32(-jnp.inf)
    big = jnp.int32(1 << 20)
    gl = jnp.where(sub < N_EXPERT_GROUPS, lt[N_EXPERTS:N_EXPERTS + EXPERTS_PER_GROUP], neg)
    gmax = jnp.max(gl, axis=0, keepdims=True)
    p_sel = 1.0 / jnp.sum(jnp.exp(gl - gmax), axis=0, keepdims=True)
    g_idx = jnp.min(jnp.where(gl == gmax, sub, big), axis=0, keepdims=True)
    el = lt[(N_EXPERT_GROUPS - 1) * EXPERTS_PER_GROUP:N_EXPERTS]
    for g in range(N_EXPERT_GROUPS - 2, -1, -1):
        el = jnp.where(g_idx == g, lt[g * EXPERTS_PER_GROUP:(g + 1) * EXPERTS_PER_GROUP], el)
    v1 = jnp.max(el, axis=0, keepdims=True)
    i1 = jnp.min(jnp.where(el == v1, sub, big), axis=0, keepdims=True)
    el2 = jnp.where(sub == i1, neg, el)
    v2 = jnp.max(el2, axis=0, keepdims=True)
    i2 = jnp.min(jnp.where(el2 == v2, sub, big), axis=0, keepdims=True)
    e2x = jnp.exp(v2 - v1)
    w1 = p_sel / (1.0 + e2x)
    w2 = p_sel * e2x / (1.0 + e2x)
    e1 = g_idx * EXPERTS_PER_GROUP + i1
    e2 = g_idx * EXPERTS_PER_GROUP + i2

    expert = lax.broadcasted_iota(jnp.int32, (N_EXPERTS, ROUTE_ROWS), 0)
    assigned = ((expert == e1) | (expert == e2)).astype(F32)
    earlier = cnt_ref[:, 0:1] + _dot(assigned.astype(BF16), later_ref[...])
    rank1 = jnp.sum(jnp.where(expert == e1, earlier, 0.0), axis=0, keepdims=True).astype(jnp.int32)
    rank2 = jnp.sum(jnp.where(expert == e2, earlier, 0.0), axis=0, keepdims=True).astype(jnp.int32)
    meta_ref[...] = jnp.where(sub == 0, e1, jnp.where(sub == 1, e2, jnp.where(
        sub == 2, rank1, jnp.where(sub == 3, rank2, 0))))
    cnt_ref[...] = cnt_ref[...] + jnp.sum(assigned, axis=1, keepdims=True)
    cnt_out_ref[...] = cnt_ref[...]

    w_rows = jnp.where(sub == 0, w1, jnp.where(sub == 1, w2, 0.0))
    pick = (lax.broadcasted_iota(jnp.int32, (EXPERTS_PER_GROUP, ROUTER_LANES), 0)
            == lax.broadcasted_iota(jnp.int32, (EXPERTS_PER_GROUP, ROUTER_LANES), 1))
    pick = pick.astype(F32).astype(BF16)
    p_hi = w_rows.astype(BF16)
    res = w_rows - p_hi.astype(F32)
    p_mid = res.astype(BF16)
    p_lo = (res - p_mid.astype(F32)).astype(BF16)
    wts_ref[...] = _dot_tn(p_hi, pick) + (_dot_tn(p_mid, pick) + _dot_tn(p_lo, pick))


def _router(x2d, gain, wr_hi, wr_lo, later):
    n = x2d.shape[0]
    assert n % ROUTE_ROWS == 0
    const2 = lambda i: (0, 0)
    rows = lambda i: (i, 0)
    return pl.pallas_call(
        _router_kernel,
        grid=(n // ROUTE_ROWS,),
        in_specs=[
            pl.BlockSpec((ROUTE_ROWS, D_MODEL), rows),
            pl.BlockSpec((1, D_MODEL), const2),
            pl.BlockSpec((D_MODEL, ROUTER_LANES), const2),
            pl.BlockSpec((D_MODEL, ROUTER_LANES), const2),
            pl.BlockSpec((ROUTE_ROWS, ROUTE_ROWS), const2),
        ],
        out_specs=[
            pl.BlockSpec((ROUTE_ROWS, PACKED), rows),
            pl.BlockSpec((EXPERTS_PER_GROUP, ROUTE_ROWS), lambda i: (0, i)),
            pl.BlockSpec((ROUTE_ROWS, ROUTER_LANES), rows),
            pl.BlockSpec((N_EXPERTS, ROUTER_LANES), const2),
        ],
        out_shape=[
            jax.ShapeDtypeStruct((n, PACKED), jnp.uint32),
            jax.ShapeDtypeStruct((EXPERTS_PER_GROUP, n), jnp.int32),
            jax.ShapeDtypeStruct((n, ROUTER_LANES), F32),
            jax.ShapeDtypeStruct((N_EXPERTS, ROUTER_LANES), F32),
        ],
        scratch_shapes=[pltpu.VMEM((N_EXPERTS, ROUTER_LANES), F32)],
        compiler_params=pltpu.CompilerParams(
            dimension_semantics=("arbitrary",), vmem_limit_bytes=VMEM_LIMIT_BYTES),
        name="router",
    )(x2d, gain, wr_hi, wr_lo, later)


def _positions_kernel(meta_ref, row_start_ref, pos_ref):
    meta = meta_ref[...]
    expert = lax.broadcasted_iota(jnp.int32, (N_EXPERTS, ROUTE_ROWS), 0)
    starts = row_start_ref[:, 0:1]
    base1 = jnp.sum(jnp.where(expert == meta[0:1], starts, 0), axis=0, keepdims=True)
    base2 = jnp.sum(jnp.where(expert == meta[1:2], starts, 0), axis=0, keepdims=True)
    sub = lax.broadcasted_iota(jnp.int32, meta.shape, 0)
    pos_ref[...] = jnp.where(sub == 0, base1 + meta[2:3], jnp.where(sub == 1, base2 + meta[3:4], 0))


def _positions(meta, row_start_lanes):
    n = meta.shape[1]
    return pl.pallas_call(
        _positions_kernel,
        grid=(n // ROUTE_ROWS,),
        in_specs=[pl.BlockSpec((EXPERTS_PER_GROUP, ROUTE_ROWS), lambda i: (0, i)),
                  pl.BlockSpec((N_EXPERTS, ROUTER_LANES), lambda i: (0, 0))],
        out_specs=pl.BlockSpec((EXPERTS_PER_GROUP, ROUTE_ROWS), lambda i: (0, i)),
        out_shape=jax.ShapeDtypeStruct(meta.shape, jnp.int32),
        compiler_params=pltpu.CompilerParams(
            dimension_semantics=("arbitrary",), vmem_limit_bytes=VMEM_LIMIT_BYTES),
        name="positions",
    )(meta, row_start_lanes)


def _row_copy(src_ref, src_row, dst_ref, dst_row, sem):
    return pltpu.make_async_copy(src_ref.at[pl.ds(src_row, 1)], dst_ref.at[pl.ds(dst_row, 1)], sem)


def _scatter_kernel(pos1_ref, pos2_ref, hpk_ref, sorted_ref, sem):
    def issue(i, carry):
        _row_copy(hpk_ref, i, sorted_ref, pos1_ref[i], sem).start(priority=0)
        _row_copy(hpk_ref, i, sorted_ref, pos2_ref[i], sem).start(priority=1)
        return carry

    lax.fori_loop(0, SCATTER_ROWS, issue, 0, unroll=DMA_UNROLL)

    def drain(i, carry):
        _row_copy(hpk_ref, 0, sorted_ref, 0, sem).wait()
        _row_copy(hpk_ref, 0, sorted_ref, 0, sem).wait()
        return carry

    lax.fori_loop(0, SCATTER_ROWS, drain, 0, unroll=DMA_UNROLL)


def _scatter(pos1, pos2, hpk, n_rows):
    n = hpk.shape[0]
    assert n % SCATTER_ROWS == 0
    idx = lambda i: (i,)
    return pl.pallas_call(
        _scatter_kernel,
        grid=(n // SCATTER_ROWS,),
        in_specs=[
            pl.BlockSpec((SCATTER_ROWS,), idx, memory_space=pltpu.SMEM),
            pl.BlockSpec((SCATTER_ROWS,), idx, memory_space=pltpu.SMEM),
            pl.BlockSpec((SCATTER_ROWS, PACKED), lambda i: (i, 0)),
        ],
        out_specs=pl.BlockSpec(memory_space=pl.ANY),
        out_shape=jax.ShapeDtypeStruct((n_rows, PACKED), jnp.uint32),
        scratch_shapes=[pltpu.SemaphoreType.DMA],
        compiler_params=pltpu.CompilerParams(
            dimension_semantics=("arbitrary",), vmem_limit_bytes=VMEM_LIMIT_BYTES),
        name="scatter_rows",
    )(pos1, pos2, hpk)


def _expert_kernel(tile_expert_ref, tile_rows_ref, n_used_ref, lhs_ref, wg_ref, wu_ref, wd_ref,
                   o_ref):
    del tile_expert_ref, n_used_ref
    n_valid = tile_rows_ref[pl.program_id(0)]

    @pl.when(n_valid > 0)
    def _():
        valid = lax.broadcasted_iota(jnp.int32, (EXPERT_TILE, 1), 0) < n_valid
        lo, hi = _unpack_bf16_pairs(lhs_ref[...])
        hh = jnp.concatenate([jnp.where(valid, lo, 0.0).astype(BF16),
                              jnp.where(valid, hi, 0.0).astype(BF16)], axis=1)
        w_gu = jnp.concatenate([wg_ref[0, 0].astype(BF16), wu_ref[0, 0].astype(BF16)], axis=1)
        gu = _dot(hh, w_gu)
        hid = _silu(gu[:, :D_EXPERT]) * gu[:, D_EXPERT:]
        o_ref[...] = _pack_bf16_pairs(_dot(hid.astype(BF16), wd_ref[0, 0].astype(BF16)))

    @pl.when(n_valid <= 0)
    def _():
        o_ref[...] = jnp.zeros_like(o_ref)


def _experts(tile_expert, tile_rows, n_used, sorted_rows, w_gate, w_up, w_down, layer):
    n_tiles = sorted_rows.shape[0] // EXPERT_TILE
    live = lambda i, nu: jnp.minimum(i, nu[0] - 1)
    expert = lambda i, te, tr, nu: (layer, te[live(i, nu)], 0, 0)
    return pl.pallas_call(
        _expert_kernel,
        grid_spec=pltpu.PrefetchScalarGridSpec(
            num_scalar_prefetch=3,
            grid=(n_tiles,),
            in_specs=[
                pl.BlockSpec((EXPERT_TILE, PACKED), lambda i, te, tr, nu: (live(i, nu), 0)),
                pl.BlockSpec((1, 1, D_MODEL, D_EXPERT), expert),
                pl.BlockSpec((1, 1, D_MODEL, D_EXPERT), expert),
                pl.BlockSpec((1, 1, D_EXPERT, D_MODEL), expert),
            ],
            out_specs=pl.BlockSpec((EXPERT_TILE, PACKED), lambda i, te, tr, nu: (i, 0)),
        ),
        out_shape=jax.ShapeDtypeStruct(sorted_rows.shape, jnp.uint32),
        compiler_params=pltpu.CompilerParams(
            dimension_semantics=("arbitrary",), vmem_limit_bytes=VMEM_LIMIT_BYTES),
        name="experts",
    )(tile_expert, tile_rows, n_used, sorted_rows, w_gate, w_up, w_down)


def _combine_kernel(pos1_ref, pos2_ref, x_ref, wts_ref, ys_ref, gfin_ref, o_ref, buf1_ref, buf2_ref,
                    sem, *, final_norm):
    def issue(i, carry):
        _row_copy(ys_ref, pos1_ref[i], buf1_ref, i, sem).start(priority=0)
        _row_copy(ys_ref, pos2_ref[i], buf2_ref, i, sem).start(priority=1)
        return carry

    lax.fori_loop(0, COMBINE_ROWS, issue, 0, unroll=DMA_UNROLL)

    def drain(i, carry):
        _row_copy(ys_ref, 0, buf1_ref, 0, sem).wait()
        _row_copy(ys_ref, 0, buf2_ref, 0, sem).wait()
        return carry

    lax.fori_loop(0, COMBINE_ROWS, drain, 0, unroll=DMA_UNROLL)
    lo1, hi1 = _unpack_bf16_pairs(buf1_ref[...])
    lo2, hi2 = _unpack_bf16_pairs(buf2_ref[...])
    w1 = wts_ref[:, 0:1]
    w2 = wts_ref[:, 1:2]
    x = x_ref[...]
    y = jnp.concatenate([x[:, :PACKED] + (w1 * lo1 + w2 * lo2),
                         x[:, PACKED:] + (w1 * hi1 + w2 * hi2)], axis=1)
    if final_norm:
        ms = jnp.mean(y * y, axis=-1, keepdims=True)
        y = (y * lax.rsqrt(ms + RMS_EPS)) * gfin_ref[...]
    o_ref[...] = y


def _combine(pos1, pos2, x2d, wts, ys, gain_final, final_norm):
    n = x2d.shape[0]
    assert n % COMBINE_ROWS == 0
    idx = lambda i: (i,)
    rows = lambda i: (i, 0)
    return pl.pallas_call(
        functools.partial(_combine_kernel, final_norm=final_norm),
        grid=(n // COMBINE_ROWS,),
        in_specs=[
            pl.BlockSpec((COMBINE_ROWS,), idx, memory_space=pltpu.SMEM),
            pl.BlockSpec((COMBINE_ROWS,), idx, memory_space=pltpu.SMEM),
            pl.BlockSpec((COMBINE_ROWS, D_MODEL), rows),
            pl.BlockSpec((COMBINE_ROWS, ROUTER_LANES), rows),
            pl.BlockSpec(memory_space=pl.ANY),
            pl.BlockSpec((1, D_MODEL), lambda i: (0, 0)),
        ],
        out_specs=pl.BlockSpec((COMBINE_ROWS, D_MODEL), rows),
        out_shape=jax.ShapeDtypeStruct(x2d.shape, F32),
        scratch_shapes=[
            pltpu.VMEM((COMBINE_ROWS, PACKED), jnp.uint32),
            pltpu.VMEM((COMBINE_ROWS, PACKED), jnp.uint32),
            pltpu.SemaphoreType.DMA,
        ],
        compiler_params=pltpu.CompilerParams(
            dimension_semantics=("arbitrary",), vmem_limit_bytes=VMEM_LIMIT_BYTES),
        name="combine",
    )(pos1, pos2, x2d, wts, ys, gain_final)


def _moe(x2d, gain, wr_hi, wr_lo, later, w_gate, w_up, w_down, layer, gain_final, final_norm):
    n = x2d.shape[0]
    max_tiles = (2 * n) // EXPERT_TILE + N_EXPERTS
    hpk, meta, wts, cnt = _router(x2d, gain, wr_hi, wr_lo, later)
    counts = cnt[:, 0].astype(jnp.int32)
    tiles_per = (counts + (EXPERT_TILE - 1)) // EXPERT_TILE
    tile_end = jnp.cumsum(tiles_per)
    tile_start = tile_end - tiles_per
    row_start = tile_start * EXPERT_TILE
    tile_ids = jnp.arange(max_tiles, dtype=jnp.int32)
    tile_expert = jnp.minimum(
        jnp.sum((tile_ids[:, None] >= tile_end[None, :]).astype(jnp.int32), axis=1), N_EXPERTS - 1)
    n_used = tile_end[-1:].astype(jnp.int32)
    rows_left = counts[tile_expert] - (tile_ids - tile_start[tile_expert]) * EXPERT_TILE
    tile_rows = jnp.where(tile_ids < n_used[0], jnp.clip(rows_left, 0, EXPERT_TILE), 0)
    pos = _positions(meta, jnp.broadcast_to(row_start[:, None], (N_EXPERTS, ROUTER_LANES)))
    sorted_rows = _scatter(pos[0], pos[1], hpk, max_tiles * EXPERT_TILE)
    ys = _experts(tile_expert, tile_rows.astype(jnp.int32), n_used, sorted_rows, w_gate, w_up, w_down,
                  layer)
    return _combine(pos[0], pos[1], x2d, wts, ys, gain_final, final_norm)


def kernel(x, lb_logits, norm_mix, w_in, hgrn_norm, tmlp_ln_g, tmlp_ln_b, w_spatial, b_spatial,
           w_out, norm_ffn, w_router_group, w_router_expert, w_gate, w_up, w_down, norm_final):
    depth = w_in.shape[0]
    bsz, seq, _ = x.shape
    p = jax.nn.softmax(lb_logits.astype(F32), axis=0)
    lower_bounds = jnp.cumsum(p, axis=0) - p[0:1]
    tril = jnp.tril(jnp.ones((SUB, SUB), dtype=bool))
    later = jnp.triu(jnp.ones((ROUTE_ROWS, ROUTE_ROWS), BF16), 1)
    for layer in range(depth):
        w_sp = jnp.where(tril[None], w_spatial[layer], 0.0).astype(BF16)
        b_sp = jnp.repeat(b_spatial[layer].T, GROUP_DIM, axis=1)
        x = _mixer(x, norm_mix[layer][None], w_in[layer].astype(BF16), lower_bounds[layer][None],
                   hgrn_norm[layer][None], tmlp_ln_g[layer][None], tmlp_ln_b[layer][None],
                   w_sp, b_sp, w_out[layer].astype(BF16))
        w_r = jnp.concatenate([w_router_expert[layer], w_router_group[layer]], axis=1)
        w_r = jnp.pad(w_r, ((0, 0), (0, ROUTER_LANES - w_r.shape[1])))
        wr_hi, wr_lo = _split_bf16(w_r)
        x2d = _moe(x.reshape(bsz * seq, D_MODEL), norm_ffn[layer][None], wr_hi, wr_lo, later,
                   w_gate, w_up, w_down, layer, norm_final[None], final_norm=(layer == depth - 1))
        x = x2d.reshape(bsz, seq, D_MODEL)
    return x
```

```python
import functools

import jax
import jax.numpy as jnp
from jax import lax
from jax.experimental import pallas as pl
from jax.experimental.pallas import tpu as pltpu

F32 = jnp.float32
BF16 = jnp.bfloat16

D_MODEL = 1024
N_HEADS = 4
HEAD_DIM = 128
D_HGRN = N_HEADS * HEAD_DIM
N_GROUPS = 4
GROUP_DIM = 128
D_TMLP = N_GROUPS * GROUP_DIM
D_IN = 4 * D_HGRN + 2 * D_TMLP
SUB = 128
N_EXPERT_GROUPS = 4
EXPERTS_PER_GROUP = 8
N_EXPERTS = N_EXPERT_GROUPS * EXPERTS_PER_GROUP
D_EXPERT = 256
ROUTER_LANES = 128
RMS_EPS = 1e-6
LN_EPS = 1e-5
F_FLOOR = 1e-30
HGRN_SAFE_EXP = 60.0
SQRT_HALF = 0.7071067811865476

MIXER_ROWS = 512
PROJ_PIECE = 256
ROUTE_ROWS = 1024
SCATTER_ROWS = 1024
COMBINE_ROWS = 512
EXPERT_TILE = 512
PACKED = D_MODEL // 2
DMA_UNROLL = 8
VMEM_LIMIT_BYTES = 56 * 1024 * 1024


def _dot(a, b):
    return jnp.dot(a, b, preferred_element_type=F32)


def _dot_nt(a, b):
    return lax.dot_general(a, b, (((1,), (1,)), ((), ())), preferred_element_type=F32)


def _dot_tn(a, b):
    return lax.dot_general(a, b, (((0,), (0,)), ((), ())), preferred_element_type=F32)


def _split_bf16(a):
    hi = a.astype(BF16)
    lo = (a - hi.astype(F32)).astype(BF16)
    return hi, lo


def _gelu(a):
    return 0.5 * a * (1.0 + lax.erf(a * SQRT_HALF))


def _silu(a):
    return a * jax.nn.sigmoid(a)


def _boundary_rows(b_ref, r0, m, width):
    pieces = []
    if 2 * m >= 8:
        for s0 in range(0, SUB, 2 * m):
            row = b_ref[pl.ds(r0 + (s0 + m - 1), 1), :]
            pieces.append(jnp.broadcast_to(row, (2 * m, width)))
    else:
        row8 = lax.broadcasted_iota(jnp.int32, (8, width), 0)
        for g0 in range(0, SUB, 8):
            acc = None
            for s0 in range(0, 8, 2 * m):
                row = jnp.broadcast_to(b_ref[pl.ds(r0 + (g0 + s0 + m - 1), 1), :], (8, width))
                acc = row if acc is None else jnp.where(row8 >= s0, row, acc)
            pieces.append(acc)
    return jnp.concatenate(pieces, axis=0)


def _half_middle_rows(b_ref, r0, width):
    half = SUB // 2
    return jnp.concatenate(
        [jnp.broadcast_to(b_ref[pl.ds(r0 + (s0 + half // 2 - 1), 1), :], (half, width))
         for s0 in range(0, SUB, half)], axis=0)


def _hgrn_level(att, q, kk, b, b_ref, r0, m, tx, row):
    ref_pt = _boundary_rows(b_ref, r0, m, D_HGRN)
    decay = jnp.exp(-jnp.abs(b - ref_pt))
    right = jnp.bitwise_and(row, m) != 0
    qt = jnp.where(right, q * decay, 0.0).astype(BF16)
    kt = jnp.where(right, 0.0, kk * decay).astype(BF16)
    same_block = tx < 2 * m
    out = []
    for h in range(N_HEADS):
        sl = slice(h * HEAD_DIM, (h + 1) * HEAD_DIM)
        term = jnp.where(same_block, _dot_nt(qt[:, sl], kt[:, sl]), 0.0)
        out.append(term if att is None else att[h] + term)
    return out


def _hgrn_attention(q, kk, b, b_ref, r0, shared_reference):
    t_idx = lax.broadcasted_iota(jnp.int32, (SUB, SUB), 0)
    s_idx = lax.broadcasted_iota(jnp.int32, (SUB, SUB), 1)
    tx = jnp.bitwise_xor(t_idx, s_idx)
    row = lax.broadcasted_iota(jnp.int32, (SUB, D_HGRN), 0)
    half = SUB // 2
    att = _hgrn_level(None, q, kk, b, b_ref, r0, half, tx, row)
    if shared_reference:
        expo = b - _half_middle_rows(b_ref, r0, D_HGRN)
        qt = (q * jnp.exp(expo)).astype(BF16)
        kt = (kk * jnp.exp(-expo)).astype(BF16)
        keep = (tx < half) & (s_idx <= t_idx)
        for h in range(N_HEADS):
            sl = slice(h * HEAD_DIM, (h + 1) * HEAD_DIM)
            att[h] = att[h] + jnp.where(keep, _dot_nt(qt[:, sl], kt[:, sl]), 0.0)
        return att
    qb = q.astype(BF16)
    kb = kk.astype(BF16)
    for h in range(N_HEADS):
        sl = slice(h * HEAD_DIM, (h + 1) * HEAD_DIM)
        att[h] = att[h] + jnp.where(tx == 0, _dot_nt(qb[:, sl], kb[:, sl]), 0.0)
    m = 1
    while m < half:
        att = _hgrn_level(att, q, kk, b, b_ref, r0, m, tx, row)
        m *= 2
    return att


def _hgrn_sub_chunk(z_ref, kk_ref, b_ref, hn_ref, y_ref, r0, state, shared_reference, per_head=None):
    rows = pl.ds(r0, SUB)
    q = _silu(z_ref[rows, 0:D_HGRN])
    kk = kk_ref[rows, :]
    b = b_ref[rows, :]
    v = z_ref[rows, 2 * D_HGRN:3 * D_HGRN].astype(BF16)
    att = _hgrn_attention(q, kk, b, b_ref, r0, shared_reference)
    b_end = b_ref[pl.ds(r0 + (SUB - 1), 1), :]
    q0 = (q * jnp.exp(b)).astype(BF16)
    k_end = (kk * jnp.exp(b_end - b)).astype(BF16)
    s_decay = jnp.exp(b_end)
    g = _silu(z_ref[rows, 3 * D_HGRN:4 * D_HGRN])
    new_state = []
    for hd in range(N_HEADS):
        if per_head is not None:
            per_head(hd)
        sl = slice(hd * HEAD_DIM, (hd + 1) * HEAD_DIM)
        st = state[hd]
        o = _dot(att[hd].astype(BF16), v[:, sl]) + _dot_nt(q0[:, sl], st.astype(BF16))
        new_state.append(st * s_decay[:, sl] + _dot_tn(v[:, sl], k_end[:, sl]))
        oms = jnp.mean(o * o, axis=-1, keepdims=True)
        on = (o * lax.rsqrt(oms + RMS_EPS)) * hn_ref[...]
        y_ref[rows, sl] = (on * g[:, sl]).astype(BF16)
    return new_state


def _mixer_tile(z_ref, zn_ref, h_next, win_ref, lb_ref, hn_ref, lng_ref, lnb_ref, wsp_ref, bsp_ref,
                y_ref, b_ref, kk_ref, st_ref):
    def project(first, count):
        for p in range(first, first + count):
            cols = slice(p * PROJ_PIECE, (p + 1) * PROJ_PIECE)
            zn_ref[:, cols] = _dot(h_next, win_ref[:, cols])

    n_pieces = D_IN // PROJ_PIECE
    early = n_pieces // 3

    t_idx = lax.broadcasted_iota(jnp.int32, (SUB, SUB), 0)
    s_idx = lax.broadcasted_iota(jnp.int32, (SUB, SUB), 1)
    tri = (s_idx <= t_idx).astype(BF16)
    n_sub = MIXER_ROWS // SUB

    worst = jnp.zeros((SUB, D_HGRN), F32)
    for c in range(n_sub):
        rows = pl.ds(c * SUB, SUB)
        lb = lb_ref[...]
        fg = lb + (1.0 - lb) * jax.nn.sigmoid(z_ref[rows, D_HGRN:2 * D_HGRN])
        lf_hi, lf_lo = _split_bf16(jnp.log(jnp.maximum(fg, F_FLOOR)))
        kk_ref[rows, :] = 1.0 - fg
        b = _dot(tri, lf_hi) + _dot(tri, lf_lo)
        b_ref[rows, :] = b
        worst = jnp.maximum(worst, jnp.abs(b - _half_middle_rows(b_ref, c * SUB, D_HGRN)))
    shared_ok = jnp.max(worst) <= HGRN_SAFE_EXP

    for c in range(n_sub):
        project(c * early // n_sub, (c + 1) * early // n_sub - c * early // n_sub)
        rows = pl.ds(c * SUB, SUB)
        u = _gelu(z_ref[rows, 4 * D_HGRN:4 * D_HGRN + D_TMLP])
        vv = _gelu(z_ref[rows, 4 * D_HGRN + D_TMLP:D_IN])
        for gi in range(N_GROUPS):
            sl = slice(gi * GROUP_DIM, (gi + 1) * GROUP_DIM)
            vg = vv[:, sl]
            mu = jnp.mean(vg, axis=-1, keepdims=True)
            cen = vg - mu
            var = jnp.mean(cen * cen, axis=-1, keepdims=True)
            vn = (cen * lax.rsqrt(var + LN_EPS)) * lng_ref[:, sl] + lnb_ref[:, sl]
            mixed = _dot(wsp_ref[gi], vn.astype(BF16)) + bsp_ref[:, sl]
            y_ref[rows, D_HGRN + gi * GROUP_DIM:D_HGRN + (gi + 1) * GROUP_DIM] = (
                u[:, sl] * mixed).astype(BF16)

    late = n_pieces - early

    @pl.when(shared_ok)
    def _():
        state = [st_ref[hd] for hd in range(N_HEADS)]
        stride = (n_sub * N_HEADS) // late

        def piece_for_slot(slot):
            if slot % stride == 0:
                project(early + slot // stride, 1)

        for c in range(n_sub):
            state = _hgrn_sub_chunk(z_ref, kk_ref, b_ref, hn_ref, y_ref, c * SUB, state, True,
                                    lambda hd, c=c: piece_for_slot(c * N_HEADS + hd))
        for hd in range(N_HEADS):
            st_ref[hd] = state[hd]

    @pl.when(jnp.logical_not(shared_ok))
    def _():
        project(early, late)

        def sub_chunk(c, carry):
            r0 = pl.multiple_of(c * SUB, SUB)
            state = [st_ref[hd] for hd in range(N_HEADS)]
            state = _hgrn_sub_chunk(z_ref, kk_ref, b_ref, hn_ref, y_ref, r0, state, False)
            for hd in range(N_HEADS):
                st_ref[hd] = state[hd]
            return carry

        lax.fori_loop(0, n_sub, sub_chunk, 0)


def _mixer_kernel(x_ref, xn_ref, gain_ref, win_ref, lb_ref, hn_ref, lng_ref, lnb_ref, wsp_ref, bsp_ref,
                  wout_ref, o_ref, za_ref, zb_ref, y_ref, b_ref, kk_ref, st_ref):
    def normed(xv):
        ms = jnp.mean(xv * xv, axis=-1, keepdims=True)
        return ((xv * lax.rsqrt(ms + RMS_EPS)) * gain_ref[...]).astype(BF16)

    @pl.when(pl.program_id(1) == 0)
    def _():
        st_ref[...] = jnp.zeros_like(st_ref)

    @pl.when((pl.program_id(0) == 0) & (pl.program_id(1) == 0))
    def _():
        za_ref[...] = _dot(normed(x_ref[0, 0:MIXER_ROWS, :]), win_ref[...])

    tile_refs = (win_ref, lb_ref, hn_ref, lng_ref, lnb_ref, wsp_ref, bsp_ref)
    _mixer_tile(za_ref, zb_ref, normed(x_ref[0, MIXER_ROWS:2 * MIXER_ROWS, :]), *tile_refs,
                y_ref, b_ref, kk_ref, st_ref)
    o_ref[0, 0:MIXER_ROWS, :] = x_ref[0, 0:MIXER_ROWS, :] + _dot(y_ref[...], wout_ref[...])
    _mixer_tile(zb_ref, za_ref, normed(xn_ref[0]), *tile_refs, y_ref, b_ref, kk_ref, st_ref)
    o_ref[0, MIXER_ROWS:2 * MIXER_ROWS, :] = (
        x_ref[0, MIXER_ROWS:2 * MIXER_ROWS, :] + _dot(y_ref[...], wout_ref[...]))


def _mixer(x, gain, w_in, lb, hgrn_norm, ln_g, ln_b, w_sp, b_sp, w_out):
    bsz, seq, _ = x.shape
    step_rows = 2 * MIXER_ROWS
    assert seq % step_rows == 0 and MIXER_ROWS % SUB == 0
    steps = seq // step_rows
    const2 = lambda b, j: (0, 0)
    const3 = lambda b, j: (0, 0, 0)
    once = pl.Buffered(1)

    def next_tile_a(b, j):
        nxt = jnp.minimum(b * steps + j + 1, bsz * steps - 1)
        return (nxt // steps, 2 * (nxt % steps), 0)

    return pl.pallas_call(
        _mixer_kernel,
        grid=(bsz, steps),
        in_specs=[
            pl.BlockSpec((1, step_rows, D_MODEL), lambda b, j: (b, j, 0)),
            pl.BlockSpec((1, MIXER_ROWS, D_MODEL), next_tile_a),
            pl.BlockSpec((1, D_MODEL), const2),
            pl.BlockSpec((D_MODEL, D_IN), const2, pipeline_mode=once),
            pl.BlockSpec((1, D_HGRN), const2),
            pl.BlockSpec((1, HEAD_DIM), const2),
            pl.BlockSpec((1, D_TMLP), const2),
            pl.BlockSpec((1, D_TMLP), const2),
            pl.BlockSpec((N_GROUPS, SUB, SUB), const3),
            pl.BlockSpec((SUB, D_TMLP), const2),
            pl.BlockSpec((D_HGRN + D_TMLP, D_MODEL), const2, pipeline_mode=once),
        ],
        out_specs=pl.BlockSpec((1, step_rows, D_MODEL), lambda b, j: (b, j, 0)),
        out_shape=jax.ShapeDtypeStruct(x.shape, F32),
        scratch_shapes=[
            pltpu.VMEM((MIXER_ROWS, D_IN), F32),
            pltpu.VMEM((MIXER_ROWS, D_IN), F32),
            pltpu.VMEM((MIXER_ROWS, D_HGRN + D_TMLP), BF16),
            pltpu.VMEM((MIXER_ROWS, D_HGRN), F32),
            pltpu.VMEM((MIXER_ROWS, D_HGRN), F32),
            pltpu.VMEM((N_HEADS, HEAD_DIM, HEAD_DIM), F32),
        ],
        compiler_params=pltpu.CompilerParams(
            dimension_semantics=("arbitrary", "arbitrary"),
            vmem_limit_bytes=VMEM_LIMIT_BYTES),
        name="mixer",
    )(x, x, gain, w_in, lb, hgrn_norm, ln_g, ln_b, w_sp, b_sp, w_out)


def _pack_bf16_pairs(a):
    lo = lax.bitcast_convert_type(a[:, :PACKED].astype(BF16).astype(F32), jnp.uint32)
    hi = lax.bitcast_convert_type(a[:, PACKED:].astype(BF16).astype(F32), jnp.uint32)
    return jnp.bitwise_or(lax.shift_right_logical(lo, jnp.uint32(16)),
                          jnp.bitwise_and(hi, jnp.uint32(0xFFFF0000)))


def _unpack_bf16_pairs(u):
    lo = lax.bitcast_convert_type(lax.shift_left(u, jnp.uint32(16)), F32)
    hi = lax.bitcast_convert_type(jnp.bitwise_and(u, jnp.uint32(0xFFFF0000)), F32)
    return lo, hi


def _router_kernel(x_ref, gain_ref, wr_hi_ref, wr_lo_ref, later_ref, hpk_ref, meta_ref, wts_ref,
                   cnt_out_ref, cnt_ref):
    @pl.when(pl.program_id(0) == 0)
    def _():
        cnt_ref[...] = jnp.zeros_like(cnt_ref)

    x = x_ref[...]
    ms = jnp.mean(x * x, axis=-1, keepdims=True)
    h = (x * lax.rsqrt(ms + RMS_EPS)) * gain_ref[...]
    hpk_ref[...] = _pack_bf16_pairs(h)
    h_hi, h_lo = _split_bf16(h)
    logits = _dot(h_hi, wr_hi_ref[...]) + (_dot(h_hi, wr_lo_ref[...]) + _dot(h_lo, wr_hi_ref[...]))
    lt = logits.T
    sub = lax.broadcasted_iota(jnp.int32, (EXPERTS_PER_GROUP, ROUTE_ROWS), 0)
    neg = jnp.float32(-jnp.inf)
    big = jnp.int32(1 << 20)
    gl = jnp.where(sub < N_EXPERT_GROUPS, lt[N_EXPERTS:N_EXPERTS + EXPERTS_PER_GROUP], neg)
    gmax = jnp.max(gl, axis=0, keepdims=True)
    p_sel = 1.0 / jnp.sum(jnp.exp(gl - gmax), axis=0, keepdims=True)
    g_idx = jnp.min(jnp.where(gl == gmax, sub, big), axis=0, keepdims=True)
    el = lt[(N_EXPERT_GROUPS - 1) * EXPERTS_PER_GROUP:N_EXPERTS]
    for g in range(N_EXPERT_GROUPS - 2, -1, -1):
        el = jnp.where(g_idx == g, lt[g * EXPERTS_PER_GROUP:(g + 1) * EXPERTS_PER_GROUP], el)
    v1 = jnp.max(el, axis=0, keepdims=True)
    i1 = jnp.min(jnp.where(el == v1, sub, big), axis=0, keepdims=True)
    el2 = jnp.where(sub == i1, neg, el)
    v2 = jnp.max(el2, axis=0, keepdims=True)
    i2 = jnp.min(jnp.where(el2 == v2, sub, big), axis=0, keepdims=True)
    e2x = jnp.exp(v2 - v1)
    w1 = p_sel / (1.0 + e2x)
    w2 = p_sel * e2x / (1.0 + e2x)
    e1 = g_idx * EXPERTS_PER_GROUP + i1
    e2 = g_idx * EXPERTS_PER_GROUP + i2

    expert = lax.broadcasted_iota(jnp.int32, (N_EXPERTS, ROUTE_ROWS), 0)
    assigned = ((expert == e1) | (expert == e2)).astype(F32)
    earlier = cnt_ref[:, 0:1] + _dot(assigned.astype(BF16), later_ref[...])
    rank1 = jnp.sum(jnp.where(expert == e1, earlier, 0.0), axis=0, keepdims=True).astype(jnp.int32)
    rank2 = jnp.sum(jnp.where(expert == e2, earlier, 0.0), axis=0, keepdims=True).astype(jnp.int32)
    meta_ref[...] = jnp.where(sub == 0, e1, jnp.where(sub == 1, e2, jnp.where(
        sub == 2, rank1, jnp.where(sub == 3, rank2, 0))))
    cnt_ref[...] = cnt_ref[...] + jnp.sum(assigned, axis=1, keepdims=True)
    cnt_out_ref[...] = cnt_ref[...]

    w_rows = jnp.where(sub == 0, w1, jnp.where(sub == 1, w2, 0.0))
    pick = (lax.broadcasted_iota(jnp.int32, (EXPERTS_PER_GROUP, ROUTER_LANES), 0)
            == lax.broadcasted_iota(jnp.int32, (EXPERTS_PER_GROUP, ROUTER_LANES), 1))
    pick = pick.astype(F32).astype(BF16)
    p_hi = w_rows.astype(BF16)
    res = w_rows - p_hi.astype(F32)
    p_mid = res.astype(BF16)
    p_lo = (res - p_mid.astype(F32)).astype(BF16)
    wts_ref[...] = _dot_tn(p_hi, pick) + (_dot_tn(p_mid, pick) + _dot_tn(p_lo, pick))


def _router(x2d, gain, wr_hi, wr_lo, later):
    n = x2d.shape[0]
    assert n % ROUTE_ROWS == 0
    const2 = lambda i: (0, 0)
    rows = lambda i: (i, 0)
    return pl.pallas_call(
        _router_kernel,
        grid=(n // ROUTE_ROWS,),
        in_specs=[
            pl.BlockSpec((ROUTE_ROWS, D_MODEL), rows),
            pl.BlockSpec((1, D_MODEL), const2),
            pl.BlockSpec((D_MODEL, ROUTER_LANES), const2),
            pl.BlockSpec((D_MODEL, ROUTER_LANES), const2),
            pl.BlockSpec((ROUTE_ROWS, ROUTE_ROWS), const2),
        ],
        out_specs=[
            pl.BlockSpec((ROUTE_ROWS, PACKED), rows),
            pl.BlockSpec((EXPERTS_PER_GROUP, ROUTE_ROWS), lambda i: (0, i)),
            pl.BlockSpec((ROUTE_ROWS, ROUTER_LANES), rows),
            pl.BlockSpec((N_EXPERTS, ROUTER_LANES), const2),
        ],
        out_shape=[
            jax.ShapeDtypeStruct((n, PACKED), jnp.uint32),
            jax.ShapeDtypeStruct((EXPERTS_PER_GROUP, n), jnp.int32),
            jax.ShapeDtypeStruct((n, ROUTER_LANES), F32),
            jax.ShapeDtypeStruct((N_EXPERTS, ROUTER_LANES), F32),
        ],
        scratch_shapes=[pltpu.VMEM((N_EXPERTS, ROUTER_LANES), F32)],
        compiler_params=pltpu.CompilerParams(
            dimension_semantics=("arbitrary",), vmem_limit_bytes=VMEM_LIMIT_BYTES),
        name="router",
    )(x2d, gain, wr_hi, wr_lo, later)


def _positions_kernel(meta_ref, row_start_ref, pos_ref):
    meta = meta_ref[...]
    expert = lax.broadcasted_iota(jnp.int32, (N_EXPERTS, ROUTE_ROWS), 0)
    starts = row_start_ref[:, 0:1]
    base1 = jnp.sum(jnp.where(expert == meta[0:1], starts, 0), axis=0, keepdims=True)
    base2 = jnp.sum(jnp.where(expert == meta[1:2], starts, 0), axis=0, keepdims=True)
    sub = lax.broadcasted_iota(jnp.int32, meta.shape, 0)
    pos_ref[...] = jnp.where(sub == 0, base1 + meta[2:3], jnp.where(sub == 1, base2 + meta[3:4], 0))


def _positions(meta, row_start_lanes):
    n = meta.shape[1]
    return pl.pallas_call(
        _positions_kernel,
        grid=(n // ROUTE_ROWS,),
        in_specs=[pl.BlockSpec((EXPERTS_PER_GROUP, ROUTE_ROWS), lambda i: (0, i)),
                  pl.BlockSpec((N_EXPERTS, ROUTER_LANES), lambda i: (0, 0))],
        out_specs=pl.BlockSpec((EXPERTS_PER_GROUP, ROUTE_ROWS), lambda i: (0, i)),
        out_shape=jax.ShapeDtypeStruct(meta.shape, jnp.int32),
        compiler_params=pltpu.CompilerParams(
            dimension_semantics=("arbitrary",), vmem_limit_bytes=VMEM_LIMIT_BYTES),
        name="positions",
    )(meta, row_start_lanes)


def _row_copy(src_ref, src_row, dst_ref, dst_row, sem):
    return pltpu.make_async_copy(src_ref.at[pl.ds(src_row, 1)], dst_ref.at[pl.ds(dst_row, 1)], sem)


def _scatter_kernel(pos1_ref, pos2_ref, hpk_ref, sorted_ref, sem):
    def issue(i, carry):
        _row_copy(hpk_ref, i, sorted_ref, pos1_ref[i], sem).start(priority=0)
        _row_copy(hpk_ref, i, sorted_ref, pos2_ref[i], sem).start(priority=1)
        return carry

    lax.fori_loop(0, SCATTER_ROWS, issue, 0, unroll=DMA_UNROLL)

    def drain(i, carry):
        _row_copy(hpk_ref, 0, sorted_ref, 0, sem).wait()
        _row_copy(hpk_ref, 0, sorted_ref, 0, sem).wait()
        return carry

    lax.fori_loop(0, SCATTER_ROWS, drain, 0, unroll=DMA_UNROLL)


def _scatter(pos1, pos2, hpk, n_rows):
    n = hpk.shape[0]
    assert n % SCATTER_ROWS == 0
    idx = lambda i: (i,)
    return pl.pallas_call(
        _scatter_kernel,
        grid=(n // SCATTER_ROWS,),
        in_specs=[
            pl.BlockSpec((SCATTER_ROWS,), idx, memory_space=pltpu.SMEM),
            pl.BlockSpec((SCATTER_ROWS,), idx, memory_space=pltpu.SMEM),
            pl.BlockSpec((SCATTER_ROWS, PACKED), lambda i: (i, 0)),
        ],
        out_specs=pl.BlockSpec(memory_space=pl.ANY),
        out_shape=jax.ShapeDtypeStruct((n_rows, PACKED), jnp.uint32),
        scratch_shapes=[pltpu.SemaphoreType.DMA],
        compiler_params=pltpu.CompilerParams(
            dimension_semantics=("arbitrary",), vmem_limit_bytes=VMEM_LIMIT_BYTES),
        name="scatter_rows",
    )(pos1, pos2, hpk)


def _expert_kernel(tile_expert_ref, tile_rows_ref, n_used_ref, lhs_ref, wg_ref, wu_ref, wd_ref,
                   o_ref):
    del tile_expert_ref, n_used_ref
    n_valid = tile_rows_ref[pl.program_id(0)]

    @pl.when(n_valid > 0)
    def _():
        valid = lax.broadcasted_iota(jnp.int32, (EXPERT_TILE, 1), 0) < n_valid
        lo, hi = _unpack_bf16_pairs(lhs_ref[...])
        hh = jnp.concatenate([jnp.where(valid, lo, 0.0).astype(BF16),
                              jnp.where(valid, hi, 0.0).astype(BF16)], axis=1)
        w_gu = jnp.concatenate([wg_ref[0, 0].astype(BF16), wu_ref[0, 0].astype(BF16)], axis=1)
        gu = _dot(hh, w_gu)
        hid = _silu(gu[:, :D_EXPERT]) * gu[:, D_EXPERT:]
        o_ref[...] = _pack_bf16_pairs(_dot(hid.astype(BF16), wd_ref[0, 0].astype(BF16)))

    @pl.when(n_valid <= 0)
    def _():
        o_ref[...] = jnp.zeros_like(o_ref)


def _experts(tile_expert, tile_rows, n_used, sorted_rows, w_gate, w_up, w_down, layer):
    n_tiles = sorted_rows.shape[0] // EXPERT_TILE
    live = lambda i, nu: jnp.minimum(i, nu[0] - 1)
    expert = lambda i, te, tr, nu: (layer, te[live(i, nu)], 0, 0)
    return pl.pallas_call(
        _expert_kernel,
        grid_spec=pltpu.PrefetchScalarGridSpec(
            num_scalar_prefetch=3,
            grid=(n_tiles,),
            in_specs=[
                pl.BlockSpec((EXPERT_TILE, PACKED), lambda i, te, tr, nu: (live(i, nu), 0)),
                pl.BlockSpec((1, 1, D_MODEL, D_EXPERT), expert),
                pl.BlockSpec((1, 1, D_MODEL, D_EXPERT), expert),
                pl.BlockSpec((1, 1, D_EXPERT, D_MODEL), expert),
            ],
            out_specs=pl.BlockSpec((EXPERT_TILE, PACKED), lambda i, te, tr, nu: (i, 0)),
        ),
        out_shape=jax.ShapeDtypeStruct(sorted_rows.shape, jnp.uint32),
        compiler_params=pltpu.CompilerParams(
            dimension_semantics=("arbitrary",), vmem_limit_bytes=VMEM_LIMIT_BYTES),
        name="experts",
    )(tile_expert, tile_rows, n_used, sorted_rows, w_gate, w_up, w_down)


def _combine_kernel(pos1_ref, pos2_ref, x_ref, wts_ref, ys_ref, gfin_ref, o_ref, buf1_ref, buf2_ref,
                    sem, *, final_norm):
    def issue(i, carry):
        _row_copy(ys_ref, pos1_ref[i], buf1_ref, i, sem).start(priority=0)
        _row_copy(ys_ref, pos2_ref[i], buf2_ref, i, sem).start(priority=1)
        return carry

    lax.fori_loop(0, COMBINE_ROWS, issue, 0, unroll=DMA_UNROLL)

    def drain(i, carry):
        _row_copy(ys_ref, 0, buf1_ref, 0, sem).wait()
        _row_copy(ys_ref, 0, buf2_ref, 0, sem).wait()
        return carry

    lax.fori_loop(0, COMBINE_ROWS, drain, 0, unroll=DMA_UNROLL)
    lo1, hi1 = _unpack_bf16_pairs(buf1_ref[...])
    lo2, hi2 = _unpack_bf16_pairs(buf2_ref[...])
    w1 = wts_ref[:, 0:1]
    w2 = wts_ref[:, 1:2]
    x = x_ref[...]
    y = jnp.concatenate([x[:, :PACKED] + (w1 * lo1 + w2 * lo2),
                         x[:, PACKED:] + (w1 * hi1 + w2 * hi2)], axis=1)
    if final_norm:
        ms = jnp.mean(y * y, axis=-1, keepdims=True)
        y = (y * lax.rsqrt(ms + RMS_EPS)) * gfin_ref[...]
    o_ref[...] = y


def _combine(pos1, pos2, x2d, wts, ys, gain_final, final_norm):
    n = x2d.shape[0]
    assert n % COMBINE_ROWS == 0
    idx = lambda i: (i,)
    rows = lambda i: (i, 0)
    return pl.pallas_call(
        functools.partial(_combine_kernel, final_norm=final_norm),
        grid=(n // COMBINE_ROWS,),
        in_specs=[
            pl.BlockSpec((COMBINE_ROWS,), idx, memory_space=pltpu.SMEM),
            pl.BlockSpec((COMBINE_ROWS,), idx, memory_space=pltpu.SMEM),
            pl.BlockSpec((COMBINE_ROWS, D_MODEL), rows),
            pl.BlockSpec((COMBINE_ROWS, ROUTER_LANES), rows),
            pl.BlockSpec(memory_space=pl.ANY),
            pl.BlockSpec((1, D_MODEL), lambda i: (0, 0)),
        ],
        out_specs=pl.BlockSpec((COMBINE_ROWS, D_MODEL), rows),
        out_shape=jax.ShapeDtypeStruct(x2d.shape, F32),
        scratch_shapes=[
            pltpu.VMEM((COMBINE_ROWS, PACKED), jnp.uint32),
            pltpu.VMEM((COMBINE_ROWS, PACKED), jnp.uint32),
            pltpu.SemaphoreType.DMA,
        ],
        compiler_params=pltpu.CompilerParams(
            dimension_semantics=("arbitrary",), vmem_limit_bytes=VMEM_LIMIT_BYTES),
        name="combine",
    )(pos1, pos2, x2d, wts, ys, gain_final)


def _moe(x2d, gain, wr_hi, wr_lo, later, w_gate, w_up, w_down, layer, gain_final, final_norm):
    n = x2d.shape[0]
    max_tiles = (2 * n) // EXPERT_TILE + N_EXPERTS
    hpk, meta, wts, cnt = _router(x2d, gain, wr_hi, wr_lo, later)
    counts = cnt[:, 0].astype(jnp.int32)
    tiles_per = (counts + (EXPERT_TILE - 1)) // EXPERT_TILE
    tile_end = jnp.cumsum(tiles_per)
    tile_start = tile_end - tiles_per
    row_start = tile_start * EXPERT_TILE
    tile_ids = jnp.arange(max_tiles, dtype=jnp.int32)
    tile_expert = jnp.minimum(
        jnp.sum((tile_ids[:, None] >= tile_end[None, :]).astype(jnp.int32), axis=1), N_EXPERTS - 1)
    n_used = tile_end[-1:].astype(jnp.int32)
    rows_left = counts[tile_expert] - (tile_ids - tile_start[tile_expert]) * EXPERT_TILE
    tile_rows = jnp.where(tile_ids < n_used[0], jnp.clip(rows_left, 0, EXPERT_TILE), 0)
    pos = _positions(meta, jnp.broadcast_to(row_start[:, None], (N_EXPERTS, ROUTER_LANES)))
    sorted_rows = _scatter(pos[0], pos[1], hpk, max_tiles * EXPERT_TILE)
    ys = _experts(tile_expert, tile_rows.astype(jnp.int32), n_used, sorted_rows, w_gate, w_up, w_down,
                  layer)
    return _combine(pos[0], pos[1], x2d, wts, ys, gain_final, final_norm)


def kernel(x, lb_logits, norm_mix, w_in, hgrn_norm, tmlp_ln_g, tmlp_ln_b, w_spatial, b_spatial,
           w_out, norm_ffn, w_router_group, w_router_expert, w_gate, w_up, w_down, norm_final):
    depth = w_in.shape[0]
    bsz, seq, _ = x.shape
    p = jax.nn.softmax(lb_logits.astype(F32), axis=0)
    lower_bounds = jnp.cumsum(p, axis=0) - p[0:1]
    tril = jnp.tril(jnp.ones((SUB, SUB), dtype=bool))
    later = jnp.triu(jnp.ones((ROUTE_ROWS, ROUTE_ROWS), BF16), 1)
    for layer in range(depth):
        w_sp = jnp.where(tril[None], w_spatial[layer], 0.0).astype(BF16)
        b_sp = jnp.repeat(b_spatial[layer].T, GROUP_DIM, axis=1)
        x = _mixer(x, norm_mix[layer][None], w_in[layer].astype(BF16), lower_bounds[layer][None],
                   hgrn_norm[layer][None], tmlp_ln_g[layer][None], tmlp_ln_b[layer][None],
                   w_sp, b_sp, w_out[layer].astype(BF16))
        w_r = jnp.concatenate([w_router_expert[layer], w_router_group[layer]], axis=1)
        w_r = jnp.pad(w_r, ((0, 0), (0, ROUTER_LANES - w_r.shape[1])))
        wr_hi, wr_lo = _split_bf16(w_r)
        x2d = _moe(x.reshape(bsz * seq, D_MODEL), norm_ffn[layer][None], wr_hi, wr_lo, later,
                   w_gate, w_up, w_down, layer, norm_final[None], final_norm=(layer == depth - 1))
        x = x2d.reshape(bsz, seq, D_MODEL)
    return x
```

```python
import functools

import jax
import jax.numpy as jnp
from jax import lax
from jax.experimental import pallas as pl
from jax.experimental.pallas import tpu as pltpu

F32 = jnp.float32
BF16 = jnp.bfloat16

D_MODEL = 1024
N_HEADS = 4
HEAD_DIM = 128
D_HGRN = N_HEADS * HEAD_DIM
N_GROUPS = 4
GROUP_DIM = 128
D_TMLP = N_GROUPS * GROUP_DIM
D_IN = 4 * D_HGRN + 2 * D_TMLP
SUB = 128
N_EXPERT_GROUPS = 4
EXPERTS_PER_GROUP = 8
N_EXPERTS = N_EXPERT_GROUPS * EXPERTS_PER_GROUP
D_EXPERT = 256
ROUTER_LANES = 128
RMS_EPS = 1e-6
LN_EPS = 1e-5
F_FLOOR = 1e-30
HGRN_SAFE_EXP = 60.0
SQRT_HALF = 0.7071067811865476

MIXER_ROWS = 512
PROJ_PIECE = 256
ROUTE_ROWS = 1024
SCATTER_ROWS = 2048
COMBINE_ROWS = 1024
EXPERT_TILE = 1024
PACKED = D_MODEL // 2
DMA_UNROLL = 8
VMEM_LIMIT_BYTES = 56 * 1024 * 1024


def _dot(a, b):
    return jnp.dot(a, b, preferred_element_type=F32)


def _dot_nt(a, b):
    return lax.dot_general(a, b, (((1,), (1,)), ((), ())), preferred_element_type=F32)


def _dot_tn(a, b):
    return lax.dot_general(a, b, (((0,), (0,)), ((), ())), preferred_element_type=F32)


def _split_bf16(a):
    hi = a.astype(BF16)
    lo = (a - hi.astype(F32)).astype(BF16)
    return hi, lo


def _gelu(a):
    return 0.5 * a * (1.0 + lax.erf(a * SQRT_HALF))


def _silu(a):
    return a * jax.nn.sigmoid(a)


def _boundary_rows(b_ref, r0, m, width):
    pieces = []
    if 2 * m >= 8:
        for s0 in range(0, SUB, 2 * m):
            row = b_ref[pl.ds(r0 + (s0 + m - 1), 1), :]
            pieces.append(jnp.broadcast_to(row, (2 * m, width)))
    else:
        row8 = lax.broadcasted_iota(jnp.int32, (8, width), 0)
        for g0 in range(0, SUB, 8):
            acc = None
            for s0 in range(0, 8, 2 * m):
                row = jnp.broadcast_to(b_ref[pl.ds(r0 + (g0 + s0 + m - 1), 1), :], (8, width))
                acc = row if acc is None else jnp.where(row8 >= s0, row, acc)
            pieces.append(acc)
    return jnp.concatenate(pieces, axis=0)


def _half_middle_rows(b_ref, r0, width):
    half = SUB // 2
    return jnp.concatenate(
        [jnp.broadcast_to(b_ref[pl.ds(r0 + (s0 + half // 2 - 1), 1), :], (half, width))
         for s0 in range(0, SUB, half)], axis=0)


def _hgrn_level(att, q, kk, b, b_ref, r0, m, tx, row):
    ref_pt = _boundary_rows(b_ref, r0, m, D_HGRN)
    decay = jnp.exp(-jnp.abs(b - ref_pt))
    right = jnp.bitwise_and(row, m) != 0
    qt = jnp.where(right, q * decay, 0.0).astype(BF16)
    kt = jnp.where(right, 0.0, kk * decay).astype(BF16)
    same_block = tx < 2 * m
    out = []
    for h in range(N_HEADS):
        sl = slice(h * HEAD_DIM, (h + 1) * HEAD_DIM)
        term = jnp.where(same_block, _dot_nt(qt[:, sl], kt[:, sl]), 0.0)
        out.append(term if att is None else att[h] + term)
    return out


def _hgrn_attention(q, kk, b, b_ref, r0, shared_reference):
    t_idx = lax.broadcasted_iota(jnp.int32, (SUB, SUB), 0)
    s_idx = lax.broadcasted_iota(jnp.int32, (SUB, SUB), 1)
    tx = jnp.bitwise_xor(t_idx, s_idx)
    row = lax.broadcasted_iota(jnp.int32, (SUB, D_HGRN), 0)
    half = SUB // 2
    att = _hgrn_level(None, q, kk, b, b_ref, r0, half, tx, row)
    if shared_reference:
        expo = b - _half_middle_rows(b_ref, r0, D_HGRN)
        qt = (q * jnp.exp(expo)).astype(BF16)
        kt = (kk * jnp.exp(-expo)).astype(BF16)
        keep = (tx < half) & (s_idx <= t_idx)
        for h in range(N_HEADS):
            sl = slice(h * HEAD_DIM, (h + 1) * HEAD_DIM)
            att[h] = att[h] + jnp.where(keep, _dot_nt(qt[:, sl], kt[:, sl]), 0.0)
        return att
    qb = q.astype(BF16)
    kb = kk.astype(BF16)
    for h in range(N_HEADS):
        sl = slice(h * HEAD_DIM, (h + 1) * HEAD_DIM)
        att[h] = att[h] + jnp.where(tx == 0, _dot_nt(qb[:, sl], kb[:, sl]), 0.0)
    m = 1
    while m < half:
        att = _hgrn_level(att, q, kk, b, b_ref, r0, m, tx, row)
        m *= 2
    return att


def _hgrn_sub_chunk(z_ref, kk_ref, b_ref, hn_ref, y_ref, r0, state, shared_reference, per_head=None):
    rows = pl.ds(r0, SUB)
    q = _silu(z_ref[rows, 0:D_HGRN])
    kk = kk_ref[rows, :]
    b = b_ref[rows, :]
    v = z_ref[rows, 2 * D_HGRN:3 * D_HGRN].astype(BF16)
    att = _hgrn_attention(q, kk, b, b_ref, r0, shared_reference)
    b_end = b_ref[pl.ds(r0 + (SUB - 1), 1), :]
    q0 = (q * jnp.exp(b)).astype(BF16)
    k_end = (kk * jnp.exp(b_end - b)).astype(BF16)
    s_decay = jnp.exp(b_end)
    g = _silu(z_ref[rows, 3 * D_HGRN:4 * D_HGRN])
    new_state = []
    for hd in range(N_HEADS):
        if per_head is not None:
            per_head(hd)
        sl = slice(hd * HEAD_DIM, (hd + 1) * HEAD_DIM)
        st = state[hd]
        o = _dot(att[hd].astype(BF16), v[:, sl]) + _dot_nt(q0[:, sl], st.astype(BF16))
        new_state.append(st * s_decay[:, sl] + _dot_tn(v[:, sl], k_end[:, sl]))
        oms = jnp.mean(o * o, axis=-1, keepdims=True)
        on = (o * lax.rsqrt(oms + RMS_EPS)) * hn_ref[...]
        y_ref[rows, sl] = (on * g[:, sl]).astype(BF16)
    return new_state


def _mixer_tile(z_ref, zn_ref, h_next, win_ref, lb_ref, hn_ref, lng_ref, lnb_ref, wsp_ref, bsp_ref,
                y_ref, b_ref, kk_ref, st_ref):
    def project(first, count):
        for p in range(first, first + count):
            cols = slice(p * PROJ_PIECE, (p + 1) * PROJ_PIECE)
            zn_ref[:, cols] = _dot(h_next, win_ref[:, cols])

    n_pieces = D_IN // PROJ_PIECE
    early = n_pieces // 3

    t_idx = lax.broadcasted_iota(jnp.int32, (SUB, SUB), 0)
    s_idx = lax.broadcasted_iota(jnp.int32, (SUB, SUB), 1)
    tri = (s_idx <= t_idx).astype(BF16)
    n_sub = MIXER_ROWS // SUB

    worst = jnp.zeros((SUB, D_HGRN), F32)
    for c in range(n_sub):
        rows = pl.ds(c * SUB, SUB)
        lb = lb_ref[...]
        fg = lb + (1.0 - lb) * jax.nn.sigmoid(z_ref[rows, D_HGRN:2 * D_HGRN])
        lf_hi, lf_lo = _split_bf16(jnp.log(jnp.maximum(fg, F_FLOOR)))
        kk_ref[rows, :] = 1.0 - fg
        b = _dot(tri, lf_hi) + _dot(tri, lf_lo)
        b_ref[rows, :] = b
        worst = jnp.maximum(worst, jnp.abs(b - _half_middle_rows(b_ref, c * SUB, D_HGRN)))
    shared_ok = jnp.max(worst) <= HGRN_SAFE_EXP

    for c in range(n_sub):
        project(c * early // n_sub, (c + 1) * early // n_sub - c * early // n_sub)
        rows = pl.ds(c * SUB, SUB)
        u = _gelu(z_ref[rows, 4 * D_HGRN:4 * D_HGRN + D_TMLP])
        vv = _gelu(z_ref[rows, 4 * D_HGRN + D_TMLP:D_IN])
        for gi in range(N_GROUPS):
            sl = slice(gi * GROUP_DIM, (gi + 1) * GROUP_DIM)
            vg = vv[:, sl]
            mu = jnp.mean(vg, axis=-1, keepdims=True)
            cen = vg - mu
            var = jnp.mean(cen * cen, axis=-1, keepdims=True)
            vn = (cen * lax.rsqrt(var + LN_EPS)) * lng_ref[:, sl] + lnb_ref[:, sl]
            mixed = _dot(wsp_ref[gi], vn.astype(BF16)) + bsp_ref[:, sl]
            y_ref[rows, D_HGRN + gi * GROUP_DIM:D_HGRN + (gi + 1) * GROUP_DIM] = (
                u[:, sl] * mixed).astype(BF16)

    late = n_pieces - early

    @pl.when(shared_ok)
    def _():
        state = [st_ref[hd] for hd in range(N_HEADS)]
        stride = (n_sub * N_HEADS) // late

        def piece_for_slot(slot):
            if slot % stride == 0:
                project(early + slot // stride, 1)

        for c in range(n_sub):
            state = _hgrn_sub_chunk(z_ref, kk_ref, b_ref, hn_ref, y_ref, c * SUB, state, True,
                                    lambda hd, c=c: piece_for_slot(c * N_HEADS + hd))
        for hd in range(N_HEADS):
            st_ref[hd] = state[hd]

    @pl.when(jnp.logical_not(shared_ok))
    def _():
        project(early, late)

        def sub_chunk(c, carry):
            r0 = pl.multiple_of(c * SUB, SUB)
            state = [st_ref[hd] for hd in range(N_HEADS)]
            state = _hgrn_sub_chunk(z_ref, kk_ref, b_ref, hn_ref, y_ref, r0, state, False)
            for hd in range(N_HEADS):
                st_ref[hd] = state[hd]
            return carry

        lax.fori_loop(0, n_sub, sub_chunk, 0)


def _mixer_kernel(x_ref, xn_ref, gain_ref, win_ref, lb_ref, hn_ref, lng_ref, lnb_ref, wsp_ref, bsp_ref,
                  wout_ref, o_ref, za_ref, zb_ref, y_ref, b_ref, kk_ref, st_ref):
    def normed(xv):
        ms = jnp.mean(xv * xv, axis=-1, keepdims=True)
        return ((xv * lax.rsqrt(ms + RMS_EPS)) * gain_ref[...]).astype(BF16)

    @pl.when(pl.program_id(1) == 0)
    def _():
        st_ref[...] = jnp.zeros_like(st_ref)

    @pl.when((pl.program_id(0) == 0) & (pl.program_id(1) == 0))
    def _():
        za_ref[...] = _dot(normed(x_ref[0, 0:MIXER_ROWS, :]), win_ref[...])

    tile_refs = (win_ref, lb_ref, hn_ref, lng_ref, lnb_ref, wsp_ref, bsp_ref)
    _mixer_tile(za_ref, zb_ref, normed(x_ref[0, MIXER_ROWS:2 * MIXER_ROWS, :]), *tile_refs,
                y_ref, b_ref, kk_ref, st_ref)
    o_ref[0, 0:MIXER_ROWS, :] = x_ref[0, 0:MIXER_ROWS, :] + _dot(y_ref[...], wout_ref[...])
    _mixer_tile(zb_ref, za_ref, normed(xn_ref[0]), *tile_refs, y_ref, b_ref, kk_ref, st_ref)
    o_ref[0, MIXER_ROWS:2 * MIXER_ROWS, :] = (
        x_ref[0, MIXER_ROWS:2 * MIXER_ROWS, :] + _dot(y_ref[...], wout_ref[...]))


def _mixer(x, gain, w_in, lb, hgrn_norm, ln_g, ln_b, w_sp, b_sp, w_out):
    bsz, seq, _ = x.shape
    step_rows = 2 * MIXER_ROWS
    assert seq % step_rows == 0 and MIXER_ROWS % SUB == 0
    steps = seq // step_rows
    const2 = lambda b, j: (0, 0)
    const3 = lambda b, j: (0, 0, 0)
    once = pl.Buffered(1)

    def next_tile_a(b, j):
        nxt = jnp.minimum(b * steps + j + 1, bsz * steps - 1)
        return (nxt // steps, 2 * (nxt % steps), 0)

    return pl.pallas_call(
        _mixer_kernel,
        grid=(bsz, steps),
        in_specs=[
            pl.BlockSpec((1, step_rows, D_MODEL), lambda b, j: (b, j, 0)),
            pl.BlockSpec((1, MIXER_ROWS, D_MODEL), next_tile_a),
            pl.BlockSpec((1, D_MODEL), const2),
            pl.BlockSpec((D_MODEL, D_IN), const2, pipeline_mode=once),
            pl.BlockSpec((1, D_HGRN), const2),
            pl.BlockSpec((1, HEAD_DIM), const2),
            pl.BlockSpec((1, D_TMLP), const2),
            pl.BlockSpec((1, D_TMLP), const2),
            pl.BlockSpec((N_GROUPS, SUB, SUB), const3),
            pl.BlockSpec((SUB, D_TMLP), const2),
            pl.BlockSpec((D_HGRN + D_TMLP, D_MODEL), const2, pipeline_mode=once),
        ],
        out_specs=pl.BlockSpec((1, step_rows, D_MODEL), lambda b, j: (b, j, 0)),
        out_shape=jax.ShapeDtypeStruct(x.shape, F32),
        scratch_shapes=[
            pltpu.VMEM((MIXER_ROWS, D_IN), F32),
            pltpu.VMEM((MIXER_ROWS, D_IN), F32),
            pltpu.VMEM((MIXER_ROWS, D_HGRN + D_TMLP), BF16),
            pltpu.VMEM((MIXER_ROWS, D_HGRN), F32),
            pltpu.VMEM((MIXER_ROWS, D_HGRN), F32),
            pltpu.VMEM((N_HEADS, HEAD_DIM, HEAD_DIM), F32),
        ],
        compiler_params=pltpu.CompilerParams(
            dimension_semantics=("arbitrary", "arbitrary"),
            vmem_limit_bytes=VMEM_LIMIT_BYTES),
        name="mixer",
    )(x, x, gain, w_in, lb, hgrn_norm, ln_g, ln_b, w_sp, b_sp, w_out)


def _pack_bf16_pairs(a):
    lo = lax.bitcast_convert_type(a[:, :PACKED].astype(BF16).astype(F32), jnp.uint32)
    hi = lax.bitcast_convert_type(a[:, PACKED:].astype(BF16).astype(F32), jnp.uint32)
    return jnp.bitwise_or(lax.shift_right_logical(lo, jnp.uint32(16)),
                          jnp.bitwise_and(hi, jnp.uint32(0xFFFF0000)))


def _unpack_bf16_pairs(u):
    lo = lax.bitcast_convert_type(lax.shift_left(u, jnp.uint32(16)), F32)
    hi = lax.bitcast_convert_type(jnp.bitwise_and(u, jnp.uint32(0xFFFF0000)), F32)
    return lo, hi


def _router_kernel(x_ref, gain_ref, wr_hi_ref, wr_lo_ref, later_ref, hpk_ref, meta_ref, wts_ref,
                   cnt_out_ref, cnt_ref):
    @pl.when(pl.program_id(0) == 0)
    def _():
        cnt_ref[...] = jnp.zeros_like(cnt_ref)

    x = x_ref[...]
    ms = jnp.mean(x * x, axis=-1, keepdims=True)
    h = (x * lax.rsqrt(ms + RMS_EPS)) * gain_ref[...]
    hpk_ref[...] = _pack_bf16_pairs(h)
    h_hi, h_lo = _split_bf16(h)
    logits = _dot(h_hi, wr_hi_ref[...]) + (_dot(h_hi, wr_lo_ref[...]) + _dot(h_lo, wr_hi_ref[...]))
    lt = logits.T
    sub = lax.broadcasted_iota(jnp.int32, (EXPERTS_PER_GROUP, ROUTE_ROWS), 0)
    neg = jnp.float32(-jnp.inf)
    big = jnp.int32(1 << 20)
    gl = jnp.where(sub < N_EXPERT_GROUPS, lt[N_EXPERTS:N_EXPERTS + EXPERTS_PER_GROUP], neg)
    gmax = jnp.max(gl, axis=0, keepdims=True)
    p_sel = 1.0 / jnp.sum(jnp.exp(gl - gmax), axis=0, keepdims=True)
    g_idx = jnp.min(jnp.where(gl == gmax, sub, big), axis=0, keepdims=True)
    el = lt[(N_EXPERT_GROUPS - 1) * EXPERTS_PER_GROUP:N_EXPERTS]
    for g in range(N_EXPERT_GROUPS - 2, -1, -1):
        el = jnp.where(g_idx == g, lt[g * EXPERTS_PER_GROUP:(g + 1) * EXPERTS_PER_GROUP], el)
    v1 = jnp.max(el, axis=0, keepdims=True)
    i1 = jnp.min(jnp.where(el == v1, sub, big), axis=0, keepdims=True)
    el2 = jnp.where(sub == i1, neg, el)
    v2 = jnp.max(el2, axis=0, keepdims=True)
    i2 = jnp.min(jnp.where(el2 == v2, sub, big), axis=0, keepdims=True)
    e2x = jnp.exp(v2 - v1)
    w1 = p_sel / (1.0 + e2x)
    w2 = p_sel * e2x / (1.0 + e2x)
    e1 = g_idx * EXPERTS_PER_GROUP + i1
    e2 = g_idx * EXPERTS_PER_GROUP + i2

    expert = lax.broadcasted_iota(jnp.int32, (N_EXPERTS, ROUTE_ROWS), 0)
    assigned = ((expert == e1) | (expert == e2)).astype(F32)
    earlier = cnt_ref[:, 0:1] + _dot(assigned.astype(BF16), later_ref[...])
    rank1 = jnp.sum(jnp.where(expert == e1, earlier, 0.0), axis=0, keepdims=True).astype(jnp.int32)
    rank2 = jnp.sum(jnp.where(expert == e2, earlier, 0.0), axis=0, keepdims=True).astype(jnp.int32)
    meta_ref[...] = jnp.where(sub == 0, e1, jnp.where(sub == 1, e2, jnp.where(
        sub == 2, rank1, jnp.where(sub == 3, rank2, 0))))
    cnt_ref[...] = cnt_ref[...] + jnp.sum(assigned, axis=1, keepdims=True)
    cnt_out_ref[...] = cnt_ref[...]

    w_rows = jnp.where(sub == 0, w1, jnp.where(sub == 1, w2, 0.0))
    pick = (lax.broadcasted_iota(jnp.int32, (EXPERTS_PER_GROUP, ROUTER_LANES), 0)
            == lax.broadcasted_iota(jnp.int32, (EXPERTS_PER_GROUP, ROUTER_LANES), 1))
    pick = pick.astype(F32).astype(BF16)
    p_hi = w_rows.astype(BF16)
    res = w_rows - p_hi.astype(F32)
    p_mid = res.astype(BF16)
    p_lo = (res - p_mid.astype(F32)).astype(BF16)
    wts_ref[...] = _dot_tn(p_hi, pick) + (_dot_tn(p_mid, pick) + _dot_tn(p_lo, pick))


def _router(x2d, gain, wr_hi, wr_lo, later):
    n = x2d.shape[0]
    assert n % ROUTE_ROWS == 0
    const2 = lambda i: (0, 0)
    rows = lambda i: (i, 0)
    return pl.pallas_call(
        _router_kernel,
        grid=(n // ROUTE_ROWS,),
        in_specs=[
            pl.BlockSpec((ROUTE_ROWS, D_MODEL), rows),
            pl.BlockSpec((1, D_MODEL), const2),
            pl.BlockSpec((D_MODEL, ROUTER_LANES), const2),
            pl.BlockSpec((D_MODEL, ROUTER_LANES), const2),
            pl.BlockSpec((ROUTE_ROWS, ROUTE_ROWS), const2),
        ],
        out_specs=[
            pl.BlockSpec((ROUTE_ROWS, PACKED), rows),
            pl.BlockSpec((EXPERTS_PER_GROUP, ROUTE_ROWS), lambda i: (0, i)),
            pl.BlockSpec((ROUTE_ROWS, ROUTER_LANES), rows),
            pl.BlockSpec((N_EXPERTS, ROUTER_LANES), const2),
        ],
        out_shape=[
            jax.ShapeDtypeStruct((n, PACKED), jnp.uint32),
            jax.ShapeDtypeStruct((EXPERTS_PER_GROUP, n), jnp.int32),
            jax.ShapeDtypeStruct((n, ROUTER_LANES), F32),
            jax.ShapeDtypeStruct((N_EXPERTS, ROUTER_LANES), F32),
        ],
        scratch_shapes=[pltpu.VMEM((N_EXPERTS, ROUTER_LANES), F32)],
        compiler_params=pltpu.CompilerParams(
            dimension_semantics=("arbitrary",), vmem_limit_bytes=VMEM_LIMIT_BYTES),
        name="router",
    )(x2d, gain, wr_hi, wr_lo, later)


def _positions_kernel(meta_ref, row_start_ref, pos_ref):
    meta = meta_ref[...]
    expert = lax.broadcasted_iota(jnp.int32, (N_EXPERTS, ROUTE_ROWS), 0)
    starts = row_start_ref[:, 0:1]
    base1 = jnp.sum(jnp.where(expert == meta[0:1], starts, 0), axis=0, keepdims=True)
    base2 = jnp.sum(jnp.where(expert == meta[1:2], starts, 0), axis=0, keepdims=True)
    sub = lax.broadcasted_iota(jnp.int32, meta.shape, 0)
    pos_ref[...] = jnp.where(sub == 0, base1 + meta[2:3], jnp.where(sub == 1, base2 + meta[3:4], 0))


def _positions(meta, row_start_lanes):
    n = meta.shape[1]
    return pl.pallas_call(
        _positions_kernel,
        grid=(n // ROUTE_ROWS,),
        in_specs=[pl.BlockSpec((EXPERTS_PER_GROUP, ROUTE_ROWS), lambda i: (0, i)),
                  pl.BlockSpec((N_EXPERTS, ROUTER_LANES), lambda i: (0, 0))],
        out_specs=pl.BlockSpec((EXPERTS_PER_GROUP, ROUTE_ROWS), lambda i: (0, i)),
        out_shape=jax.ShapeDtypeStruct(meta.shape, jnp.int32),
        compiler_params=pltpu.CompilerParams(
            dimension_semantics=("arbitrary",), vmem_limit_bytes=VMEM_LIMIT_BYTES),
        name="positions",
    )(meta, row_start_lanes)


def _row_copy(src_ref, src_row, dst_ref, dst_row, sem):
    return pltpu.make_async_copy(src_ref.at[pl.ds(src_row, 1)], dst_ref.at[pl.ds(dst_row, 1)], sem)


def _scatter_kernel(pos1_ref, pos2_ref, hpk_ref, sorted_ref, sem):
    def issue(i, carry):
        _row_copy(hpk_ref, i, sorted_ref, pos1_ref[i], sem).start(priority=0)
        _row_copy(hpk_ref, i, sorted_ref, pos2_ref[i], sem).start(priority=1)
        return carry

    lax.fori_loop(0, SCATTER_ROWS, issue, 0, unroll=DMA_UNROLL)

    def drain(i, carry):
        _row_copy(hpk_ref, 0, sorted_ref, 0, sem).wait()
        _row_copy(hpk_ref, 0, sorted_ref, 0, sem).wait()
        return carry

    lax.fori_loop(0, SCATTER_ROWS, drain, 0, unroll=DMA_UNROLL)


def _scatter(pos1, pos2, hpk, n_rows):
    n = hpk.shape[0]
    assert n % SCATTER_ROWS == 0
    idx = lambda i: (i,)
    return pl.pallas_call(
        _scatter_kernel,
        grid=(n // SCATTER_ROWS,),
        in_specs=[
            pl.BlockSpec((SCATTER_ROWS,), idx, memory_space=pltpu.SMEM),
            pl.BlockSpec((SCATTER_ROWS,), idx, memory_space=pltpu.SMEM),
            pl.BlockSpec((SCATTER_ROWS, PACKED), lambda i: (i, 0)),
        ],
        out_specs=pl.BlockSpec(memory_space=pl.ANY),
        out_shape=jax.ShapeDtypeStruct((n_rows, PACKED), jnp.uint32),
        scratch_shapes=[pltpu.SemaphoreType.DMA],
        compiler_params=pltpu.CompilerParams(
            dimension_semantics=("arbitrary",), vmem_limit_bytes=VMEM_LIMIT_BYTES),
        name="scatter_rows",
    )(pos1, pos2, hpk)


def _expert_kernel(tile_expert_ref, tile_rows_ref, n_used_ref, lhs_ref, wg_ref, wu_ref, wd_ref,
                   o_ref):
    del tile_expert_ref, n_used_ref
    n_valid = tile_rows_ref[pl.program_id(0)]

    @pl.when(n_valid > 0)
    def _():
        valid = lax.broadcasted_iota(jnp.int32, (EXPERT_TILE, 1), 0) < n_valid
        lo, hi = _unpack_bf16_pairs(lhs_ref[...])
        hh = jnp.concatenate([jnp.where(valid, lo, 0.0).astype(BF16),
                              jnp.where(valid, hi, 0.0).astype(BF16)], axis=1)
        w_gu = jnp.concatenate([wg_ref[0, 0].astype(BF16), wu_ref[0, 0].astype(BF16)], axis=1)
        gu = _dot(hh, w_gu)
        hid = _silu(gu[:, :D_EXPERT]) * gu[:, D_EXPERT:]
        o_ref[...] = _pack_bf16_pairs(_dot(hid.astype(BF16), wd_ref[0, 0].astype(BF16)))

    @pl.when(n_valid <= 0)
    def _():
        o_ref[...] = jnp.zeros_like(o_ref)


def _experts(tile_expert, tile_rows, n_used, sorted_rows, w_gate, w_up, w_down, layer):
    n_tiles = sorted_rows.shape[0] // EXPERT_TILE
    live = lambda i, nu: jnp.minimum(i, nu[0] - 1)
    expert = lambda i, te, tr, nu: (layer, te[live(i, nu)], 0, 0)
    return pl.pallas_call(
        _expert_kernel,
        grid_spec=pltpu.PrefetchScalarGridSpec(
            num_scalar_prefetch=3,
            grid=(n_tiles,),
            in_specs=[
                pl.BlockSpec((EXPERT_TILE, PACKED), lambda i, te, tr, nu: (live(i, nu), 0)),
                pl.BlockSpec((1, 1, D_MODEL, D_EXPERT), expert),
                pl.BlockSpec((1, 1, D_MODEL, D_EXPERT), expert),
                pl.BlockSpec((1, 1, D_EXPERT, D_MODEL), expert),
            ],
            out_specs=pl.BlockSpec((EXPERT_TILE, PACKED), lambda i, te, tr, nu: (i, 0)),
        ),
        out_shape=jax.ShapeDtypeStruct(sorted_rows.shape, jnp.uint32),
        compiler_params=pltpu.CompilerParams(
            dimension_semantics=("arbitrary",), vmem_limit_bytes=VMEM_LIMIT_BYTES),
        name="experts",
    )(tile_expert, tile_rows, n_used, sorted_rows, w_gate, w_up, w_down)


def _combine_kernel(pos1_ref, pos2_ref, x_ref, wts_ref, ys_ref, gfin_ref, o_ref, buf1_ref, buf2_ref,
                    sem, *, final_norm):
    def issue(i, carry):
        _row_copy(ys_ref, pos1_ref[i], buf1_ref, i, sem).start(priority=0)
        _row_copy(ys_ref, pos2_ref[i], buf2_ref, i, sem).start(priority=1)
        return carry

    lax.fori_loop(0, COMBINE_ROWS, issue, 0, unroll=DMA_UNROLL)

    def drain(i, carry):
        _row_copy(ys_ref, 0, buf1_ref, 0, sem).wait()
        _row_copy(ys_ref, 0, buf2_ref, 0, sem).wait()
        return carry

    lax.fori_loop(0, COMBINE_ROWS, drain, 0, unroll=DMA_UNROLL)
    lo1, hi1 = _unpack_bf16_pairs(buf1_ref[...])
    lo2, hi2 = _unpack_bf16_pairs(buf2_ref[...])
    w1 = wts_ref[:, 0:1]
    w2 = wts_ref[:, 1:2]
    x = x_ref[...]
    y = jnp.concatenate([x[:, :PACKED] + (w1 * lo1 + w2 * lo2),
                         x[:, PACKED:] + (w1 * hi1 + w2 * hi2)], axis=1)
    if final_norm:
        ms = jnp.mean(y * y, axis=-1, keepdims=True)
        y = (y * lax.rsqrt(ms + RMS_EPS)) * gfin_ref[...]
    o_ref[...] = y


def _combine(pos1, pos2, x2d, wts, ys, gain_final, final_norm):
    n = x2d.shape[0]
    assert n % COMBINE_ROWS == 0
    idx = lambda i: (i,)
    rows = lambda i: (i, 0)
    return pl.pallas_call(
        functools.partial(_combine_kernel, final_norm=final_norm),
        grid=(n // COMBINE_ROWS,),
        in_specs=[
            pl.BlockSpec((COMBINE_ROWS,), idx, memory_space=pltpu.SMEM),
            pl.BlockSpec((COMBINE_ROWS,), idx, memory_space=pltpu.SMEM),
            pl.BlockSpec((COMBINE_ROWS, D_MODEL), rows),
            pl.BlockSpec((COMBINE_ROWS, ROUTER_LANES), rows),
            pl.BlockSpec(memory_space=pl.ANY),
            pl.BlockSpec((1, D_MODEL), lambda i: (0, 0)),
        ],
        out_specs=pl.BlockSpec((COMBINE_ROWS, D_MODEL), rows),
        out_shape=jax.ShapeDtypeStruct(x2d.shape, F32),
        scratch_shapes=[
            pltpu.VMEM((COMBINE_ROWS, PACKED), jnp.uint32),
            pltpu.VMEM((COMBINE_ROWS, PACKED), jnp.uint32),
            pltpu.SemaphoreType.DMA,
        ],
        compiler_params=pltpu.CompilerParams(
            dimension_semantics=("arbitrary",), vmem_limit_bytes=VMEM_LIMIT_BYTES),
        name="combine",
    )(pos1, pos2, x2d, wts, ys, gain_final)


def _moe(x2d, gain, wr_hi, wr_lo, later, w_gate, w_up, w_down, layer, gain_final, final_norm):
    n = x2d.shape[0]
    max_tiles = (2 * n) // EXPERT_TILE + N_EXPERTS
    hpk, meta, wts, cnt = _router(x2d, gain, wr_hi, wr_lo, later)
    counts = cnt[:, 0].astype(jnp.int32)
    tiles_per = (counts + (EXPERT_TILE - 1)) // EXPERT_TILE
    tile_end = jnp.cumsum(tiles_per)
    tile_start = tile_end - tiles_per
    row_start = tile_start * EXPERT_TILE
    tile_ids = jnp.arange(max_tiles, dtype=jnp.int32)
    tile_expert = jnp.minimum(
        jnp.sum((tile_ids[:, None] >= tile_end[None, :]).astype(jnp.int32), axis=1), N_EXPERTS - 1)
    n_used = tile_end[-1:].astype(jnp.int32)
    rows_left = counts[tile_expert] - (tile_ids - tile_start[tile_expert]) * EXPERT_TILE
    tile_rows = jnp.where(tile_ids < n_used[0], jnp.clip(rows_left, 0, EXPERT_TILE), 0)
    pos = _positions(meta, jnp.broadcast_to(row_start[:, None], (N_EXPERTS, ROUTER_LANES)))
    sorted_rows = _scatter(pos[0], pos[1], hpk, max_tiles * EXPERT_TILE)
    ys = _experts(tile_expert, tile_rows.astype(jnp.int32), n_used, sorted_rows, w_gate, w_up, w_down,
                  layer)
    return _combine(pos[0], pos[1], x2d, wts, ys, gain_final, final_norm)


def kernel(x, lb_logits, norm_mix, w_in, hgrn_norm, tmlp_ln_g, tmlp_ln_b, w_spatial, b_spatial,
           w_out, norm_ffn, w_router_group, w_router_expert, w_gate, w_up, w_down, norm_final):
    depth = w_in.shape[0]
    bsz, seq, _ = x.shape
    p = jax.nn.softmax(lb_logits.astype(F32), axis=0)
    lower_bounds = jnp.cumsum(p, axis=0) - p[0:1]
    tril = jnp.tril(jnp.ones((SUB, SUB), dtype=bool))
    later = jnp.triu(jnp.ones((ROUTE_ROWS, ROUTE_ROWS), BF16), 1)
    for layer in range(depth):
        w_sp = jnp.where(tril[None], w_spatial[layer], 0.0).astype(BF16)
        b_sp = jnp.repeat(b_spatial[layer].T, GROUP_DIM, axis=1)
        x = _mixer(x, norm_mix[layer][None], w_in[layer].astype(BF16), lower_bounds[layer][None],
                   hgrn_norm[layer][None], tmlp_ln_g[layer][None], tmlp_ln_b[layer][None],
                   w_sp, b_sp, w_out[layer].astype(BF16))
        w_r = jnp.concatenate([w_router_expert[layer], w_router_group[layer]], axis=1)
        w_r = jnp.pad(w_r, ((0, 0), (0, ROUTER_LANES - w_r.shape[1])))
        wr_hi, wr_lo = _split_bf16(w_r)
        x2d = _moe(x.reshape(bsz * seq, D_MODEL), norm_ffn[layer][None], wr_hi, wr_lo, later,
                   w_gate, w_up, w_down, layer, norm_final[None], final_norm=(layer == depth - 1))
        x = x2d.reshape(bsz, seq, D_MODEL)
    return x
```

```python
import functools

import jax
import jax.numpy as jnp
from jax import lax
from jax.experimental import pallas as pl
from jax.experimental.pallas import tpu as pltpu

F32 = jnp.float32
BF16 = jnp.bfloat16

D_MODEL = 1024
N_HEADS = 4
HEAD_DIM = 128
D_HGRN = N_HEADS * HEAD_DIM
N_GROUPS = 4
GROUP_DIM = 128
D_TMLP = N_GROUPS * GROUP_DIM
D_IN = 4 * D_HGRN + 2 * D_TMLP
SUB = 128
N_EXPERT_GROUPS = 4
EXPERTS_PER_GROUP = 8
N_EXPERTS = N_EXPERT_GROUPS * EXPERTS_PER_GROUP
D_EXPERT = 256
ROUTER_LANES = 128
RMS_EPS = 1e-6
LN_EPS = 1e-5
F_FLOOR = 1e-30
HGRN_SAFE_EXP = 60.0
SQRT_HALF = 0.7071067811865476

MIXER_ROWS = 512
PROJ_PIECE = 256
ROUTE_ROWS = 1024
SCATTER_ROWS = 2048
COMBINE_ROWS = 1024
EXPERT_TILE = 1024
PACKED = D_MODEL // 2
DMA_UNROLL = 8
VMEM_LIMIT_BYTES = 56 * 1024 * 1024


def _dot(a, b):
    return jnp.dot(a, b, preferred_element_type=F32)


def _dot_nt(a, b):
    return lax.dot_general(a, b, (((1,), (1,)), ((), ())), preferred_element_type=F32)


def _dot_tn(a, b):
    return lax.dot_general(a, b, (((0,), (0,)), ((), ())), preferred_element_type=F32)


def _split_bf16(a):
    hi = a.astype(BF16)
    lo = (a - hi.astype(F32)).astype(BF16)
    return hi, lo


def _gelu(a):
    return 0.5 * a * (1.0 + lax.erf(a * SQRT_HALF))


def _silu(a):
    return a * jax.nn.sigmoid(a)


def _boundary_rows(b_ref, r0, m, width):
    pieces = []
    if 2 * m >= 8:
        for s0 in range(0, SUB, 2 * m):
            row = b_ref[pl.ds(r0 + (s0 + m - 1), 1), :]
            pieces.append(jnp.broadcast_to(row, (2 * m, width)))
    else:
        row8 = lax.broadcasted_iota(jnp.int32, (8, width), 0)
        for g0 in range(0, SUB, 8):
            acc = None
            for s0 in range(0, 8, 2 * m):
                row = jnp.broadcast_to(b_ref[pl.ds(r0 + (g0 + s0 + m - 1), 1), :], (8, width))
                acc = row if acc is None else jnp.where(row8 >= s0, row, acc)
            pieces.append(acc)
    return jnp.concatenate(pieces, axis=0)


def _half_middle_rows(b_ref, r0, width):
    half = SUB // 2
    return jnp.concatenate(
        [jnp.broadcast_to(b_ref[pl.ds(r0 + (s0 + half // 2 - 1), 1), :], (half, width))
         for s0 in range(0, SUB, half)], axis=0)


def _hgrn_level(att, q, kk, b, b_ref, r0, m, tx, row):
    ref_pt = _boundary_rows(b_ref, r0, m, D_HGRN)
    decay = jnp.exp(-jnp.abs(b - ref_pt))
    right = jnp.bitwise_and(row, m) != 0
    qt = jnp.where(right, q * decay, 0.0).astype(BF16)
    kt = jnp.where(right, 0.0, kk * decay).astype(BF16)
    same_block = tx < 2 * m
    out = []
    for h in range(N_HEADS):
        sl = slice(h * HEAD_DIM, (h + 1) * HEAD_DIM)
        term = jnp.where(same_block, _dot_nt(qt[:, sl], kt[:, sl]), 0.0)
        out.append(term if att is None else att[h] + term)
    return out


def _hgrn_attention(q, kk, b, b_ref, r0, shared_reference):
    t_idx = lax.broadcasted_iota(jnp.int32, (SUB, SUB), 0)
    s_idx = lax.broadcasted_iota(jnp.int32, (SUB, SUB), 1)
    tx = jnp.bitwise_xor(t_idx, s_idx)
    row = lax.broadcasted_iota(jnp.int32, (SUB, D_HGRN), 0)
    half = SUB // 2
    att = _hgrn_level(None, q, kk, b, b_ref, r0, half, tx, row)
    if shared_reference:
        expo = b - _half_middle_rows(b_ref, r0, D_HGRN)
        qt = (q * jnp.exp(expo)).astype(BF16)
        kt = (kk * jnp.exp(-expo)).astype(BF16)
        keep = (tx < half) & (s_idx <= t_idx)
        for h in range(N_HEADS):
            sl = slice(h * HEAD_DIM, (h + 1) * HEAD_DIM)
            att[h] = att[h] + jnp.where(keep, _dot_nt(qt[:, sl], kt[:, sl]), 0.0)
        return att
    qb = q.astype(BF16)
    kb = kk.astype(BF16)
    for h in range(N_HEADS):
        sl = slice(h * HEAD_DIM, (h + 1) * HEAD_DIM)
        att[h] = att[h] + jnp.where(tx == 0, _dot_nt(qb[:, sl], kb[:, sl]), 0.0)
    m = 1
    while m < half:
        att = _hgrn_level(att, q, kk, b, b_ref, r0, m, tx, row)
        m *= 2
    return att


def _hgrn_sub_chunk(z_ref, kk_ref, b_ref, hn_ref, y_ref, r0, state, shared_reference, per_head=None):
    rows = pl.ds(r0, SUB)
    q = _silu(z_ref[rows, 0:D_HGRN])
    kk = kk_ref[rows, :]
    b = b_ref[rows, :]
    v = z_ref[rows, 2 * D_HGRN:3 * D_HGRN].astype(BF16)
    att = _hgrn_attention(q, kk, b, b_ref, r0, shared_reference)
    b_end = b_ref[pl.ds(r0 + (SUB - 1), 1), :]
    q0 = (q * jnp.exp(b)).astype(BF16)
    k_end = (kk * jnp.exp(b_end - b)).astype(BF16)
    s_decay = jnp.exp(b_end)
    g = _silu(z_ref[rows, 3 * D_HGRN:4 * D_HGRN])
    new_state = []
    for hd in range(N_HEADS):
        if per_head is not None:
            per_head(hd)
        sl = slice(hd * HEAD_DIM, (hd + 1) * HEAD_DIM)
        st = state[hd]
        o = _dot(att[hd].astype(BF16), v[:, sl]) + _dot_nt(q0[:, sl], st.astype(BF16))
        new_state.append(st * s_decay[:, sl] + _dot_tn(v[:, sl], k_end[:, sl]))
        oms = jnp.mean(o * o, axis=-1, keepdims=True)
        on = (o * lax.rsqrt(oms + RMS_EPS)) * hn_ref[...]
        y_ref[rows, sl] = (on * g[:, sl]).astype(BF16)
    return new_state


def _mixer_tile(z_ref, zn_ref, h_next, win_ref, lb_ref, hn_ref, lng_ref, lnb_ref, wsp_ref, bsp_ref,
                y_ref, b_ref, kk_ref, st_ref):
    def project(first, count):
        for p in range(first, first + count):
            cols = slice(p * PROJ_PIECE, (p + 1) * PROJ_PIECE)
            zn_ref[:, cols] = _dot(h_next, win_ref[:, cols])

    n_pieces = D_IN // PROJ_PIECE
    early = n_pieces // 3

    t_idx = lax.broadcasted_iota(jnp.int32, (SUB, SUB), 0)
    s_idx = lax.broadcasted_iota(jnp.int32, (SUB, SUB), 1)
    tri = (s_idx <= t_idx).astype(BF16)
    n_sub = MIXER_ROWS // SUB

    worst = jnp.zeros((SUB, D_HGRN), F32)
    for c in range(n_sub):
        rows = pl.ds(c * SUB, SUB)
        lb = lb_ref[...]
        fg = lb + (1.0 - lb) * jax.nn.sigmoid(z_ref[rows, D_HGRN:2 * D_HGRN])
        lf_hi, lf_lo = _split_bf16(jnp.log(jnp.maximum(fg, F_FLOOR)))
        kk_ref[rows, :] = 1.0 - fg
        b = _dot(tri, lf_hi) + _dot(tri, lf_lo)
        b_ref[rows, :] = b
        worst = jnp.maximum(worst, jnp.abs(b - _half_middle_rows(b_ref, c * SUB, D_HGRN)))
    shared_ok = jnp.max(worst) <= HGRN_SAFE_EXP

    for c in range(n_sub):
        project(c * early // n_sub, (c + 1) * early // n_sub - c * early // n_sub)
        rows = pl.ds(c * SUB, SUB)
        u = _gelu(z_ref[rows, 4 * D_HGRN:4 * D_HGRN + D_TMLP])
        vv = _gelu(z_ref[rows, 4 * D_HGRN + D_TMLP:D_IN])
        for gi in range(N_GROUPS):
            sl = slice(gi * GROUP_DIM, (gi + 1) * GROUP_DIM)
            vg = vv[:, sl]
            mu = jnp.mean(vg, axis=-1, keepdims=True)
            cen = vg - mu
            var = jnp.mean(cen * cen, axis=-1, keepdims=True)
            vn = (cen * lax.rsqrt(var + LN_EPS)) * lng_ref[:, sl] + lnb_ref[:, sl]
            mixed = _dot(wsp_ref[gi], vn.astype(BF16)) + bsp_ref[:, sl]
            y_ref[rows, D_HGRN + gi * GROUP_DIM:D_HGRN + (gi + 1) * GROUP_DIM] = (
                u[:, sl] * mixed).astype(BF16)

    late = n_pieces - early

    @pl.when(shared_ok)
    def _():
        state = [st_ref[hd] for hd in range(N_HEADS)]
        stride = (n_sub * N_HEADS) // late

        def piece_for_slot(slot):
            if slot % stride == 0:
                project(early + slot // stride, 1)

        for c in range(n_sub):
            state = _hgrn_sub_chunk(z_ref, kk_ref, b_ref, hn_ref, y_ref, c * SUB, state, True,
                                    lambda hd, c=c: piece_for_slot(c * N_HEADS + hd))
        for hd in range(N_HEADS):
            st_ref[hd] = state[hd]

    @pl.when(jnp.logical_not(shared_ok))
    def _():
        project(early, late)

        def sub_chunk(c, carry):
            r0 = pl.multiple_of(c * SUB, SUB)
            state = [st_ref[hd] for hd in range(N_HEADS)]
            state = _hgrn_sub_chunk(z_ref, kk_ref, b_ref, hn_ref, y_ref, r0, state, False)
            for hd in range(N_HEADS):
                st_ref[hd] = state[hd]
            return carry

        lax.fori_loop(0, n_sub, sub_chunk, 0)


def _mixer_kernel(x_ref, xn_ref, gain_ref, win_ref, lb_ref, hn_ref, lng_ref, lnb_ref, wsp_ref, bsp_ref,
                  wout_ref, o_ref, za_ref, zb_ref, y_ref, b_ref, kk_ref, st_ref):
    def normed(xv):
        ms = jnp.mean(xv * xv, axis=-1, keepdims=True)
        return ((xv * lax.rsqrt(ms + RMS_EPS)) * gain_ref[...]).astype(BF16)

    @pl.when(pl.program_id(1) == 0)
    def _():
        st_ref[...] = jnp.zeros_like(st_ref)

    @pl.when((pl.program_id(0) == 0) & (pl.program_id(1) == 0))
    def _():
        za_ref[...] = _dot(normed(x_ref[0, 0:MIXER_ROWS, :]), win_ref[...])

    tile_refs = (win_ref, lb_ref, hn_ref, lng_ref, lnb_ref, wsp_ref, bsp_ref)
    _mixer_tile(za_ref, zb_ref, normed(x_ref[0, MIXER_ROWS:2 * MIXER_ROWS, :]), *tile_refs,
                y_ref, b_ref, kk_ref, st_ref)
    o_ref[0, 0:MIXER_ROWS, :] = x_ref[0, 0:MIXER_ROWS, :] + _dot(y_ref[...], wout_ref[...])
    _mixer_tile(zb_ref, za_ref, normed(xn_ref[0]), *tile_refs, y_ref, b_ref, kk_ref, st_ref)
    o_ref[0, MIXER_ROWS:2 * MIXER_ROWS, :] = (
        x_ref[0, MIXER_ROWS:2 * MIXER_ROWS, :] + _dot(y_ref[...], wout_ref[...]))


def _mixer(x, gain, w_in, lb, hgrn_norm, ln_g, ln_b, w_sp, b_sp, w_out):
    bsz, seq, _ = x.shape
    step_rows = 2 * MIXER_ROWS
    assert seq % step_rows == 0 and MIXER_ROWS % SUB == 0
    steps = seq // step_rows
    const2 = lambda b, j: (0, 0)
    const3 = lambda b, j: (0, 0, 0)
    once = pl.Buffered(1)

    def next_tile_a(b, j):
        nxt = jnp.minimum(b * steps + j + 1, bsz * steps - 1)
        return (nxt // steps, 2 * (nxt % steps), 0)

    return pl.pallas_call(
        _mixer_kernel,
        grid=(bsz, steps),
        in_specs=[
            pl.BlockSpec((1, step_rows, D_MODEL), lambda b, j: (b, j, 0)),
            pl.BlockSpec((1, MIXER_ROWS, D_MODEL), next_tile_a),
            pl.BlockSpec((1, D_MODEL), const2),
            pl.BlockSpec((D_MODEL, D_IN), const2, pipeline_mode=once),
            pl.BlockSpec((1, D_HGRN), const2),
            pl.BlockSpec((1, HEAD_DIM), const2),
            pl.BlockSpec((1, D_TMLP), const2),
            pl.BlockSpec((1, D_TMLP), const2),
            pl.BlockSpec((N_GROUPS, SUB, SUB), const3),
            pl.BlockSpec((SUB, D_TMLP), const2),
            pl.BlockSpec((D_HGRN + D_TMLP, D_MODEL), const2, pipeline_mode=once),
        ],
        out_specs=pl.BlockSpec((1, step_rows, D_MODEL), lambda b, j: (b, j, 0)),
        out_shape=jax.ShapeDtypeStruct(x.shape, F32),
        scratch_shapes=[
            pltpu.VMEM((MIXER_ROWS, D_IN), F32),
            pltpu.VMEM((MIXER_ROWS, D_IN), F32),
            pltpu.VMEM((MIXER_ROWS, D_HGRN + D_TMLP), BF16),
            pltpu.VMEM((MIXER_ROWS, D_HGRN), F32),
            pltpu.VMEM((MIXER_ROWS, D_HGRN), F32),
            pltpu.VMEM((N_HEADS, HEAD_DIM, HEAD_DIM), F32),
        ],
        compiler_params=pltpu.CompilerParams(
            dimension_semantics=("arbitrary", "arbitrary"),
            vmem_limit_bytes=VMEM_LIMIT_BYTES),
        name="mixer",
    )(x, x, gain, w_in, lb, hgrn_norm, ln_g, ln_b, w_sp, b_sp, w_out)


def _pack_bf16_pairs(a):
    lo = lax.bitcast_convert_type(a[:, :PACKED].astype(BF16).astype(F32), jnp.uint32)
    hi = lax.bitcast_convert_type(a[:, PACKED:].astype(BF16).astype(F32), jnp.uint32)
    return jnp.bitwise_or(lax.shift_right_logical(lo, jnp.uint32(16)),
                          jnp.bitwise_and(hi, jnp.uint32(0xFFFF0000)))


def _unpack_bf16_pairs(u):
    lo = lax.bitcast_convert_type(lax.shift_left(u, jnp.uint32(16)), F32)
    hi = lax.bitcast_convert_type(jnp.bitwise_and(u, jnp.uint32(0xFFFF0000)), F32)
    return lo, hi


def _router_kernel(x_ref, gain_ref, wr_hi_ref, wr_lo_ref, later_ref, hpk_ref, meta_ref, wts_ref,
                   cnt_out_ref, cnt_ref):
    @pl.when(pl.program_id(0) == 0)
    def _():
        cnt_ref[...] = jnp.zeros_like(cnt_ref)

    x = x_ref[...]
    ms = jnp.mean(x * x, axis=-1, keepdims=True)
    h = (x * lax.rsqrt(ms + RMS_EPS)) * gain_ref[...]
    hpk_ref[...] = _pack_bf16_pairs(h)
    h_hi, h_lo = _split_bf16(h)
    logits = _dot(h_hi, wr_hi_ref[...]) + (_dot(h_hi, wr_lo_ref[...]) + _dot(h_lo, wr_hi_ref[...]))
    lt = logits.T
    sub = lax.broadcasted_iota(jnp.int32, (EXPERTS_PER_GROUP, ROUTE_ROWS), 0)
    neg = jnp.float32(-jnp.inf)
    big = jnp.int32(1 << 20)
    gl = jnp.where(sub < N_EXPERT_GROUPS, lt[N_EXPERTS:N_EXPERTS + EXPERTS_PER_GROUP], neg)
    gmax = jnp.max(gl, axis=0, keepdims=True)
    p_sel = 1.0 / jnp.sum(jnp.exp(gl - gmax), axis=0, keepdims=True)
    g_idx = jnp.min(jnp.where(gl == gmax, sub, big), axis=0, keepdims=True)
    el = lt[(N_EXPERT_GROUPS - 1) * EXPERTS_PER_GROUP:N_EXPERTS]
    for g in range(N_EXPERT_GROUPS - 2, -1, -1):
        el = jnp.where(g_idx == g, lt[g * EXPERTS_PER_GROUP:(g + 1) * EXPERTS_PER_GROUP], el)
    v1 = jnp.max(el, axis=0, keepdims=True)
    i1 = jnp.min(jnp.where(el == v1, sub, big), axis=0, keepdims=True)
    el2 = jnp.where(sub == i1, neg, el)
    v2 = jnp.max(el2, axis=0, keepdims=True)
    i2 = jnp.min(jnp.where(el2 == v2, sub, big), axis=0, keepdims=True)
    e2x = jnp.exp(v2 - v1)
    w1 = p_sel / (1.0 + e2x)
    w2 = p_sel * e2x / (1.0 + e2x)
    e1 = g_idx * EXPERTS_PER_GROUP + i1
    e2 = g_idx * EXPERTS_PER_GROUP + i2

    expert = lax.broadcasted_iota(jnp.int32, (N_EXPERTS, ROUTE_ROWS), 0)
    assigned = ((expert == e1) | (expert == e2)).astype(F32)
    earlier = cnt_ref[:, 0:1] + _dot(assigned.astype(BF16), later_ref[...])
    rank1 = jnp.sum(jnp.where(expert == e1, earlier, 0.0), axis=0, keepdims=True).astype(jnp.int32)
    rank2 = jnp.sum(jnp.where(expert == e2, earlier, 0.0), axis=0, keepdims=True).astype(jnp.int32)
    meta_ref[...] = jnp.where(sub == 0, e1, jnp.where(sub == 1, e2, jnp.where(
        sub == 2, rank1, jnp.where(sub == 3, rank2, 0))))
    cnt_ref[...] = cnt_ref[...] + jnp.sum(assigned, axis=1, keepdims=True)
    cnt_out_ref[...] = cnt_ref[...]

    w_rows = jnp.where(sub == 0, w1, jnp.where(sub == 1, w2, 0.0))
    pick = (lax.broadcasted_iota(jnp.int32, (EXPERTS_PER_GROUP, ROUTER_LANES), 0)
            == lax.broadcasted_iota(jnp.int32, (EXPERTS_PER_GROUP, ROUTER_LANES), 1))
    pick = pick.astype(F32).astype(BF16)
    p_hi = w_rows.astype(BF16)
    res = w_rows - p_hi.astype(F32)
    p_mid = res.astype(BF16)
    p_lo = (res - p_mid.astype(F32)).astype(BF16)
    wts_ref[...] = _dot_tn(p_hi, pick) + (_dot_tn(p_mid, pick) + _dot_tn(p_lo, pick))


def _router(x2d, gain, wr_hi, wr_lo, later):
    n = x2d.shape[0]
    assert n % ROUTE_ROWS == 0
    const2 = lambda i: (0, 0)
    rows = lambda i: (i, 0)
    return pl.pallas_call(
        _router_kernel,
        grid=(n // ROUTE_ROWS,),
        in_specs=[
            pl.BlockSpec((ROUTE_ROWS, D_MODEL), rows),
            pl.BlockSpec((1, D_MODEL), const2),
            pl.BlockSpec((D_MODEL, ROUTER_LANES), const2),
            pl.BlockSpec((D_MODEL, ROUTER_LANES), const2),
            pl.BlockSpec((ROUTE_ROWS, ROUTE_ROWS), const2),
        ],
        out_specs=[
            pl.BlockSpec((ROUTE_ROWS, PACKED), rows),
            pl.BlockSpec((EXPERTS_PER_GROUP, ROUTE_ROWS), lambda i: (0, i)),
            pl.BlockSpec((ROUTE_ROWS, ROUTER_LANES), rows),
            pl.BlockSpec((N_EXPERTS, ROUTER_LANES), const2),
        ],
        out_shape=[
            jax.ShapeDtypeStruct((n, PACKED), jnp.uint32),
            jax.ShapeDtypeStruct((EXPERTS_PER_GROUP, n), jnp.int32),
            jax.ShapeDtypeStruct((n, ROUTER_LANES), F32),
            jax.ShapeDtypeStruct((N_EXPERTS, ROUTER_LANES), F32),
        ],
        scratch_shapes=[pltpu.VMEM((N_EXPERTS, ROUTER_LANES), F32)],
        compiler_params=pltpu.CompilerParams(
            dimension_semantics=("arbitrary",), vmem_limit_bytes=VMEM_LIMIT_BYTES),
        name="router",
    )(x2d, gain, wr_hi, wr_lo, later)


def _positions_kernel(meta_ref, row_start_ref, pos_ref):
    meta = meta_ref[...]
    expert = lax.broadcasted_iota(jnp.int32, (N_EXPERTS, ROUTE_ROWS), 0)
    starts = row_start_ref[:, 0:1]
    base1 = jnp.sum(jnp.where(expert == meta[0:1], starts, 0), axis=0, keepdims=True)
    base2 = jnp.sum(jnp.where(expert == meta[1:2], starts, 0), axis=0, keepdims=True)
    sub = lax.broadcasted_iota(jnp.int32, meta.shape, 0)
    pos_ref[...] = jnp.where(sub == 0, base1 + meta[2:3], jnp.where(sub == 1, base2 + meta[3:4], 0))


def _positions(meta, row_start_lanes):
    n = meta.shape[1]
    return pl.pallas_call(
        _positions_kernel,
        grid=(n // ROUTE_ROWS,),
        in_specs=[pl.BlockSpec((EXPERTS_PER_GROUP, ROUTE_ROWS), lambda i: (0, i)),
                  pl.BlockSpec((N_EXPERTS, ROUTER_LANES), lambda i: (0, 0))],
        out_specs=pl.BlockSpec((EXPERTS_PER_GROUP, ROUTE_ROWS), lambda i: (0, i)),
        out_shape=jax.ShapeDtypeStruct(meta.shape, jnp.int32),
        compiler_params=pltpu.CompilerParams(
            dimension_semantics=("arbitrary",), vmem_limit_bytes=VMEM_LIMIT_BYTES),
        name="positions",
    )(meta, row_start_lanes)


def _row_copy(src_ref, src_row, dst_ref, dst_row, sem):
    return pltpu.make_async_copy(src_ref.at[pl.ds(src_row, 1)], dst_ref.at[pl.ds(dst_row, 1)], sem)


def _scatter_kernel(pos1_ref, pos2_ref, hpk_ref, sorted_ref, sem):
    first = pl.program_id(0) * SCATTER_ROWS

    def issue(i, carry):
        _row_copy(hpk_ref, first + i, sorted_ref, pos1_ref[i], sem).start(priority=0)
        _row_copy(hpk_ref, first + i, sorted_ref, pos2_ref[i], sem).start(priority=1)
        return carry

    lax.fori_loop(0, SCATTER_ROWS, issue, 0, unroll=DMA_UNROLL)

    def drain(i, carry):
        _row_copy(hpk_ref, 0, sorted_ref, 0, sem).wait()
        _row_copy(hpk_ref, 0, sorted_ref, 0, sem).wait()
        return carry

    lax.fori_loop(0, SCATTER_ROWS, drain, 0, unroll=DMA_UNROLL)


def _scatter(pos1, pos2, hpk, n_rows):
    n = hpk.shape[0]
    assert n % SCATTER_ROWS == 0
    idx = lambda i: (i,)
    return pl.pallas_call(
        _scatter_kernel,
        grid=(n // SCATTER_ROWS,),
        in_specs=[
            pl.BlockSpec((SCATTER_ROWS,), idx, memory_space=pltpu.SMEM),
            pl.BlockSpec((SCATTER_ROWS,), idx, memory_space=pltpu.SMEM),
            pl.BlockSpec(memory_space=pl.ANY),
        ],
        out_specs=pl.BlockSpec(memory_space=pl.ANY),
        out_shape=jax.ShapeDtypeStruct((n_rows, PACKED), jnp.uint32),
        scratch_shapes=[pltpu.SemaphoreType.DMA],
        compiler_params=pltpu.CompilerParams(
            dimension_semantics=("arbitrary",), vmem_limit_bytes=VMEM_LIMIT_BYTES),
        name="scatter_rows",
    )(pos1, pos2, hpk)


def _expert_kernel(tile_expert_ref, tile_rows_ref, n_used_ref, lhs_ref, wg_ref, wu_ref, wd_ref,
                   o_ref):
    del tile_expert_ref, n_used_ref
    n_valid = tile_rows_ref[pl.program_id(0)]

    @pl.when(n_valid > 0)
    def _():
        valid = lax.broadcasted_iota(jnp.int32, (EXPERT_TILE, 1), 0) < n_valid
        lo, hi = _unpack_bf16_pairs(lhs_ref[...])
        hh = jnp.concatenate([jnp.where(valid, lo, 0.0).astype(BF16),
                              jnp.where(valid, hi, 0.0).astype(BF16)], axis=1)
        w_gu = jnp.concatenate([wg_ref[0, 0].astype(BF16), wu_ref[0, 0].astype(BF16)], axis=1)
        gu = _dot(hh, w_gu)
        hid = _silu(gu[:, :D_EXPERT]) * gu[:, D_EXPERT:]
        o_ref[...] = _pack_bf16_pairs(_dot(hid.astype(BF16), wd_ref[0, 0].astype(BF16)))

    @pl.when(n_valid <= 0)
    def _():
        o_ref[...] = jnp.zeros_like(o_ref)


def _experts(tile_expert, tile_rows, n_used, sorted_rows, w_gate, w_up, w_down, layer):
    n_tiles = sorted_rows.shape[0] // EXPERT_TILE
    live = lambda i, nu: jnp.minimum(i, nu[0] - 1)
    expert = lambda i, te, tr, nu: (layer, te[live(i, nu)], 0, 0)
    return pl.pallas_call(
        _expert_kernel,
        grid_spec=pltpu.PrefetchScalarGridSpec(
            num_scalar_prefetch=3,
            grid=(n_tiles,),
            in_specs=[
                pl.BlockSpec((EXPERT_TILE, PACKED), lambda i, te, tr, nu: (live(i, nu), 0)),
                pl.BlockSpec((1, 1, D_MODEL, D_EXPERT), expert),
                pl.BlockSpec((1, 1, D_MODEL, D_EXPERT), expert),
                pl.BlockSpec((1, 1, D_EXPERT, D_MODEL), expert),
            ],
            out_specs=pl.BlockSpec((EXPERT_TILE, PACKED), lambda i, te, tr, nu: (i, 0)),
        ),
        out_shape=jax.ShapeDtypeStruct(sorted_rows.shape, jnp.uint32),
        compiler_params=pltpu.CompilerParams(
            dimension_semantics=("arbitrary",), vmem_limit_bytes=VMEM_LIMIT_BYTES),
        name="experts",
    )(tile_expert, tile_rows, n_used, sorted_rows, w_gate, w_up, w_down)


def _combine_kernel(pos1_ref, pos2_ref, x_ref, wts_ref, ys_ref, gfin_ref, o_ref, buf1_ref, buf2_ref,
                    sem, *, final_norm):
    def issue(i, carry):
        _row_copy(ys_ref, pos1_ref[i], buf1_ref, i, sem).start(priority=0)
        _row_copy(ys_ref, pos2_ref[i], buf2_ref, i, sem).start(priority=1)
        return carry

    lax.fori_loop(0, COMBINE_ROWS, issue, 0, unroll=DMA_UNROLL)

    def drain(i, carry):
        _row_copy(ys_ref, 0, buf1_ref, 0, sem).wait()
        _row_copy(ys_ref, 0, buf2_ref, 0, sem).wait()
        return carry

    lax.fori_loop(0, COMBINE_ROWS, drain, 0, unroll=DMA_UNROLL)
    lo1, hi1 = _unpack_bf16_pairs(buf1_ref[...])
    lo2, hi2 = _unpack_bf16_pairs(buf2_ref[...])
    w1 = wts_ref[:, 0:1]
    w2 = wts_ref[:, 1:2]
    x = x_ref[...]
    y = jnp.concatenate([x[:, :PACKED] + (w1 * lo1 + w2 * lo2),
                         x[:, PACKED:] + (w1 * hi1 + w2 * hi2)], axis=1)
    if final_norm:
        ms = jnp.mean(y * y, axis=-1, keepdims=True)
        y = (y * lax.rsqrt(ms + RMS_EPS)) * gfin_ref[...]
    o_ref[...] = y


def _combine(pos1, pos2, x2d, wts, ys, gain_final, final_norm):
    n = x2d.shape[0]
    assert n % COMBINE_ROWS == 0
    idx = lambda i: (i,)
    rows = lambda i: (i, 0)
    return pl.pallas_call(
        functools.partial(_combine_kernel, final_norm=final_norm),
        grid=(n // COMBINE_ROWS,),
        in_specs=[
            pl.BlockSpec((COMBINE_ROWS,), idx, memory_space=pltpu.SMEM),
            pl.BlockSpec((COMBINE_ROWS,), idx, memory_space=pltpu.SMEM),
            pl.BlockSpec((COMBINE_ROWS, D_MODEL), rows),
            pl.BlockSpec((COMBINE_ROWS, ROUTER_LANES), rows),
            pl.BlockSpec(memory_space=pl.ANY),
            pl.BlockSpec((1, D_MODEL), lambda i: (0, 0)),
        ],
        out_specs=pl.BlockSpec((COMBINE_ROWS, D_MODEL), rows),
        out_shape=jax.ShapeDtypeStruct(x2d.shape, F32),
        scratch_shapes=[
            pltpu.VMEM((COMBINE_ROWS, PACKED), jnp.uint32),
            pltpu.VMEM((COMBINE_ROWS, PACKED), jnp.uint32),
            pltpu.SemaphoreType.DMA,
        ],
        compiler_params=pltpu.CompilerParams(
            dimension_semantics=("arbitrary",), vmem_limit_bytes=VMEM_LIMIT_BYTES),
        name="combine",
    )(pos1, pos2, x2d, wts, ys, gain_final)


def _moe(x2d, gain, wr_hi, wr_lo, later, w_gate, w_up, w_down, layer, gain_final, final_norm):
    n = x2d.shape[0]
    max_tiles = (2 * n) // EXPERT_TILE + N_EXPERTS
    hpk, meta, wts, cnt = _router(x2d, gain, wr_hi, wr_lo, later)
    counts = cnt[:, 0].astype(jnp.int32)
    tiles_per = (counts + (EXPERT_TILE - 1)) // EXPERT_TILE
    tile_end = jnp.cumsum(tiles_per)
    tile_start = tile_end - tiles_per
    row_start = tile_start * EXPERT_TILE
    tile_ids = jnp.arange(max_tiles, dtype=jnp.int32)
    tile_expert = jnp.minimum(
        jnp.sum((tile_ids[:, None] >= tile_end[None, :]).astype(jnp.int32), axis=1), N_EXPERTS - 1)
    n_used = tile_end[-1:].astype(jnp.int32)
    rows_left = counts[tile_expert] - (tile_ids - tile_start[tile_expert]) * EXPERT_TILE
    tile_rows = jnp.where(tile_ids < n_used[0], jnp.clip(rows_left, 0, EXPERT_TILE), 0)
    pos = _positions(meta, jnp.broadcast_to(row_start[:, None], (N_EXPERTS, ROUTER_LANES)))
    sorted_rows = _scatter(pos[0], pos[1], hpk, max_tiles * EXPERT_TILE)
    ys = _experts(tile_expert, tile_rows.astype(jnp.int32), n_used, sorted_rows, w_gate, w_up, w_down,
                  layer)
    return _combine(pos[0], pos[1], x2d, wts, ys, gain_final, final_norm)


def kernel(x, lb_logits, norm_mix, w_in, hgrn_norm, tmlp_ln_g, tmlp_ln_b, w_spatial, b_spatial,
           w_out, norm_ffn, w_router_group, w_router_expert, w_gate, w_up, w_down, norm_final):
    depth = w_in.shape[0]
    bsz, seq, _ = x.shape
    p = jax.nn.softmax(lb_logits.astype(F32), axis=0)
    lower_bounds = jnp.cumsum(p, axis=0) - p[0:1]
    tril = jnp.tril(jnp.ones((SUB, SUB), dtype=bool))
    later = jnp.triu(jnp.ones((ROUTE_ROWS, ROUTE_ROWS), BF16), 1)
    for layer in range(depth):
        w_sp = jnp.where(tril[None], w_spatial[layer], 0.0).astype(BF16)
        b_sp = jnp.repeat(b_spatial[layer].T, GROUP_DIM, axis=1)
        x = _mixer(x, norm_mix[layer][None], w_in[layer].astype(BF16), lower_bounds[layer][None],
                   hgrn_norm[layer][None], tmlp_ln_g[layer][None], tmlp_ln_b[layer][None],
                   w_sp, b_sp, w_out[layer].astype(BF16))
        w_r = jnp.concatenate([w_router_expert[layer], w_router_group[layer]], axis=1)
        w_r = jnp.pad(w_r, ((0, 0), (0, ROUTER_LANES - w_r.shape[1])))
        wr_hi, wr_lo = _split_bf16(w_r)
        x2d = _moe(x.reshape(bsz * seq, D_MODEL), norm_ffn[layer][None], wr_hi, wr_lo, later,
                   w_gate, w_up, w_down, layer, norm_final[None], final_norm=(layer == depth - 1))
        x = x2d.reshape(bsz, seq, D_MODEL)
    return x
```

```python
import functools

import jax
import jax.numpy as jnp
from jax import lax
from jax.experimental import pallas as pl
from jax.experimental.pallas import tpu as pltpu

F32 = jnp.float32
BF16 = jnp.bfloat16

D_MODEL = 1024
N_HEADS = 4
HEAD_DIM = 128
D_HGRN = N_HEADS * HEAD_DIM
N_GROUPS = 4
GROUP_DIM = 128
D_TMLP = N_GROUPS * GROUP_DIM
D_IN = 4 * D_HGRN + 2 * D_TMLP
SUB = 128
N_EXPERT_GROUPS = 4
EXPERTS_PER_GROUP = 8
N_EXPERTS = N_EXPERT_GROUPS * EXPERTS_PER_GROUP
D_EXPERT = 256
ROUTER_LANES = 128
RMS_EPS = 1e-6
LN_EPS = 1e-5
F_FLOOR = 1e-30
HGRN_SAFE_EXP = 60.0
SQRT_HALF = 0.7071067811865476

MIXER_ROWS = 512
PROJ_PIECE = 256
ROUTE_ROWS = 1024
SCATTER_ROWS = 2048
COMBINE_ROWS = 1024
EXPERT_TILE = 1024
PACKED = D_MODEL // 2
DMA_UNROLL = 8
VMEM_LIMIT_BYTES = 56 * 1024 * 1024


def _dot(a, b):
    return jnp.dot(a, b, preferred_element_type=F32)


def _dot_nt(a, b):
    return lax.dot_general(a, b, (((1,), (1,)), ((), ())), preferred_element_type=F32)


def _dot_tn(a, b):
    return lax.dot_general(a, b, (((0,), (0,)), ((), ())), preferred_element_type=F32)


def _split_bf16(a):
    hi = a.astype(BF16)
    lo = (a - hi.astype(F32)).astype(BF16)
    return hi, lo


def _gelu(a):
    return 0.5 * a * (1.0 + lax.erf(a * SQRT_HALF))


def _silu(a):
    return a * jax.nn.sigmoid(a)


def _boundary_rows(b_ref, r0, m, width):
    pieces = []
    if 2 * m >= 8:
        for s0 in range(0, SUB, 2 * m):
            row = b_ref[pl.ds(r0 + (s0 + m - 1), 1), :]
            pieces.append(jnp.broadcast_to(row, (2 * m, width)))
    else:
        row8 = lax.broadcasted_iota(jnp.int32, (8, width), 0)
        for g0 in range(0, SUB, 8):
            acc = None
            for s0 in range(0, 8, 2 * m):
                row = jnp.broadcast_to(b_ref[pl.ds(r0 + (g0 + s0 + m - 1), 1), :], (8, width))
                acc = row if acc is None else jnp.where(row8 >= s0, row, acc)
            pieces.append(acc)
    return jnp.concatenate(pieces, axis=0)


def _half_middle_rows(b_ref, r0, width):
    half = SUB // 2
    return jnp.concatenate(
        [jnp.broadcast_to(b_ref[pl.ds(r0 + (s0 + half // 2 - 1), 1), :], (half, width))
         for s0 in range(0, SUB, half)], axis=0)


def _hgrn_level(att, q, kk, b, b_ref, r0, m, tx, row):
    ref_pt = _boundary_rows(b_ref, r0, m, D_HGRN)
    decay = jnp.exp(-jnp.abs(b - ref_pt))
    right = jnp.bitwise_and(row, m) != 0
    qt = jnp.where(right, q * decay, 0.0).astype(BF16)
    kt = jnp.where(right, 0.0, kk * decay).astype(BF16)
    same_block = tx < 2 * m
    out = []
    for h in range(N_HEADS):
        sl = slice(h * HEAD_DIM, (h + 1) * HEAD_DIM)
        term = jnp.where(same_block, _dot_nt(qt[:, sl], kt[:, sl]), 0.0)
        out.append(term if att is None else att[h] + term)
    return out


def _hgrn_attention(q, kk, b, b_ref, r0, shared_reference):
    t_idx = lax.broadcasted_iota(jnp.int32, (SUB, SUB), 0)
    s_idx = lax.broadcasted_iota(jnp.int32, (SUB, SUB), 1)
    tx = jnp.bitwise_xor(t_idx, s_idx)
    row = lax.broadcasted_iota(jnp.int32, (SUB, D_HGRN), 0)
    half = SUB // 2
    att = _hgrn_level(None, q, kk, b, b_ref, r0, half, tx, row)
    if shared_reference:
        expo = b - _half_middle_rows(b_ref, r0, D_HGRN)
        qt = (q * jnp.exp(expo)).astype(BF16)
        kt = (kk * jnp.exp(-expo)).astype(BF16)
        keep = (tx < half) & (s_idx <= t_idx)
        for h in range(N_HEADS):
            sl = slice(h * HEAD_DIM, (h + 1) * HEAD_DIM)
            att[h] = att[h] + jnp.where(keep, _dot_nt(qt[:, sl], kt[:, sl]), 0.0)
        return att
    qb = q.astype(BF16)
    kb = kk.astype(BF16)
    for h in range(N_HEADS):
        sl = slice(h * HEAD_DIM, (h + 1) * HEAD_DIM)
        att[h] = att[h] + jnp.where(tx == 0, _dot_nt(qb[:, sl], kb[:, sl]), 0.0)
    m = 1
    while m < half:
        att = _hgrn_level(att, q, kk, b, b_ref, r0, m, tx, row)
        m *= 2
    return att


def _hgrn_sub_chunk(z_ref, kk_ref, b_ref, hn_ref, y_ref, r0, state, shared_reference, per_head=None):
    rows = pl.ds(r0, SUB)
    q = _silu(z_ref[rows, 0:D_HGRN])
    kk = kk_ref[rows, :]
    b = b_ref[rows, :]
    v = z_ref[rows, 2 * D_HGRN:3 * D_HGRN].astype(BF16)
    att = _hgrn_attention(q, kk, b, b_ref, r0, shared_reference)
    b_end = b_ref[pl.ds(r0 + (SUB - 1), 1), :]
    q0 = (q * jnp.exp(b)).astype(BF16)
    k_end = (kk * jnp.exp(b_end - b)).astype(BF16)
    s_decay = jnp.exp(b_end)
    g = _silu(z_ref[rows, 3 * D_HGRN:4 * D_HGRN])
    new_state = []
    for hd in range(N_HEADS):
        if per_head is not None:
            per_head(hd)
        sl = slice(hd * HEAD_DIM, (hd + 1) * HEAD_DIM)
        st = state[hd]
        o = _dot(att[hd].astype(BF16), v[:, sl]) + _dot_nt(q0[:, sl], st.astype(BF16))
        new_state.append(st * s_decay[:, sl] + _dot_tn(v[:, sl], k_end[:, sl]))
        oms = jnp.mean(o * o, axis=-1, keepdims=True)
        on = (o * lax.rsqrt(oms + RMS_EPS)) * hn_ref[...]
        y_ref[rows, sl] = (on * g[:, sl]).astype(BF16)
    return new_state


def _block_diag_pair(a):
    rows = a.shape[0]
    r = lax.broadcasted_iota(jnp.int32, (2 * rows, 2 * HEAD_DIM), 0)
    c = lax.broadcasted_iota(jnp.int32, (2 * rows, 2 * HEAD_DIM), 1)
    return jnp.where((r < rows) == (c < HEAD_DIM), jnp.concatenate([a, a], axis=0), jnp.zeros((), a.dtype))


def _hgrn_sub_chunk_paired(z_ref, kk_ref, b_ref, hn_ref, y_ref, r0, state, per_pair=None):
    rows = pl.ds(r0, SUB)
    q = _silu(z_ref[rows, 0:D_HGRN])
    kk = kk_ref[rows, :]
    b = b_ref[rows, :]
    v = z_ref[rows, 2 * D_HGRN:3 * D_HGRN].astype(BF16)
    t_idx = lax.broadcasted_iota(jnp.int32, (SUB, 2 * SUB), 0)
    s_idx = jnp.bitwise_and(lax.broadcasted_iota(jnp.int32, (SUB, 2 * SUB), 1), SUB - 1)
    half = SUB // 2
    row = lax.broadcasted_iota(jnp.int32, (SUB, D_HGRN), 0)
    upper = row >= half
    decay = jnp.exp(-jnp.abs(b - _boundary_rows(b_ref, r0, half, D_HGRN)))
    qt_far = jnp.where(upper, q * decay, 0.0).astype(BF16)
    kt_far = jnp.where(upper, 0.0, kk * decay).astype(BF16)
    expo = b - _half_middle_rows(b_ref, r0, D_HGRN)
    qt_near = (q * jnp.exp(expo)).astype(BF16)
    kt_near = (kk * jnp.exp(-expo)).astype(BF16)
    keep = (jnp.bitwise_xor(t_idx, s_idx) < half) & (s_idx <= t_idx)
    b_end = b_ref[pl.ds(r0 + (SUB - 1), 1), :]
    q0 = (q * jnp.exp(b)).astype(BF16)
    k_end = (kk * jnp.exp(b_end - b)).astype(BF16)
    s_decay = jnp.exp(b_end)
    g = _silu(z_ref[rows, 3 * D_HGRN:4 * D_HGRN])
    blk_r = lax.broadcasted_iota(jnp.int32, (2 * HEAD_DIM, 2 * HEAD_DIM), 0) < HEAD_DIM
    blk_c = lax.broadcasted_iota(jnp.int32, (2 * HEAD_DIM, 2 * HEAD_DIM), 1) < HEAD_DIM
    on_diag = blk_r == blk_c
    new_state = []
    for p in range(N_HEADS // 2):
        if per_pair is not None:
            per_pair(p)
        sl2 = slice(2 * p * HEAD_DIM, 2 * (p + 1) * HEAD_DIM)
        att = (_dot_nt(qt_far[:, sl2], _block_diag_pair(kt_far[:, sl2]))
               + jnp.where(keep, _dot_nt(qt_near[:, sl2], _block_diag_pair(kt_near[:, sl2])), 0.0))
        st = state[p]
        o2 = (_dot(att.astype(BF16), _block_diag_pair(v[:, sl2]))
              + _dot_nt(q0[:, sl2], st.astype(BF16)))
        new_state.append(st * s_decay[:, sl2]
                         + jnp.where(on_diag, _dot_tn(v[:, sl2], k_end[:, sl2]), 0.0))
        for i in range(2):
            hd = 2 * p + i
            sl = slice(hd * HEAD_DIM, (hd + 1) * HEAD_DIM)
            o = o2[:, i * HEAD_DIM:(i + 1) * HEAD_DIM]
            oms = jnp.mean(o * o, axis=-1, keepdims=True)
            on = (o * lax.rsqrt(oms + RMS_EPS)) * hn_ref[...]
            y_ref[rows, sl] = (on * g[:, sl]).astype(BF16)
    return new_state


def _mixer_tile(z_ref, zn_ref, h_next, win_ref, lb_ref, hn_ref, lng_ref, lnb_ref, wsp_ref, bsp_ref,
                y_ref, b_ref, kk_ref, st_ref):
    def project(first, count):
        for p in range(first, first + count):
            cols = slice(p * PROJ_PIECE, (p + 1) * PROJ_PIECE)
            zn_ref[:, cols] = _dot(h_next, win_ref[:, cols])

    n_pieces = D_IN // PROJ_PIECE
    early = n_pieces // 3

    t_idx = lax.broadcasted_iota(jnp.int32, (SUB, SUB), 0)
    s_idx = lax.broadcasted_iota(jnp.int32, (SUB, SUB), 1)
    tri = (s_idx <= t_idx).astype(BF16)
    n_sub = MIXER_ROWS // SUB

    worst = jnp.zeros((SUB, D_HGRN), F32)
    for c in range(n_sub):
        rows = pl.ds(c * SUB, SUB)
        lb = lb_ref[...]
        fg = lb + (1.0 - lb) * jax.nn.sigmoid(z_ref[rows, D_HGRN:2 * D_HGRN])
        lf_hi, lf_lo = _split_bf16(jnp.log(jnp.maximum(fg, F_FLOOR)))
        kk_ref[rows, :] = 1.0 - fg
        b = _dot(tri, lf_hi) + _dot(tri, lf_lo)
        b_ref[rows, :] = b
        worst = jnp.maximum(worst, jnp.abs(b - _half_middle_rows(b_ref, c * SUB, D_HGRN)))
    shared_ok = jnp.max(worst) <= HGRN_SAFE_EXP

    for c in range(n_sub):
        project(c * early // n_sub, (c + 1) * early // n_sub - c * early // n_sub)
        rows = pl.ds(c * SUB, SUB)
        u = _gelu(z_ref[rows, 4 * D_HGRN:4 * D_HGRN + D_TMLP])
        vv = _gelu(z_ref[rows, 4 * D_HGRN + D_TMLP:D_IN])
        for gi in range(N_GROUPS):
            sl = slice(gi * GROUP_DIM, (gi + 1) * GROUP_DIM)
            vg = vv[:, sl]
            mu = jnp.mean(vg, axis=-1, keepdims=True)
            cen = vg - mu
            var = jnp.mean(cen * cen, axis=-1, keepdims=True)
            vn = (cen * lax.rsqrt(var + LN_EPS)) * lng_ref[:, sl] + lnb_ref[:, sl]
            mixed = _dot(wsp_ref[gi], vn.astype(BF16)) + bsp_ref[:, sl]
            y_ref[rows, D_HGRN + gi * GROUP_DIM:D_HGRN + (gi + 1) * GROUP_DIM] = (
                u[:, sl] * mixed).astype(BF16)

    late = n_pieces - early

    @pl.when(shared_ok)
    def _():
        n_pairs = N_HEADS // 2
        zero = jnp.zeros((HEAD_DIM, HEAD_DIM), F32)
        state = [jnp.concatenate([jnp.concatenate([st_ref[2 * p], zero], axis=1),
                                  jnp.concatenate([zero, st_ref[2 * p + 1]], axis=1)], axis=0)
                 for p in range(n_pairs)]
        stride = max(1, (n_sub * n_pairs) // late)

        def piece_for_slot(slot):
            if slot % stride == 0 and slot // stride < late:
                project(early + slot // stride, 1)

        for c in range(n_sub):
            state = _hgrn_sub_chunk_paired(z_ref, kk_ref, b_ref, hn_ref, y_ref, c * SUB, state,
                                           lambda p, c=c: piece_for_slot(c * n_pairs + p))
        project(early + (n_sub * n_pairs + stride - 1) // stride,
                max(0, late - (n_sub * n_pairs + stride - 1) // stride))
        for p in range(n_pairs):
            st_ref[2 * p] = state[p][0:HEAD_DIM, 0:HEAD_DIM]
            st_ref[2 * p + 1] = state[p][HEAD_DIM:2 * HEAD_DIM, HEAD_DIM:2 * HEAD_DIM]

    @pl.when(jnp.logical_not(shared_ok))
    def _():
        project(early, late)

        def sub_chunk(c, carry):
            r0 = pl.multiple_of(c * SUB, SUB)
            state = [st_ref[hd] for hd in range(N_HEADS)]
            state = _hgrn_sub_chunk(z_ref, kk_ref, b_ref, hn_ref, y_ref, r0, state, False)
            for hd in range(N_HEADS):
                st_ref[hd] = state[hd]
            return carry

        lax.fori_loop(0, n_sub, sub_chunk, 0)


def _mixer_kernel(x_ref, xn_ref, gain_ref, win_ref, lb_ref, hn_ref, lng_ref, lnb_ref, wsp_ref, bsp_ref,
                  wout_ref, o_ref, za_ref, zb_ref, y_ref, b_ref, kk_ref, st_ref):
    def normed(xv):
        ms = jnp.mean(xv * xv, axis=-1, keepdims=True)
        return ((xv * lax.rsqrt(ms + RMS_EPS)) * gain_ref[...]).astype(BF16)

    @pl.when(pl.program_id(1) == 0)
    def _():
        st_ref[...] = jnp.zeros_like(st_ref)

    @pl.when((pl.program_id(0) == 0) & (pl.program_id(1) == 0))
    def _():
        za_ref[...] = _dot(normed(x_ref[0, 0:MIXER_ROWS, :]), win_ref[...])

    tile_refs = (win_ref, lb_ref, hn_ref, lng_ref, lnb_ref, wsp_ref, bsp_ref)
    _mixer_tile(za_ref, zb_ref, normed(x_ref[0, MIXER_ROWS:2 * MIXER_ROWS, :]), *tile_refs,
                y_ref, b_ref, kk_ref, st_ref)
    o_ref[0, 0:MIXER_ROWS, :] = x_ref[0, 0:MIXER_ROWS, :] + _dot(y_ref[...], wout_ref[...])
    _mixer_tile(zb_ref, za_ref, normed(xn_ref[0]), *tile_refs, y_ref, b_ref, kk_ref, st_ref)
    o_ref[0, MIXER_ROWS:2 * MIXER_ROWS, :] = (
        x_ref[0, MIXER_ROWS:2 * MIXER_ROWS, :] + _dot(y_ref[...], wout_ref[...]))


def _mixer(x, gain, w_in, lb, hgrn_norm, ln_g, ln_b, w_sp, b_sp, w_out):
    bsz, seq, _ = x.shape
    step_rows = 2 * MIXER_ROWS
    assert seq % step_rows == 0 and MIXER_ROWS % SUB == 0
    steps = seq // step_rows
    const2 = lambda b, j: (0, 0)
    const3 = lambda b, j: (0, 0, 0)
    once = pl.Buffered(1)

    def next_tile_a(b, j):
        nxt = jnp.minimum(b * steps + j + 1, bsz * steps - 1)
        return (nxt // steps, 2 * (nxt % steps), 0)

    return pl.pallas_call(
        _mixer_kernel,
        grid=(bsz, steps),
        in_specs=[
            pl.BlockSpec((1, step_rows, D_MODEL), lambda b, j: (b, j, 0)),
            pl.BlockSpec((1, MIXER_ROWS, D_MODEL), next_tile_a),
            pl.BlockSpec((1, D_MODEL), const2),
            pl.BlockSpec((D_MODEL, D_IN), const2, pipeline_mode=once),
            pl.BlockSpec((1, D_HGRN), const2),
            pl.BlockSpec((1, HEAD_DIM), const2),
            pl.BlockSpec((1, D_TMLP), const2),
            pl.BlockSpec((1, D_TMLP), const2),
            pl.BlockSpec((N_GROUPS, SUB, SUB), const3),
            pl.BlockSpec((SUB, D_TMLP), const2),
            pl.BlockSpec((D_HGRN + D_TMLP, D_MODEL), const2, pipeline_mode=once),
        ],
        out_specs=pl.BlockSpec((1, step_rows, D_MODEL), lambda b, j: (b, j, 0)),
        out_shape=jax.ShapeDtypeStruct(x.shape, F32),
        scratch_shapes=[
            pltpu.VMEM((MIXER_ROWS, D_IN), F32),
            pltpu.VMEM((MIXER_ROWS, D_IN), F32),
            pltpu.VMEM((MIXER_ROWS, D_HGRN + D_TMLP), BF16),
            pltpu.VMEM((MIXER_ROWS, D_HGRN), F32),
            pltpu.VMEM((MIXER_ROWS, D_HGRN), F32),
            pltpu.VMEM((N_HEADS, HEAD_DIM, HEAD_DIM), F32),
        ],
        compiler_params=pltpu.CompilerParams(
            dimension_semantics=("arbitrary", "arbitrary"),
            vmem_limit_bytes=VMEM_LIMIT_BYTES),
        name="mixer",
    )(x, x, gain, w_in, lb, hgrn_norm, ln_g, ln_b, w_sp, b_sp, w_out)


def _pack_bf16_pairs(a):
    lo = lax.bitcast_convert_type(a[:, :PACKED].astype(BF16).astype(F32), jnp.uint32)
    hi = lax.bitcast_convert_type(a[:, PACKED:].astype(BF16).astype(F32), jnp.uint32)
    return jnp.bitwise_or(lax.shift_right_logical(lo, jnp.uint32(16)),
                          jnp.bitwise_and(hi, jnp.uint32(0xFFFF0000)))


def _unpack_bf16_pairs(u):
    lo = lax.bitcast_convert_type(lax.shift_left(u, jnp.uint32(16)), F32)
    hi = lax.bitcast_convert_type(jnp.bitwise_and(u, jnp.uint32(0xFFFF0000)), F32)
    return lo, hi


def _router_kernel(x_ref, gain_ref, wr_hi_ref, wr_lo_ref, later_ref, hpk_ref, meta_ref, wts_ref,
                   cnt_out_ref, cnt_ref):
    @pl.when(pl.program_id(0) == 0)
    def _():
        cnt_ref[...] = jnp.zeros_like(cnt_ref)

    x = x_ref[...]
    ms = jnp.mean(x * x, axis=-1, keepdims=True)
    h = (x * lax.rsqrt(ms + RMS_EPS)) * gain_ref[...]
    hpk_ref[...] = _pack_bf16_pairs(h)
    h_hi, h_lo = _split_bf16(h)
    logits = _dot(h_hi, wr_hi_ref[...]) + (_dot(h_hi, wr_lo_ref[...]) + _dot(h_lo, wr_hi_ref[...]))
    lt = logits.T
    sub = lax.broadcasted_iota(jnp.int32, (EXPERTS_PER_GROUP, ROUTE_ROWS), 0)
    neg = jnp.float32(-jnp.inf)
    big = jnp.int32(1 << 20)
    gl = jnp.where(sub < N_EXPERT_GROUPS, lt[N_EXPERTS:N_EXPERTS + EXPERTS_PER_GROUP], neg)
    gmax = jnp.max(gl, axis=0, keepdims=True)
    p_sel = 1.0 / jnp.sum(jnp.exp(gl - gmax), axis=0, keepdims=True)
    g_idx = jnp.min(jnp.where(gl == gmax, sub, big), axis=0, keepdims=True)
    el = lt[(N_EXPERT_GROUPS - 1) * EXPERTS_PER_GROUP:N_EXPERTS]
    for g in range(N_EXPERT_GROUPS - 2, -1, -1):
        el = jnp.where(g_idx == g, lt[g * EXPERTS_PER_GROUP:(g + 1) * EXPERTS_PER_GROUP], el)
    v1 = jnp.max(el, axis=0, keepdims=True)
    i1 = jnp.min(jnp.where(el == v1, sub, big), axis=0, keepdims=True)
    el2 = jnp.where(sub == i1, neg, el)
    v2 = jnp.max(el2, axis=0, keepdims=True)
    i2 = jnp.min(jnp.where(el2 == v2, sub, big), axis=0, keepdims=True)
    e2x = jnp.exp(v2 - v1)
    w1 = p_sel / (1.0 + e2x)
    w2 = p_sel * e2x / (1.0 + e2x)
    e1 = g_idx * EXPERTS_PER_GROUP + i1
    e2 = g_idx * EXPERTS_PER_GROUP + i2

    expert = lax.broadcasted_iota(jnp.int32, (N_EXPERTS, ROUTE_ROWS), 0)
    assigned = ((expert == e1) | (expert == e2)).astype(F32)
    earlier = cnt_ref[:, 0:1] + _dot(assigned.astype(BF16), later_ref[...])
    rank1 = jnp.sum(jnp.where(expert == e1, earlier, 0.0), axis=0, keepdims=True).astype(jnp.int32)
    rank2 = jnp.sum(jnp.where(expert == e2, earlier, 0.0), axis=0, keepdims=True).astype(jnp.int32)
    meta_ref[...] = jnp.where(sub == 0, e1, jnp.where(sub == 1, e2, jnp.where(
        sub == 2, rank1, jnp.where(sub == 3, rank2, 0))))
    cnt_ref[...] = cnt_ref[...] + jnp.sum(assigned, axis=1, keepdims=True)
    cnt_out_ref[...] = cnt_ref[...]

    w_rows = jnp.where(sub == 0, w1, jnp.where(sub == 1, w2, 0.0))
    pick = (lax.broadcasted_iota(jnp.int32, (EXPERTS_PER_GROUP, ROUTER_LANES), 0)
            == lax.broadcasted_iota(jnp.int32, (EXPERTS_PER_GROUP, ROUTER_LANES), 1))
    pick = pick.astype(F32).astype(BF16)
    p_hi = w_rows.astype(BF16)
    res = w_rows - p_hi.astype(F32)
    p_mid = res.astype(BF16)
    p_lo = (res - p_mid.astype(F32)).astype(BF16)
    wts_ref[...] = _dot_tn(p_hi, pick) + (_dot_tn(p_mid, pick) + _dot_tn(p_lo, pick))


def _router(x2d, gain, wr_hi, wr_lo, later):
    n = x2d.shape[0]
    assert n % ROUTE_ROWS == 0
    const2 = lambda i: (0, 0)
    rows = lambda i: (i, 0)
    return pl.pallas_call(
        _router_kernel,
        grid=(n // ROUTE_ROWS,),
        in_specs=[
            pl.BlockSpec((ROUTE_ROWS, D_MODEL), rows),
            pl.BlockSpec((1, D_MODEL), const2),
            pl.BlockSpec((D_MODEL, ROUTER_LANES), const2),
            pl.BlockSpec((D_MODEL, ROUTER_LANES), const2),
            pl.BlockSpec((ROUTE_ROWS, ROUTE_ROWS), const2),
        ],
        out_specs=[
            pl.BlockSpec((ROUTE_ROWS, PACKED), rows),
            pl.BlockSpec((EXPERTS_PER_GROUP, ROUTE_ROWS), lambda i: (0, i)),
            pl.BlockSpec((ROUTE_ROWS, ROUTER_LANES), rows),
            pl.BlockSpec((N_EXPERTS, ROUTER_LANES), const2),
        ],
        out_shape=[
            jax.ShapeDtypeStruct((n, PACKED), jnp.uint32),
            jax.ShapeDtypeStruct((EXPERTS_PER_GROUP, n), jnp.int32),
            jax.ShapeDtypeStruct((n, ROUTER_LANES), F32),
            jax.ShapeDtypeStruct((N_EXPERTS, ROUTER_LANES), F32),
        ],
        scratch_shapes=[pltpu.VMEM((N_EXPERTS, ROUTER_LANES), F32)],
        compiler_params=pltpu.CompilerParams(
            dimension_semantics=("arbitrary",), vmem_limit_bytes=VMEM_LIMIT_BYTES),
        name="router",
    )(x2d, gain, wr_hi, wr_lo, later)


def _positions_kernel(meta_ref, row_start_ref, pos_ref):
    meta = meta_ref[...]
    expert = lax.broadcasted_iota(jnp.int32, (N_EXPERTS, ROUTE_ROWS), 0)
    starts = row_start_ref[:, 0:1]
    base1 = jnp.sum(jnp.where(expert == meta[0:1], starts, 0), axis=0, keepdims=True)
    base2 = jnp.sum(jnp.where(expert == meta[1:2], starts, 0), axis=0, keepdims=True)
    sub = lax.broadcasted_iota(jnp.int32, meta.shape, 0)
    pos_ref[...] = jnp.where(sub == 0, base1 + meta[2:3], jnp.where(sub == 1, base2 + meta[3:4], 0))


def _positions(meta, row_start_lanes):
    n = meta.shape[1]
    return pl.pallas_call(
        _positions_kernel,
        grid=(n // ROUTE_ROWS,),
        in_specs=[pl.BlockSpec((EXPERTS_PER_GROUP, ROUTE_ROWS), lambda i: (0, i)),
                  pl.BlockSpec((N_EXPERTS, ROUTER_LANES), lambda i: (0, 0))],
        out_specs=pl.BlockSpec((EXPERTS_PER_GROUP, ROUTE_ROWS), lambda i: (0, i)),
        out_shape=jax.ShapeDtypeStruct(meta.shape, jnp.int32),
        compiler_params=pltpu.CompilerParams(
            dimension_semantics=("arbitrary",), vmem_limit_bytes=VMEM_LIMIT_BYTES),
        name="positions",
    )(meta, row_start_lanes)


def _row_copy(src_ref, src_row, dst_ref, dst_row, sem):
    return pltpu.make_async_copy(src_ref.at[pl.ds(src_row, 1)], dst_ref.at[pl.ds(dst_row, 1)], sem)


def _scatter_kernel(pos1_ref, pos2_ref, hpk_ref, sorted_ref, sem):
    def issue(i, carry):
        _row_copy(hpk_ref, i, sorted_ref, pos1_ref[i], sem).start(priority=0)
        _row_copy(hpk_ref, i, sorted_ref, pos2_ref[i], sem).start(priority=1)
        return carry

    lax.fori_loop(0, SCATTER_ROWS, issue, 0, unroll=DMA_UNROLL)

    def drain(i, carry):
        _row_copy(hpk_ref, 0, sorted_ref, 0, sem).wait()
        _row_copy(hpk_ref, 0, sorted_ref, 0, sem).wait()
        return carry

    lax.fori_loop(0, SCATTER_ROWS, drain, 0, unroll=DMA_UNROLL)


def _scatter(pos1, pos2, hpk, n_rows):
    n = hpk.shape[0]
    assert n % SCATTER_ROWS == 0
    idx = lambda i: (i,)
    return pl.pallas_call(
        _scatter_kernel,
        grid=(n // SCATTER_ROWS,),
        in_specs=[
            pl.BlockSpec((SCATTER_ROWS,), idx, memory_space=pltpu.SMEM),
            pl.BlockSpec((SCATTER_ROWS,), idx, memory_space=pltpu.SMEM),
            pl.BlockSpec((SCATTER_ROWS, PACKED), lambda i: (i, 0)),
        ],
        out_specs=pl.BlockSpec(memory_space=pl.ANY),
        out_shape=jax.ShapeDtypeStruct((n_rows, PACKED), jnp.uint32),
        scratch_shapes=[pltpu.SemaphoreType.DMA],
        compiler_params=pltpu.CompilerParams(
            dimension_semantics=("arbitrary",), vmem_limit_bytes=VMEM_LIMIT_BYTES),
        name="scatter_rows",
    )(pos1, pos2, hpk)


def _expert_kernel(tile_expert_ref, tile_rows_ref, n_used_ref, lhs_ref, wg_ref, wu_ref, wd_ref,
                   o_ref):
    del tile_expert_ref, n_used_ref
    n_valid = tile_rows_ref[pl.program_id(0)]

    @pl.when(n_valid > 0)
    def _():
        valid = lax.broadcasted_iota(jnp.int32, (EXPERT_TILE, 1), 0) < n_valid
        lo, hi = _unpack_bf16_pairs(lhs_ref[...])
        hh = jnp.concatenate([jnp.where(valid, lo, 0.0).astype(BF16),
                              jnp.where(valid, hi, 0.0).astype(BF16)], axis=1)
        w_gu = jnp.concatenate([wg_ref[0, 0].astype(BF16), wu_ref[0, 0].astype(BF16)], axis=1)
        gu = _dot(hh, w_gu)
        hid = _silu(gu[:, :D_EXPERT]) * gu[:, D_EXPERT:]
        o_ref[...] = _pack_bf16_pairs(_dot(hid.astype(BF16), wd_ref[0, 0].astype(BF16)))

    @pl.when(n_valid <= 0)
    def _():
        o_ref[...] = jnp.zeros_like(o_ref)


def _experts(tile_expert, tile_rows, n_used, sorted_rows, w_gate, w_up, w_down, layer):
    n_tiles = sorted_rows.shape[0] // EXPERT_TILE
    live = lambda i, nu: jnp.minimum(i, nu[0] - 1)
    expert = lambda i, te, tr, nu: (layer, te[live(i, nu)], 0, 0)
    return pl.pallas_call(
        _expert_kernel,
        grid_spec=pltpu.PrefetchScalarGridSpec(
            num_scalar_prefetch=3,
            grid=(n_tiles,),
            in_specs=[
                pl.BlockSpec((EXPERT_TILE, PACKED), lambda i, te, tr, nu: (live(i, nu), 0)),
                pl.BlockSpec((1, 1, D_MODEL, D_EXPERT), expert),
                pl.BlockSpec((1, 1, D_MODEL, D_EXPERT), expert),
                pl.BlockSpec((1, 1, D_EXPERT, D_MODEL), expert),
            ],
            out_specs=pl.BlockSpec((EXPERT_TILE, PACKED), lambda i, te, tr, nu: (i, 0)),
        ),
        out_shape=jax.ShapeDtypeStruct(sorted_rows.shape, jnp.uint32),
        compiler_params=pltpu.CompilerParams(
            dimension_semantics=("arbitrary",), vmem_limit_bytes=VMEM_LIMIT_BYTES),
        name="experts",
    )(tile_expert, tile_rows, n_used, sorted_rows, w_gate, w_up, w_down)


def _combine_kernel(pos1_ref, pos2_ref, x_ref, wts_ref, ys_ref, gfin_ref, o_ref, buf1_ref, buf2_ref,
                    sem, *, final_norm):
    def issue(i, carry):
        _row_copy(ys_ref, pos1_ref[i], buf1_ref, i, sem).start(priority=0)
        _row_copy(ys_ref, pos2_ref[i], buf2_ref, i, sem).start(priority=1)
        return carry

    lax.fori_loop(0, COMBINE_ROWS, issue, 0, unroll=DMA_UNROLL)

    def drain(i, carry):
        _row_copy(ys_ref, 0, buf1_ref, 0, sem).wait()
        _row_copy(ys_ref, 0, buf2_ref, 0, sem).wait()
        return carry

    lax.fori_loop(0, COMBINE_ROWS, drain, 0, unroll=DMA_UNROLL)
    lo1, hi1 = _unpack_bf16_pairs(buf1_ref[...])
    lo2, hi2 = _unpack_bf16_pairs(buf2_ref[...])
    w1 = wts_ref[:, 0:1]
    w2 = wts_ref[:, 1:2]
    x = x_ref[...]
    y = jnp.concatenate([x[:, :PACKED] + (w1 * lo1 + w2 * lo2),
                         x[:, PACKED:] + (w1 * hi1 + w2 * hi2)], axis=1)
    if final_norm:
        ms = jnp.mean(y * y, axis=-1, keepdims=True)
        y = (y * lax.rsqrt(ms + RMS_EPS)) * gfin_ref[...]
    o_ref[...] = y


def _combine(pos1, pos2, x2d, wts, ys, gain_final, final_norm):
    n = x2d.shape[0]
    assert n % COMBINE_ROWS == 0
    idx = lambda i: (i,)
    rows = lambda i: (i, 0)
    return pl.pallas_call(
        functools.partial(_combine_kernel, final_norm=final_norm),
        grid=(n // COMBINE_ROWS,),
        in_specs=[
            pl.BlockSpec((COMBINE_ROWS,), idx, memory_space=pltpu.SMEM),
            pl.BlockSpec((COMBINE_ROWS,), idx, memory_space=pltpu.SMEM),
            pl.BlockSpec((COMBINE_ROWS, D_MODEL), rows),
            pl.BlockSpec((COMBINE_ROWS, ROUTER_LANES), rows),
            pl.BlockSpec(memory_space=pl.ANY),
            pl.BlockSpec((1, D_MODEL), lambda i: (0, 0)),
        ],
        out_specs=pl.BlockSpec((COMBINE_ROWS, D_MODEL), rows),
        out_shape=jax.ShapeDtypeStruct(x2d.shape, F32),
        scratch_shapes=[
            pltpu.VMEM((COMBINE_ROWS, PACKED), jnp.uint32),
            pltpu.VMEM((COMBINE_ROWS, PACKED), jnp.uint32),
            pltpu.SemaphoreType.DMA,
        ],
        compiler_params=pltpu.CompilerParams(
            dimension_semantics=("arbitrary",), vmem_limit_bytes=VMEM_LIMIT_BYTES),
        name="combine",
    )(pos1, pos2, x2d, wts, ys, gain_final)


def _moe(x2d, gain, wr_hi, wr_lo, later, w_gate, w_up, w_down, layer, gain_final, final_norm):
    n = x2d.shape[0]
    max_tiles = (2 * n) // EXPERT_TILE + N_EXPERTS
    hpk, meta, wts, cnt = _router(x2d, gain, wr_hi, wr_lo, later)
    counts = cnt[:, 0].astype(jnp.int32)
    tiles_per = (counts + (EXPERT_TILE - 1)) // EXPERT_TILE
    tile_end = jnp.cumsum(tiles_per)
    tile_start = tile_end - tiles_per
    row_start = tile_start * EXPERT_TILE
    tile_ids = jnp.arange(max_tiles, dtype=jnp.int32)
    tile_expert = jnp.minimum(
        jnp.sum((tile_ids[:, None] >= tile_end[None, :]).astype(jnp.int32), axis=1), N_EXPERTS - 1)
    n_used = tile_end[-1:].astype(jnp.int32)
    rows_left = counts[tile_expert] - (tile_ids - tile_start[tile_expert]) * EXPERT_TILE
    tile_rows = jnp.where(tile_ids < n_used[0], jnp.clip(rows_left, 0, EXPERT_TILE), 0)
    pos = _positions(meta, jnp.broadcast_to(row_start[:, None], (N_EXPERTS, ROUTER_LANES)))
    sorted_rows = _scatter(pos[0], pos[1], hpk, max_tiles * EXPERT_TILE)
    ys = _experts(tile_expert, tile_rows.astype(jnp.int32), n_used, sorted_rows, w_gate, w_up, w_down,
                  layer)
    return _combine(pos[0], pos[1], x2d, wts, ys, gain_final, final_norm)


def kernel(x, lb_logits, norm_mix, w_in, hgrn_norm, tmlp_ln_g, tmlp_ln_b, w_spatial, b_spatial,
           w_out, norm_ffn, w_router_group, w_router_expert, w_gate, w_up, w_down, norm_final):
    depth = w_in.shape[0]
    bsz, seq, _ = x.shape
    p = jax.nn.softmax(lb_logits.astype(F32), axis=0)
    lower_bounds = jnp.cumsum(p, axis=0) - p[0:1]
    tril = jnp.tril(jnp.ones((SUB, SUB), dtype=bool))
    later = jnp.triu(jnp.ones((ROUTE_ROWS, ROUTE_ROWS), BF16), 1)
    for layer in range(depth):
        w_sp = jnp.where(tril[None], w_spatial[layer], 0.0).astype(BF16)
        b_sp = jnp.repeat(b_spatial[layer].T, GROUP_DIM, axis=1)
        x = _mixer(x, norm_mix[layer][None], w_in[layer].astype(BF16), lower_bounds[layer][None],
                   hgrn_norm[layer][None], tmlp_ln_g[layer][None], tmlp_ln_b[layer][None],
                   w_sp, b_sp, w_out[layer].astype(BF16))
        w_r = jnp.concatenate([w_router_expert[layer], w_router_group[layer]], axis=1)
        w_r = jnp.pad(w_r, ((0, 0), (0, ROUTER_LANES - w_r.shape[1])))
        wr_hi, wr_lo = _split_bf16(w_r)
        x2d = _moe(x.reshape(bsz * seq, D_MODEL), norm_ffn[layer][None], wr_hi, wr_lo, later,
                   w_gate, w_up, w_down, layer, norm_final[None], final_norm=(layer == depth - 1))
        x = x2d.reshape(bsz, seq, D_MODEL)
    return x
```

```python
import functools

import jax
import jax.numpy as jnp
from jax import lax
from jax.experimental import pallas as pl
from jax.experimental.pallas import tpu as pltpu

F32 = jnp.float32
BF16 = jnp.bfloat16

D_MODEL = 1024
N_HEADS = 4
HEAD_DIM = 128
D_HGRN = N_HEADS * HEAD_DIM
N_GROUPS = 4
GROUP_DIM = 128
D_TMLP = N_GROUPS * GROUP_DIM
D_IN = 4 * D_HGRN + 2 * D_TMLP
SUB = 128
N_EXPERT_GROUPS = 4
EXPERTS_PER_GROUP = 8
N_EXPERTS = N_EXPERT_GROUPS * EXPERTS_PER_GROUP
D_EXPERT = 256
ROUTER_LANES = 128
RMS_EPS = 1e-6
LN_EPS = 1e-5
F_FLOOR = 1e-30
HGRN_SAFE_EXP = 60.0
SQRT_HALF = 0.7071067811865476

MIXER_ROWS = 512
PROJ_PIECE = 256
ROUTE_ROWS = 1024
SCATTER_ROWS = 2048
COMBINE_ROWS = 1024
EXPERT_TILE = 1024
PACKED = D_MODEL // 2
DMA_UNROLL = 8
VMEM_LIMIT_BYTES = 56 * 1024 * 1024


def _dot(a, b):
    return jnp.dot(a, b, preferred_element_type=F32)


def _dot_nt(a, b):
    return lax.dot_general(a, b, (((1,), (1,)), ((), ())), preferred_element_type=F32)


def _dot_tn(a, b):
    return lax.dot_general(a, b, (((0,), (0,)), ((), ())), preferred_element_type=F32)


def _split_bf16(a):
    hi = a.astype(BF16)
    lo = (a - hi.astype(F32)).astype(BF16)
    return hi, lo


def _gelu(a):
    return 0.5 * a * (1.0 + lax.erf(a * SQRT_HALF))


def _silu(a):
    return a * jax.nn.sigmoid(a)


def _boundary_rows(b_ref, r0, m, width):
    pieces = []
    if 2 * m >= 8:
        for s0 in range(0, SUB, 2 * m):
            row = b_ref[pl.ds(r0 + (s0 + m - 1), 1), :]
            pieces.append(jnp.broadcast_to(row, (2 * m, width)))
    else:
        row8 = lax.broadcasted_iota(jnp.int32, (8, width), 0)
        for g0 in range(0, SUB, 8):
            acc = None
            for s0 in range(0, 8, 2 * m):
                row = jnp.broadcast_to(b_ref[pl.ds(r0 + (g0 + s0 + m - 1), 1), :], (8, width))
                acc = row if acc is None else jnp.where(row8 >= s0, row, acc)
            pieces.append(acc)
    return jnp.concatenate(pieces, axis=0)


def _half_middle_rows(b_ref, r0, width):
    half = SUB // 2
    return jnp.concatenate(
        [jnp.broadcast_to(b_ref[pl.ds(r0 + (s0 + half // 2 - 1), 1), :], (half, width))
         for s0 in range(0, SUB, half)], axis=0)


def _hgrn_level(att, q, kk, b, b_ref, r0, m, tx, row):
    ref_pt = _boundary_rows(b_ref, r0, m, D_HGRN)
    decay = jnp.exp(-jnp.abs(b - ref_pt))
    right = jnp.bitwise_and(row, m) != 0
    qt = jnp.where(right, q * decay, 0.0).astype(BF16)
    kt = jnp.where(right, 0.0, kk * decay).astype(BF16)
    same_block = tx < 2 * m
    out = []
    for h in range(N_HEADS):
        sl = slice(h * HEAD_DIM, (h + 1) * HEAD_DIM)
        term = jnp.where(same_block, _dot_nt(qt[:, sl], kt[:, sl]), 0.0)
        out.append(term if att is None else att[h] + term)
    return out


def _hgrn_attention(q, kk, b, b_ref, r0, shared_reference):
    t_idx = lax.broadcasted_iota(jnp.int32, (SUB, SUB), 0)
    s_idx = lax.broadcasted_iota(jnp.int32, (SUB, SUB), 1)
    tx = jnp.bitwise_xor(t_idx, s_idx)
    row = lax.broadcasted_iota(jnp.int32, (SUB, D_HGRN), 0)
    half = SUB // 2
    att = _hgrn_level(None, q, kk, b, b_ref, r0, half, tx, row)
    if shared_reference:
        expo = b - _half_middle_rows(b_ref, r0, D_HGRN)
        qt = (q * jnp.exp(expo)).astype(BF16)
        kt = (kk * jnp.exp(-expo)).astype(BF16)
        keep = (tx < half) & (s_idx <= t_idx)
        for h in range(N_HEADS):
            sl = slice(h * HEAD_DIM, (h + 1) * HEAD_DIM)
            att[h] = att[h] + jnp.where(keep, _dot_nt(qt[:, sl], kt[:, sl]), 0.0)
        return att
    qb = q.astype(BF16)
    kb = kk.astype(BF16)
    for h in range(N_HEADS):
        sl = slice(h * HEAD_DIM, (h + 1) * HEAD_DIM)
        att[h] = att[h] + jnp.where(tx == 0, _dot_nt(qb[:, sl], kb[:, sl]), 0.0)
    m = 1
    while m < half:
        att = _hgrn_level(att, q, kk, b, b_ref, r0, m, tx, row)
        m *= 2
    return att


def _hgrn_sub_chunk(z_ref, kk_ref, b_ref, hn_ref, y_ref, r0, state, shared_reference, per_head=None):
    rows = pl.ds(r0, SUB)
    q = _silu(z_ref[rows, 0:D_HGRN])
    kk = kk_ref[rows, :]
    b = b_ref[rows, :]
    v = z_ref[rows, 2 * D_HGRN:3 * D_HGRN].astype(BF16)
    att = _hgrn_attention(q, kk, b, b_ref, r0, shared_reference)
    b_end = b_ref[pl.ds(r0 + (SUB - 1), 1), :]
    q0 = (q * jnp.exp(b)).astype(BF16)
    k_end = (kk * jnp.exp(b_end - b)).astype(BF16)
    s_decay = jnp.exp(b_end)
    g = _silu(z_ref[rows, 3 * D_HGRN:4 * D_HGRN])
    new_state = []
    for hd in range(N_HEADS):
        if per_head is not None:
            per_head(hd)
        sl = slice(hd * HEAD_DIM, (hd + 1) * HEAD_DIM)
        st = state[hd]
        o = _dot(att[hd].astype(BF16), v[:, sl]) + _dot_nt(q0[:, sl], st.astype(BF16))
        new_state.append(st * s_decay[:, sl] + _dot_tn(v[:, sl], k_end[:, sl]))
        oms = jnp.mean(o * o, axis=-1, keepdims=True)
        on = (o * lax.rsqrt(oms + RMS_EPS)) * hn_ref[...]
        y_ref[rows, sl] = (on * g[:, sl]).astype(BF16)
    return new_state


def _hgrn_head_shared(z_ref, kk_ref, b_ref, hn_ref, y_ref, r0, hd, st):
    rows = pl.ds(r0, SUB)
    sl = slice(hd * HEAD_DIM, (hd + 1) * HEAD_DIM)
    half = SUB // 2
    q = _silu(z_ref[rows, hd * HEAD_DIM:(hd + 1) * HEAD_DIM])
    kk = kk_ref[rows, sl]
    b = b_ref[rows, sl]
    v = z_ref[rows, 2 * D_HGRN + hd * HEAD_DIM:2 * D_HGRN + (hd + 1) * HEAD_DIM].astype(BF16)
    t_idx = lax.broadcasted_iota(jnp.int32, (SUB, SUB), 0)
    s_idx = lax.broadcasted_iota(jnp.int32, (SUB, SUB), 1)
    upper = lax.broadcasted_iota(jnp.int32, (SUB, HEAD_DIM), 0) >= half

    boundary = jnp.broadcast_to(b_ref[pl.ds(r0 + (half - 1), 1), sl], (SUB, HEAD_DIM))
    decay = jnp.exp(-jnp.abs(b - boundary))
    qt_far = jnp.where(upper, q * decay, 0.0).astype(BF16)
    kt_far = jnp.where(upper, 0.0, kk * decay).astype(BF16)
    middle = jnp.concatenate(
        [jnp.broadcast_to(b_ref[pl.ds(r0 + (s0 + half // 2 - 1), 1), sl], (half, HEAD_DIM))
         for s0 in range(0, SUB, half)], axis=0)
    expo = b - middle
    qt_near = (q * jnp.exp(expo)).astype(BF16)
    kt_near = (kk * jnp.exp(-expo)).astype(BF16)
    keep = (jnp.bitwise_xor(t_idx, s_idx) < half) & (s_idx <= t_idx)
    att = _dot_nt(qt_far, kt_far) + jnp.where(keep, _dot_nt(qt_near, kt_near), 0.0)

    b_end = b_ref[pl.ds(r0 + (SUB - 1), 1), sl]
    q0 = (q * jnp.exp(b)).astype(BF16)
    k_end = (kk * jnp.exp(b_end - b)).astype(BF16)
    o = _dot(att.astype(BF16), v) + _dot_nt(q0, st.astype(BF16))
    new_st = st * jnp.exp(b_end) + _dot_tn(v, k_end)
    oms = jnp.mean(o * o, axis=-1, keepdims=True)
    on = (o * lax.rsqrt(oms + RMS_EPS)) * hn_ref[...]
    g = _silu(z_ref[rows, 3 * D_HGRN + hd * HEAD_DIM:3 * D_HGRN + (hd + 1) * HEAD_DIM])
    y_ref[rows, sl] = (on * g).astype(BF16)
    return new_st


def _mixer_tile(z_ref, zn_ref, h_next, win_ref, lb_ref, hn_ref, lng_ref, lnb_ref, wsp_ref, bsp_ref,
                y_ref, b_ref, kk_ref, st_ref):
    def project(first, count):
        for p in range(first, first + count):
            cols = slice(p * PROJ_PIECE, (p + 1) * PROJ_PIECE)
            zn_ref[:, cols] = _dot(h_next, win_ref[:, cols])

    n_pieces = D_IN // PROJ_PIECE
    early = n_pieces // 3

    t_idx = lax.broadcasted_iota(jnp.int32, (SUB, SUB), 0)
    s_idx = lax.broadcasted_iota(jnp.int32, (SUB, SUB), 1)
    tri = (s_idx <= t_idx).astype(BF16)
    n_sub = MIXER_ROWS // SUB

    worst = jnp.zeros((SUB, D_HGRN), F32)
    for c in range(n_sub):
        rows = pl.ds(c * SUB, SUB)
        lb = lb_ref[...]
        fg = lb + (1.0 - lb) * jax.nn.sigmoid(z_ref[rows, D_HGRN:2 * D_HGRN])
        lf_hi, lf_lo = _split_bf16(jnp.log(jnp.maximum(fg, F_FLOOR)))
        kk_ref[rows, :] = 1.0 - fg
        b = _dot(tri, lf_hi) + _dot(tri, lf_lo)
        b_ref[rows, :] = b
        worst = jnp.maximum(worst, jnp.abs(b - _half_middle_rows(b_ref, c * SUB, D_HGRN)))
    shared_ok = jnp.max(worst) <= HGRN_SAFE_EXP

    for c in range(n_sub):
        project(c * early // n_sub, (c + 1) * early // n_sub - c * early // n_sub)
        rows = pl.ds(c * SUB, SUB)
        u = _gelu(z_ref[rows, 4 * D_HGRN:4 * D_HGRN + D_TMLP])
        vv = _gelu(z_ref[rows, 4 * D_HGRN + D_TMLP:D_IN])
        for gi in range(N_GROUPS):
            sl = slice(gi * GROUP_DIM, (gi + 1) * GROUP_DIM)
            vg = vv[:, sl]
            mu = jnp.mean(vg, axis=-1, keepdims=True)
            cen = vg - mu
            var = jnp.mean(cen * cen, axis=-1, keepdims=True)
            vn = (cen * lax.rsqrt(var + LN_EPS)) * lng_ref[:, sl] + lnb_ref[:, sl]
            mixed = _dot(wsp_ref[gi], vn.astype(BF16)) + bsp_ref[:, sl]
            y_ref[rows, D_HGRN + gi * GROUP_DIM:D_HGRN + (gi + 1) * GROUP_DIM] = (
                u[:, sl] * mixed).astype(BF16)

    late = n_pieces - early

    @pl.when(shared_ok)
    def _():
        state = [st_ref[hd] for hd in range(N_HEADS)]
        stride = (n_sub * N_HEADS) // late

        def piece_for_slot(slot):
            if slot % stride == 0:
                project(early + slot // stride, 1)

        for c in range(n_sub):
            for hd in range(N_HEADS):
                piece_for_slot(c * N_HEADS + hd)
                state[hd] = _hgrn_head_shared(z_ref, kk_ref, b_ref, hn_ref, y_ref, c * SUB, hd,
                                              state[hd])
        for hd in range(N_HEADS):
            st_ref[hd] = state[hd]

    @pl.when(jnp.logical_not(shared_ok))
    def _():
        project(early, late)

        def sub_chunk(c, carry):
            r0 = pl.multiple_of(c * SUB, SUB)
            state = [st_ref[hd] for hd in range(N_HEADS)]
            state = _hgrn_sub_chunk(z_ref, kk_ref, b_ref, hn_ref, y_ref, r0, state, False)
            for hd in range(N_HEADS):
                st_ref[hd] = state[hd]
            return carry

        lax.fori_loop(0, n_sub, sub_chunk, 0)


def _mixer_kernel(x_ref, xn_ref, gain_ref, win_ref, lb_ref, hn_ref, lng_ref, lnb_ref, wsp_ref, bsp_ref,
                  wout_ref, o_ref, za_ref, zb_ref, y_ref, b_ref, kk_ref, st_ref):
    def normed(xv):
        ms = jnp.mean(xv * xv, axis=-1, keepdims=True)
        return ((xv * lax.rsqrt(ms + RMS_EPS)) * gain_ref[...]).astype(BF16)

    @pl.when(pl.program_id(1) == 0)
    def _():
        st_ref[...] = jnp.zeros_like(st_ref)

    @pl.when((pl.program_id(0) == 0) & (pl.program_id(1) == 0))
    def _():
        za_ref[...] = _dot(normed(x_ref[0, 0:MIXER_ROWS, :]), win_ref[...])

    tile_refs = (win_ref, lb_ref, hn_ref, lng_ref, lnb_ref, wsp_ref, bsp_ref)
    _mixer_tile(za_ref, zb_ref, normed(x_ref[0, MIXER_ROWS:2 * MIXER_ROWS, :]), *tile_refs,
                y_ref, b_ref, kk_ref, st_ref)
    o_ref[0, 0:MIXER_ROWS, :] = x_ref[0, 0:MIXER_ROWS, :] + _dot(y_ref[...], wout_ref[...])
    _mixer_tile(zb_ref, za_ref, normed(xn_ref[0]), *tile_refs, y_ref, b_ref, kk_ref, st_ref)
    o_ref[0, MIXER_ROWS:2 * MIXER_ROWS, :] = (
        x_ref[0, MIXER_ROWS:2 * MIXER_ROWS, :] + _dot(y_ref[...], wout_ref[...]))


def _mixer(x, gain, w_in, lb, hgrn_norm, ln_g, ln_b, w_sp, b_sp, w_out):
    bsz, seq, _ = x.shape
    step_rows = 2 * MIXER_ROWS
    assert seq % step_rows == 0 and MIXER_ROWS % SUB == 0
    steps = seq // step_rows
    const2 = lambda b, j: (0, 0)
    const3 = lambda b, j: (0, 0, 0)
    once = pl.Buffered(1)

    def next_tile_a(b, j):
        nxt = jnp.minimum(b * steps + j + 1, bsz * steps - 1)
        return (nxt // steps, 2 * (nxt % steps), 0)

    return pl.pallas_call(
        _mixer_kernel,
        grid=(bsz, steps),
        in_specs=[
            pl.BlockSpec((1, step_rows, D_MODEL), lambda b, j: (b, j, 0)),
            pl.BlockSpec((1, MIXER_ROWS, D_MODEL), next_tile_a),
            pl.BlockSpec((1, D_MODEL), const2),
            pl.BlockSpec((D_MODEL, D_IN), const2, pipeline_mode=once),
            pl.BlockSpec((1, D_HGRN), const2),
            pl.BlockSpec((1, HEAD_DIM), const2),
            pl.BlockSpec((1, D_TMLP), const2),
            pl.BlockSpec((1, D_TMLP), const2),
            pl.BlockSpec((N_GROUPS, SUB, SUB), const3),
            pl.BlockSpec((SUB, D_TMLP), const2),
            pl.BlockSpec((D_HGRN + D_TMLP, D_MODEL), const2, pipeline_mode=once),
        ],
        out_specs=pl.BlockSpec((1, step_rows, D_MODEL), lambda b, j: (b, j, 0)),
        out_shape=jax.ShapeDtypeStruct(x.shape, F32),
        scratch_shapes=[
            pltpu.VMEM((MIXER_ROWS, D_IN), F32),
            pltpu.VMEM((MIXER_ROWS, D_IN), F32),
            pltpu.VMEM((MIXER_ROWS, D_HGRN + D_TMLP), BF16),
            pltpu.VMEM((MIXER_ROWS, D_HGRN), F32),
            pltpu.VMEM((MIXER_ROWS, D_HGRN), F32),
            pltpu.VMEM((N_HEADS, HEAD_DIM, HEAD_DIM), F32),
        ],
        compiler_params=pltpu.CompilerParams(
            dimension_semantics=("arbitrary", "arbitrary"),
            vmem_limit_bytes=VMEM_LIMIT_BYTES),
        name="mixer",
    )(x, x, gain, w_in, lb, hgrn_norm, ln_g, ln_b, w_sp, b_sp, w_out)


def _pack_bf16_pairs(a):
    lo = lax.bitcast_convert_type(a[:, :PACKED].astype(BF16).astype(F32), jnp.uint32)
    hi = lax.bitcast_convert_type(a[:, PACKED:].astype(BF16).astype(F32), jnp.uint32)
    return jnp.bitwise_or(lax.shift_right_logical(lo, jnp.uint32(16)),
                          jnp.bitwise_and(hi, jnp.uint32(0xFFFF0000)))


def _unpack_bf16_pairs(u):
    lo = lax.bitcast_convert_type(lax.shift_left(u, jnp.uint32(16)), F32)
    hi = lax.bitcast_convert_type(jnp.bitwise_and(u, jnp.uint32(0xFFFF0000)), F32)
    return lo, hi


def _router_kernel(x_ref, gain_ref, wr_hi_ref, wr_lo_ref, later_ref, hpk_ref, meta_ref, wts_ref,
                   cnt_out_ref, cnt_ref):
    @pl.when(pl.program_id(0) == 0)
    def _():
        cnt_ref[...] = jnp.zeros_like(cnt_ref)

    x = x_ref[...]
    ms = jnp.mean(x * x, axis=-1, keepdims=True)
    h = (x * lax.rsqrt(ms + RMS_EPS)) * gain_ref[...]
    hpk_ref[...] = _pack_bf16_pairs(h)
    h_hi, h_lo = _split_bf16(h)
    logits = _dot(h_hi, wr_hi_ref[...]) + (_dot(h_hi, wr_lo_ref[...]) + _dot(h_lo, wr_hi_ref[...]))
    lt = logits.T
    sub = lax.broadcasted_iota(jnp.int32, (EXPERTS_PER_GROUP, ROUTE_ROWS), 0)
    neg = jnp.float32(-jnp.inf)
    big = jnp.int32(1 << 20)
    gl = jnp.where(sub < N_EXPERT_GROUPS, lt[N_EXPERTS:N_EXPERTS + EXPERTS_PER_GROUP], neg)
    gmax = jnp.max(gl, axis=0, keepdims=True)
    p_sel = 1.0 / jnp.sum(jnp.exp(gl - gmax), axis=0, keepdims=True)
    g_idx = jnp.min(jnp.where(gl == gmax, sub, big), axis=0, keepdims=True)
    el = lt[(N_EXPERT_GROUPS - 1) * EXPERTS_PER_GROUP:N_EXPERTS]
    for g in range(N_EXPERT_GROUPS - 2, -1, -1):
        el = jnp.where(g_idx == g, lt[g * EXPERTS_PER_GROUP:(g + 1) * EXPERTS_PER_GROUP], el)
    v1 = jnp.max(el, axis=0, keepdims=True)
    i1 = jnp.min(jnp.where(el == v1, sub, big), axis=0, keepdims=True)
    el2 = jnp.where(sub == i1, neg, el)
    v2 = jnp.max(el2, axis=0, keepdims=True)
    i2 = jnp.min(jnp.where(el2 == v2, sub, big), axis=0, keepdims=True)
    e2x = jnp.exp(v2 - v1)
    w1 = p_sel / (1.0 + e2x)
    w2 = p_sel * e2x / (1.0 + e2x)
    e1 = g_idx * EXPERTS_PER_GROUP + i1
    e2 = g_idx * EXPERTS_PER_GROUP + i2

    expert = lax.broadcasted_iota(jnp.int32, (N_EXPERTS, ROUTE_ROWS), 0)
    assigned = ((expert == e1) | (expert == e2)).astype(F32)
    earlier = cnt_ref[:, 0:1] + _dot(assigned.astype(BF16), later_ref[...])
    rank1 = jnp.sum(jnp.where(expert == e1, earlier, 0.0), axis=0, keepdims=True).astype(jnp.int32)
    rank2 = jnp.sum(jnp.where(expert == e2, earlier, 0.0), axis=0, keepdims=True).astype(jnp.int32)
    meta_ref[...] = jnp.where(sub == 0, e1, jnp.where(sub == 1, e2, jnp.where(
        sub == 2, rank1, jnp.where(sub == 3, rank2, 0))))
    cnt_ref[...] = cnt_ref[...] + jnp.sum(assigned, axis=1, keepdims=True)
    cnt_out_ref[...] = cnt_ref[...]

    w_rows = jnp.where(sub == 0, w1, jnp.where(sub == 1, w2, 0.0))
    pick = (lax.broadcasted_iota(jnp.int32, (EXPERTS_PER_GROUP, ROUTER_LANES), 0)
            == lax.broadcasted_iota(jnp.int32, (EXPERTS_PER_GROUP, ROUTER_LANES), 1))
    pick = pick.astype(F32).astype(BF16)
    p_hi = w_rows.astype(BF16)
    res = w_rows - p_hi.astype(F32)
    p_mid = res.astype(BF16)
    p_lo = (res - p_mid.astype(F32)).astype(BF16)
    wts_ref[...] = _dot_tn(p_hi, pick) + (_dot_tn(p_mid, pick) + _dot_tn(p_lo, pick))


def _router(x2d, gain, wr_hi, wr_lo, later):
    n = x2d.shape[0]
    assert n % ROUTE_ROWS == 0
    const2 = lambda i: (0, 0)
    rows = lambda i: (i, 0)
    return pl.pallas_call(
        _router_kernel,
        grid=(n // ROUTE_ROWS,),
        in_specs=[
            pl.BlockSpec((ROUTE_ROWS, D_MODEL), rows),
            pl.BlockSpec((1, D_MODEL), const2),
            pl.BlockSpec((D_MODEL, ROUTER_LANES), const2),
            pl.BlockSpec((D_MODEL, ROUTER_LANES), const2),
            pl.BlockSpec((ROUTE_ROWS, ROUTE_ROWS), const2),
        ],
        out_specs=[
            pl.BlockSpec((ROUTE_ROWS, PACKED), rows),
            pl.BlockSpec((EXPERTS_PER_GROUP, ROUTE_ROWS), lambda i: (0, i)),
            pl.BlockSpec((ROUTE_ROWS, ROUTER_LANES), rows),
            pl.BlockSpec((N_EXPERTS, ROUTER_LANES), const2),
        ],
        out_shape=[
            jax.ShapeDtypeStruct((n, PACKED), jnp.uint32),
            jax.ShapeDtypeStruct((EXPERTS_PER_GROUP, n), jnp.int32),
            jax.ShapeDtypeStruct((n, ROUTER_LANES), F32),
            jax.ShapeDtypeStruct((N_EXPERTS, ROUTER_LANES), F32),
        ],
        scratch_shapes=[pltpu.VMEM((N_EXPERTS, ROUTER_LANES), F32)],
        compiler_params=pltpu.CompilerParams(
            dimension_semantics=("arbitrary",), vmem_limit_bytes=VMEM_LIMIT_BYTES),
        name="router",
    )(x2d, gain, wr_hi, wr_lo, later)


def _positions_kernel(meta_ref, row_start_ref, pos_ref):
    meta = meta_ref[...]
    expert = lax.broadcasted_iota(jnp.int32, (N_EXPERTS, ROUTE_ROWS), 0)
    starts = row_start_ref[:, 0:1]
    base1 = jnp.sum(jnp.where(expert == meta[0:1], starts, 0), axis=0, keepdims=True)
    base2 = jnp.sum(jnp.where(expert == meta[1:2], starts, 0), axis=0, keepdims=True)
    sub = lax.broadcasted_iota(jnp.int32, meta.shape, 0)
    pos_ref[...] = jnp.where(sub == 0, base1 + meta[2:3], jnp.where(sub == 1, base2 + meta[3:4], 0))


def _positions(meta, row_start_lanes):
    n = meta.shape[1]
    return pl.pallas_call(
        _positions_kernel,
        grid=(n // ROUTE_ROWS,),
        in_specs=[pl.BlockSpec((EXPERTS_PER_GROUP, ROUTE_ROWS), lambda i: (0, i)),
                  pl.BlockSpec((N_EXPERTS, ROUTER_LANES), lambda i: (0, 0))],
        out_specs=pl.BlockSpec((EXPERTS_PER_GROUP, ROUTE_ROWS), lambda i: (0, i)),
        out_shape=jax.ShapeDtypeStruct(meta.shape, jnp.int32),
        compiler_params=pltpu.CompilerParams(
            dimension_semantics=("arbitrary",), vmem_limit_bytes=VMEM_LIMIT_BYTES),
        name="positions",
    )(meta, row_start_lanes)


def _row_copy(src_ref, src_row, dst_ref, dst_row, sem):
    return pltpu.make_async_copy(src_ref.at[pl.ds(src_row, 1)], dst_ref.at[pl.ds(dst_row, 1)], sem)


def _scatter_kernel(pos1_ref, pos2_ref, hpk_ref, sorted_ref, sem):
    def issue(i, carry):
        _row_copy(hpk_ref, i, sorted_ref, pos1_ref[i], sem).start(priority=0)
        _row_copy(hpk_ref, i, sorted_ref, pos2_ref[i], sem).start(priority=1)
        return carry

    lax.fori_loop(0, SCATTER_ROWS, issue, 0, unroll=DMA_UNROLL)

    def drain(i, carry):
        _row_copy(hpk_ref, 0, sorted_ref, 0, sem).wait()
        _row_copy(hpk_ref, 0, sorted_ref, 0, sem).wait()
        return carry

    lax.fori_loop(0, SCATTER_ROWS, drain, 0, unroll=DMA_UNROLL)


def _scatter(pos1, pos2, hpk, n_rows):
    n = hpk.shape[0]
    assert n % SCATTER_ROWS == 0
    idx = lambda i: (i,)
    return pl.pallas_call(
        _scatter_kernel,
        grid=(n // SCATTER_ROWS,),
        in_specs=[
            pl.BlockSpec((SCATTER_ROWS,), idx, memory_space=pltpu.SMEM),
            pl.BlockSpec((SCATTER_ROWS,), idx, memory_space=pltpu.SMEM),
            pl.BlockSpec((SCATTER_ROWS, PACKED), lambda i: (i, 0)),
        ],
        out_specs=pl.BlockSpec(memory_space=pl.ANY),
        out_shape=jax.ShapeDtypeStruct((n_rows, PACKED), jnp.uint32),
        scratch_shapes=[pltpu.SemaphoreType.DMA],
        compiler_params=pltpu.CompilerParams(
            dimension_semantics=("arbitrary",), vmem_limit_bytes=VMEM_LIMIT_BYTES),
        name="scatter_rows",
    )(pos1, pos2, hpk)


def _expert_kernel(tile_expert_ref, tile_rows_ref, n_used_ref, lhs_ref, wg_ref, wu_ref, wd_ref,
                   o_ref):
    del tile_expert_ref, n_used_ref
    n_valid = tile_rows_ref[pl.program_id(0)]

    @pl.when(n_valid > 0)
    def _():
        valid = lax.broadcasted_iota(jnp.int32, (EXPERT_TILE, 1), 0) < n_valid
        lo, hi = _unpack_bf16_pairs(lhs_ref[...])
        hh = jnp.concatenate([jnp.where(valid, lo, 0.0).astype(BF16),
                              jnp.where(valid, hi, 0.0).astype(BF16)], axis=1)
        w_gu = jnp.concatenate([wg_ref[0, 0].astype(BF16), wu_ref[0, 0].astype(BF16)], axis=1)
        gu = _dot(hh, w_gu)
        hid = _silu(gu[:, :D_EXPERT]) * gu[:, D_EXPERT:]
        o_ref[...] = _pack_bf16_pairs(_dot(hid.astype(BF16), wd_ref[0, 0].astype(BF16)))

    @pl.when(n_valid <= 0)
    def _():
        o_ref[...] = jnp.zeros_like(o_ref)


def _experts(tile_expert, tile_rows, n_used, sorted_rows, w_gate, w_up, w_down, layer):
    n_tiles = sorted_rows.shape[0] // EXPERT_TILE
    live = lambda i, nu: jnp.minimum(i, nu[0] - 1)
    expert = lambda i, te, tr, nu: (layer, te[live(i, nu)], 0, 0)
    return pl.pallas_call(
        _expert_kernel,
        grid_spec=pltpu.PrefetchScalarGridSpec(
            num_scalar_prefetch=3,
            grid=(n_tiles,),
            in_specs=[
                pl.BlockSpec((EXPERT_TILE, PACKED), lambda i, te, tr, nu: (live(i, nu), 0)),
                pl.BlockSpec((1, 1, D_MODEL, D_EXPERT), expert),
                pl.BlockSpec((1, 1, D_MODEL, D_EXPERT), expert),
                pl.BlockSpec((1, 1, D_EXPERT, D_MODEL), expert),
            ],
            out_specs=pl.BlockSpec((EXPERT_TILE, PACKED), lambda i, te, tr, nu: (i, 0)),
        ),
        out_shape=jax.ShapeDtypeStruct(sorted_rows.shape, jnp.uint32),
        compiler_params=pltpu.CompilerParams(
            dimension_semantics=("arbitrary",), vmem_limit_bytes=VMEM_LIMIT_BYTES),
        name="experts",
    )(tile_expert, tile_rows, n_used, sorted_rows, w_gate, w_up, w_down)


def _combine_kernel(pos1_ref, pos2_ref, x_ref, wts_ref, ys_ref, gfin_ref, o_ref, buf1_ref, buf2_ref,
                    sem, *, final_norm):
    def issue(i, carry):
        _row_copy(ys_ref, pos1_ref[i], buf1_ref, i, sem).start(priority=0)
        _row_copy(ys_ref, pos2_ref[i], buf2_ref, i, sem).start(priority=1)
        return carry

    lax.fori_loop(0, COMBINE_ROWS, issue, 0, unroll=DMA_UNROLL)

    def drain(i, carry):
        _row_copy(ys_ref, 0, buf1_ref, 0, sem).wait()
        _row_copy(ys_ref, 0, buf2_ref, 0, sem).wait()
        return carry

    lax.fori_loop(0, COMBINE_ROWS, drain, 0, unroll=DMA_UNROLL)
    lo1, hi1 = _unpack_bf16_pairs(buf1_ref[...])
    lo2, hi2 = _unpack_bf16_pairs(buf2_ref[...])
    w1 = wts_ref[:, 0:1]
    w2 = wts_ref[:, 1:2]
    x = x_ref[...]
    y = jnp.concatenate([x[:, :PACKED] + (w1 * lo1 + w2 * lo2),
                         x[:, PACKED:] + (w1 * hi1 + w2 * hi2)], axis=1)
    if final_norm:
        ms = jnp.mean(y * y, axis=-1, keepdims=True)
        y = (y * lax.rsqrt(ms + RMS_EPS)) * gfin_ref[...]
    o_ref[...] = y


def _combine(pos1, pos2, x2d, wts, ys, gain_final, final_norm):
    n = x2d.shape[0]
    assert n % COMBINE_ROWS == 0
    idx = lambda i: (i,)
    rows = lambda i: (i, 0)
    return pl.pallas_call(
        functools.partial(_combine_kernel, final_norm=final_norm),
        grid=(n // COMBINE_ROWS,),
        in_specs=[
            pl.BlockSpec((COMBINE_ROWS,), idx, memory_space=pltpu.SMEM),
            pl.BlockSpec((COMBINE_ROWS,), idx, memory_space=pltpu.SMEM),
            pl.BlockSpec((COMBINE_ROWS, D_MODEL), rows),
            pl.BlockSpec((COMBINE_ROWS, ROUTER_LANES), rows),
            pl.BlockSpec(memory_space=pl.ANY),
            pl.BlockSpec((1, D_MODEL), lambda i: (0, 0)),
        ],
        out_specs=pl.BlockSpec((COMBINE_ROWS, D_MODEL), rows),
        out_shape=jax.ShapeDtypeStruct(x2d.shape, F32),
        scratch_shapes=[
            pltpu.VMEM((COMBINE_ROWS, PACKED), jnp.uint32),
            pltpu.VMEM((COMBINE_ROWS, PACKED), jnp.uint32),
            pltpu.SemaphoreType.DMA,
        ],
        compiler_params=pltpu.CompilerParams(
            dimension_semantics=("arbitrary",), vmem_limit_bytes=VMEM_LIMIT_BYTES),
        name="combine",
    )(pos1, pos2, x2d, wts, ys, gain_final)


def _moe(x2d, gain, wr_hi, wr_lo, later, w_gate, w_up, w_down, layer, gain_final, final_norm):
    n = x2d.shape[0]
    max_tiles = (2 * n) // EXPERT_TILE + N_EXPERTS
    hpk, meta, wts, cnt = _router(x2d, gain, wr_hi, wr_lo, later)
    counts = cnt[:, 0].astype(jnp.int32)
    tiles_per = (counts + (EXPERT_TILE - 1)) // EXPERT_TILE
    tile_end = jnp.cumsum(tiles_per)
    tile_start = tile_end - tiles_per
    row_start = tile_start * EXPERT_TILE
    tile_ids = jnp.arange(max_tiles, dtype=jnp.int32)
    tile_expert = jnp.minimum(
        jnp.sum((tile_ids[:, None] >= tile_end[None, :]).astype(jnp.int32), axis=1), N_EXPERTS - 1)
    n_used = tile_end[-1:].astype(jnp.int32)
    rows_left = counts[tile_expert] - (tile_ids - tile_start[tile_expert]) * EXPERT_TILE
    tile_rows = jnp.where(tile_ids < n_used[0], jnp.clip(rows_left, 0, EXPERT_TILE), 0)
    pos = _positions(meta, jnp.broadcast_to(row_start[:, None], (N_EXPERTS, ROUTER_LANES)))
    sorted_rows = _scatter(pos[0], pos[1], hpk, max_tiles * EXPERT_TILE)
    ys = _experts(tile_expert, tile_rows.astype(jnp.int32), n_used, sorted_rows, w_gate, w_up, w_down,
                  layer)
    return _combine(pos[0], pos[1], x2d, wts, ys, gain_final, final_norm)


def kernel(x, lb_logits, norm_mix, w_in, hgrn_norm, tmlp_ln_g, tmlp_ln_b, w_spatial, b_spatial,
           w_out, norm_ffn, w_router_group, w_router_expert, w_gate, w_up, w_down, norm_final):
    depth = w_in.shape[0]
    bsz, seq, _ = x.shape
    p = jax.nn.softmax(lb_logits.astype(F32), axis=0)
    lower_bounds = jnp.cumsum(p, axis=0) - p[0:1]
    tril = jnp.tril(jnp.ones((SUB, SUB), dtype=bool))
    later = jnp.triu(jnp.ones((ROUTE_ROWS, ROUTE_ROWS), BF16), 1)
    for layer in range(depth):
        w_sp = jnp.where(tril[None], w_spatial[layer], 0.0).astype(BF16)
        b_sp = jnp.repeat(b_spatial[layer].T, GROUP_DIM, axis=1)
        x = _mixer(x, norm_mix[layer][None], w_in[layer].astype(BF16), lower_bounds[layer][None],
                   hgrn_norm[layer][None], tmlp_ln_g[layer][None], tmlp_ln_b[layer][None],
                   w_sp, b_sp, w_out[layer].astype(BF16))
        w_r = jnp.concatenate([w_router_expert[layer], w_router_group[layer]], axis=1)
        w_r = jnp.pad(w_r, ((0, 0), (0, ROUTER_LANES - w_r.shape[1])))
        wr_hi, wr_lo = _split_bf16(w_r)
        x2d = _moe(x.reshape(bsz * seq, D_MODEL), norm_ffn[layer][None], wr_hi, wr_lo, later,
                   w_gate, w_up, w_down, layer, norm_final[None], final_norm=(layer == depth - 1))
        x = x2d.reshape(bsz, seq, D_MODEL)
    return x
```

```python
import functools

import jax
import jax.numpy as jnp
from jax import lax
from jax.experimental import pallas as pl
from jax.experimental.pallas import tpu as pltpu

F32 = jnp.float32
BF16 = jnp.bfloat16

D_MODEL = 1024
N_HEADS = 4
HEAD_DIM = 128
D_HGRN = N_HEADS * HEAD_DIM
N_GROUPS = 4
GROUP_DIM = 128
D_TMLP = N_GROUPS * GROUP_DIM
D_IN = 4 * D_HGRN + 2 * D_TMLP
SUB = 128
N_EXPERT_GROUPS = 4
EXPERTS_PER_GROUP = 8
N_EXPERTS = N_EXPERT_GROUPS * EXPERTS_PER_GROUP
D_EXPERT = 256
ROUTER_LANES = 128
RMS_EPS = 1e-6
LN_EPS = 1e-5
F_FLOOR = 1e-30
HGRN_SAFE_EXP = 60.0
SQRT_HALF = 0.7071067811865476

MIXER_ROWS = 512
PROJ_PIECE = 256
ROUTE_ROWS = 1024
SCATTER_ROWS = 2048
COMBINE_ROWS = 1024
EXPERT_TILE = 1024
PACKED = D_MODEL // 2
DMA_UNROLL = 8
VMEM_LIMIT_BYTES = 56 * 1024 * 1024


def _dot(a, b):
    return jnp.dot(a, b, preferred_element_type=F32)


def _dot_nt(a, b):
    return lax.dot_general(a, b, (((1,), (1,)), ((), ())), preferred_element_type=F32)


def _dot_tn(a, b):
    return lax.dot_general(a, b, (((0,), (0,)), ((), ())), preferred_element_type=F32)


def _split_bf16(a):
    hi = a.astype(BF16)
    lo = (a - hi.astype(F32)).astype(BF16)
    return hi, lo


def _gelu(a):
    return 0.5 * a * (1.0 + lax.erf(a * SQRT_HALF))


def _silu(a):
    return a * jax.nn.sigmoid(a)


def _boundary_rows(b_ref, r0, m, width):
    pieces = []
    if 2 * m >= 8:
        for s0 in range(0, SUB, 2 * m):
            row = b_ref[pl.ds(r0 + (s0 + m - 1), 1), :]
            pieces.append(jnp.broadcast_to(row, (2 * m, width)))
    else:
        row8 = lax.broadcasted_iota(jnp.int32, (8, width), 0)
        for g0 in range(0, SUB, 8):
            acc = None
            for s0 in range(0, 8, 2 * m):
                row = jnp.broadcast_to(b_ref[pl.ds(r0 + (g0 + s0 + m - 1), 1), :], (8, width))
                acc = row if acc is None else jnp.where(row8 >= s0, row, acc)
            pieces.append(acc)
    return jnp.concatenate(pieces, axis=0)


def _half_middle_rows(b_ref, r0, width):
    half = SUB // 2
    return jnp.concatenate(
        [jnp.broadcast_to(b_ref[pl.ds(r0 + (s0 + half // 2 - 1), 1), :], (half, width))
         for s0 in range(0, SUB, half)], axis=0)


def _hgrn_level(att, q, kk, b, b_ref, r0, m, tx, row):
    ref_pt = _boundary_rows(b_ref, r0, m, D_HGRN)
    decay = jnp.exp(-jnp.abs(b - ref_pt))
    right = jnp.bitwise_and(row, m) != 0
    qt = jnp.where(right, q * decay, 0.0).astype(BF16)
    kt = jnp.where(right, 0.0, kk * decay).astype(BF16)
    same_block = tx < 2 * m
    out = []
    for h in range(N_HEADS):
        sl = slice(h * HEAD_DIM, (h + 1) * HEAD_DIM)
        term = jnp.where(same_block, _dot_nt(qt[:, sl], kt[:, sl]), 0.0)
        out.append(term if att is None else att[h] + term)
    return out


def _hgrn_attention(q, kk, b, b_ref, r0, shared_reference):
    t_idx = lax.broadcasted_iota(jnp.int32, (SUB, SUB), 0)
    s_idx = lax.broadcasted_iota(jnp.int32, (SUB, SUB), 1)
    tx = jnp.bitwise_xor(t_idx, s_idx)
    row = lax.broadcasted_iota(jnp.int32, (SUB, D_HGRN), 0)
    half = SUB // 2
    att = _hgrn_level(None, q, kk, b, b_ref, r0, half, tx, row)
    if shared_reference:
        expo = b - _half_middle_rows(b_ref, r0, D_HGRN)
        qt = (q * jnp.exp(expo)).astype(BF16)
        kt = (kk * jnp.exp(-expo)).astype(BF16)
        keep = (tx < half) & (s_idx <= t_idx)
        for h in range(N_HEADS):
            sl = slice(h * HEAD_DIM, (h + 1) * HEAD_DIM)
            att[h] = att[h] + jnp.where(keep, _dot_nt(qt[:, sl], kt[:, sl]), 0.0)
        return att
    qb = q.astype(BF16)
    kb = kk.astype(BF16)
    for h in range(N_HEADS):
        sl = slice(h * HEAD_DIM, (h + 1) * HEAD_DIM)
        att[h] = att[h] + jnp.where(tx == 0, _dot_nt(qb[:, sl], kb[:, sl]), 0.0)
    m = 1
    while m < half:
        att = _hgrn_level(att, q, kk, b, b_ref, r0, m, tx, row)
        m *= 2
    return att


def _hgrn_sub_chunk(z_ref, kk_ref, b_ref, hn_ref, y_ref, r0, state, shared_reference, per_head=None):
    rows = pl.ds(r0, SUB)
    q = _silu(z_ref[rows, 0:D_HGRN])
    kk = kk_ref[rows, :]
    b = b_ref[rows, :]
    v = z_ref[rows, 2 * D_HGRN:3 * D_HGRN].astype(BF16)
    att = _hgrn_attention(q, kk, b, b_ref, r0, shared_reference)
    b_end = b_ref[pl.ds(r0 + (SUB - 1), 1), :]
    q0 = (q * jnp.exp(b)).astype(BF16)
    k_end = (kk * jnp.exp(b_end - b)).astype(BF16)
    s_decay = jnp.exp(b_end)
    g = _silu(z_ref[rows, 3 * D_HGRN:4 * D_HGRN])
    new_state = []
    for hd in range(N_HEADS):
        if per_head is not None:
            per_head(hd)
        sl = slice(hd * HEAD_DIM, (hd + 1) * HEAD_DIM)
        st = state[hd]
        o = _dot(att[hd].astype(BF16), v[:, sl]) + _dot_nt(q0[:, sl], st.astype(BF16))
        new_state.append(st * s_decay[:, sl] + _dot_tn(v[:, sl], k_end[:, sl]))
        oms = jnp.mean(o * o, axis=-1, keepdims=True)
        on = (o * lax.rsqrt(oms + RMS_EPS)) * hn_ref[...]
        y_ref[rows, sl] = (on * g[:, sl]).astype(BF16)
    return new_state


def _mixer_tile(z_ref, zn_ref, h_next, win_ref, lb_ref, hn_ref, lng_ref, lnb_ref, wsp_ref, bsp_ref,
                y_ref, b_ref, kk_ref, st_ref):
    def project(first, count):
        for p in range(first, first + count):
            cols = slice(p * PROJ_PIECE, (p + 1) * PROJ_PIECE)
            zn_ref[:, cols] = _dot(h_next, win_ref[:, cols])

    n_pieces = D_IN // PROJ_PIECE
    early = n_pieces // 3

    t_idx = lax.broadcasted_iota(jnp.int32, (SUB, SUB), 0)
    s_idx = lax.broadcasted_iota(jnp.int32, (SUB, SUB), 1)
    tri = (s_idx <= t_idx).astype(BF16)
    n_sub = MIXER_ROWS // SUB

    worst = jnp.zeros((SUB, D_HGRN), F32)
    for c in range(n_sub):
        rows = pl.ds(c * SUB, SUB)
        lb = lb_ref[...]
        fg = lb + (1.0 - lb) * jax.nn.sigmoid(z_ref[rows, D_HGRN:2 * D_HGRN])
        lf_hi, lf_lo = _split_bf16(jnp.log(jnp.maximum(fg, F_FLOOR)))
        kk_ref[rows, :] = 1.0 - fg
        b = _dot(tri, lf_hi) + _dot(tri, lf_lo)
        b_ref[rows, :] = b
        worst = jnp.maximum(worst, jnp.abs(b - _half_middle_rows(b_ref, c * SUB, D_HGRN)))
    shared_ok = jnp.max(worst) <= HGRN_SAFE_EXP

    for c in range(n_sub):
        project(c * early // n_sub, (c + 1) * early // n_sub - c * early // n_sub)
        rows = pl.ds(c * SUB, SUB)
        u = _gelu(z_ref[rows, 4 * D_HGRN:4 * D_HGRN + D_TMLP])
        vv = _gelu(z_ref[rows, 4 * D_HGRN + D_TMLP:D_IN])
        for gi in range(N_GROUPS):
            sl = slice(gi * GROUP_DIM, (gi + 1) * GROUP_DIM)
            vg = vv[:, sl]
            mu = jnp.mean(vg, axis=-1, keepdims=True)
            cen = vg - mu
            var = jnp.mean(cen * cen, axis=-1, keepdims=True)
            vn = (cen * lax.rsqrt(var + LN_EPS)) * lng_ref[:, sl] + lnb_ref[:, sl]
            mixed = _dot(wsp_ref[gi], vn.astype(BF16)) + bsp_ref[:, sl]
            y_ref[rows, D_HGRN + gi * GROUP_DIM:D_HGRN + (gi + 1) * GROUP_DIM] = (
                u[:, sl] * mixed).astype(BF16)

    late = n_pieces - early

    @pl.when(shared_ok)
    def _():
        state = [st_ref[hd] for hd in range(N_HEADS)]
        stride = (n_sub * N_HEADS) // late

        def piece_for_slot(slot):
            if slot % stride == 0:
                project(early + slot // stride, 1)

        for c in range(n_sub):
            state = _hgrn_sub_chunk(z_ref, kk_ref, b_ref, hn_ref, y_ref, c * SUB, state, True,
                                    lambda hd, c=c: piece_for_slot(c * N_HEADS + hd))
        for hd in range(N_HEADS):
            st_ref[hd] = state[hd]

    @pl.when(jnp.logical_not(shared_ok))
    def _():
        project(early, late)

        def sub_chunk(c, carry):
            r0 = pl.multiple_of(c * SUB, SUB)
            state = [st_ref[hd] for hd in range(N_HEADS)]
            state = _hgrn_sub_chunk(z_ref, kk_ref, b_ref, hn_ref, y_ref, r0, state, False)
            for hd in range(N_HEADS):
                st_ref[hd] = state[hd]
            return carry

        lax.fori_loop(0, n_sub, sub_chunk, 0)


def _mixer_kernel(x_ref, xn_ref, gain_ref, win_ref, lb_ref, hn_ref, lng_ref, lnb_ref, wsp_ref, bsp_ref,
                  wout_ref, o_ref, za_ref, zb_ref, y_ref, b_ref, kk_ref, st_ref):
    def normed(xv):
        ms = jnp.mean(xv * xv, axis=-1, keepdims=True)
        return ((xv * lax.rsqrt(ms + RMS_EPS)) * gain_ref[...]).astype(BF16)

    @pl.when(pl.program_id(1) == 0)
    def _():
        st_ref[...] = jnp.zeros_like(st_ref)

    @pl.when((pl.program_id(0) == 0) & (pl.program_id(1) == 0))
    def _():
        za_ref[...] = _dot(normed(x_ref[0, 0:MIXER_ROWS, :]), win_ref[...])

    tile_refs = (win_ref, lb_ref, hn_ref, lng_ref, lnb_ref, wsp_ref, bsp_ref)
    _mixer_tile(za_ref, zb_ref, normed(x_ref[0, MIXER_ROWS:2 * MIXER_ROWS, :]), *tile_refs,
                y_ref, b_ref, kk_ref, st_ref)
    o_ref[0, 0:MIXER_ROWS, :] = x_ref[0, 0:MIXER_ROWS, :] + _dot(y_ref[...], wout_ref[...])
    _mixer_tile(zb_ref, za_ref, normed(xn_ref[0]), *tile_refs, y_ref, b_ref, kk_ref, st_ref)
    o_ref[0, MIXER_ROWS:2 * MIXER_ROWS, :] = (
        x_ref[0, MIXER_ROWS:2 * MIXER_ROWS, :] + _dot(y_ref[...], wout_ref[...]))


def _mixer(x, gain, w_in, lb, hgrn_norm, ln_g, ln_b, w_sp, b_sp, w_out):
    bsz, seq, _ = x.shape
    step_rows = 2 * MIXER_ROWS
    assert seq % step_rows == 0 and MIXER_ROWS % SUB == 0
    steps = seq // step_rows
    const2 = lambda b, j: (0, 0)
    const3 = lambda b, j: (0, 0, 0)
    once = pl.Buffered(1)

    def next_tile_a(b, j):
        nxt = jnp.minimum(b * steps + j + 1, bsz * steps - 1)
        return (nxt // steps, 2 * (nxt % steps), 0)

    return pl.pallas_call(
        _mixer_kernel,
        grid=(bsz, steps),
        in_specs=[
            pl.BlockSpec((1, step_rows, D_MODEL), lambda b, j: (b, j, 0)),
            pl.BlockSpec((1, MIXER_ROWS, D_MODEL), next_tile_a),
            pl.BlockSpec((1, D_MODEL), const2),
            pl.BlockSpec((D_MODEL, D_IN), const2, pipeline_mode=once),
            pl.BlockSpec((1, D_HGRN), const2),
            pl.BlockSpec((1, HEAD_DIM), const2),
            pl.BlockSpec((1, D_TMLP), const2),
            pl.BlockSpec((1, D_TMLP), const2),
            pl.BlockSpec((N_GROUPS, SUB, SUB), const3),
            pl.BlockSpec((SUB, D_TMLP), const2),
            pl.BlockSpec((D_HGRN + D_TMLP, D_MODEL), const2, pipeline_mode=once),
        ],
        out_specs=pl.BlockSpec((1, step_rows, D_MODEL), lambda b, j: (b, j, 0)),
        out_shape=jax.ShapeDtypeStruct(x.shape, F32),
        scratch_shapes=[
            pltpu.VMEM((MIXER_ROWS, D_IN), F32),
            pltpu.VMEM((MIXER_ROWS, D_IN), F32),
            pltpu.VMEM((MIXER_ROWS, D_HGRN + D_TMLP), BF16),
            pltpu.VMEM((MIXER_ROWS, D_HGRN), F32),
            pltpu.VMEM((MIXER_ROWS, D_HGRN), F32),
            pltpu.VMEM((N_HEADS, HEAD_DIM, HEAD_DIM), F32),
        ],
        compiler_params=pltpu.CompilerParams(
            dimension_semantics=("arbitrary", "arbitrary"),
            vmem_limit_bytes=VMEM_LIMIT_BYTES),
        name="mixer",
    )(x, x, gain, w_in, lb, hgrn_norm, ln_g, ln_b, w_sp, b_sp, w_out)


def _pack_bf16_pairs(a):
    lo = lax.bitcast_convert_type(a[:, :PACKED].astype(BF16).astype(F32), jnp.uint32)
    hi = lax.bitcast_convert_type(a[:, PACKED:].astype(BF16).astype(F32), jnp.uint32)
    return jnp.bitwise_or(lax.shift_right_logical(lo, jnp.uint32(16)),
                          jnp.bitwise_and(hi, jnp.uint32(0xFFFF0000)))


def _unpack_bf16_pairs(u):
    lo = lax.bitcast_convert_type(lax.shift_left(u, jnp.uint32(16)), F32)
    hi = lax.bitcast_convert_type(jnp.bitwise_and(u, jnp.uint32(0xFFFF0000)), F32)
    return lo, hi


def _router_kernel(x_ref, gain_ref, wr_hi_ref, wr_lo_ref, later_ref, hpk_ref, meta_ref, wts_ref,
                   cnt_out_ref, cnt_ref):
    @pl.when(pl.program_id(0) == 0)
    def _():
        cnt_ref[...] = jnp.zeros_like(cnt_ref)

    x = x_ref[...]
    ms = jnp.mean(x * x, axis=-1, keepdims=True)
    h = (x * lax.rsqrt(ms + RMS_EPS)) * gain_ref[...]
    hpk_ref[...] = _pack_bf16_pairs(h)
    h_hi, h_lo = _split_bf16(h)
    logits = _dot(h_hi, wr_hi_ref[...]) + (_dot(h_hi, wr_lo_ref[...]) + _dot(h_lo, wr_hi_ref[...]))
    lt = logits.T
    sub = lax.broadcasted_iota(jnp.int32, (EXPERTS_PER_GROUP, ROUTE_ROWS), 0)
    neg = jnp.float32(-jnp.inf)
    big = jnp.int32(1 << 20)
    gl = jnp.where(sub < N_EXPERT_GROUPS, lt[N_EXPERTS:N_EXPERTS + EXPERTS_PER_GROUP], neg)
    gmax = jnp.max(gl, axis=0, keepdims=True)
    p_sel = 1.0 / jnp.sum(jnp.exp(gl - gmax), axis=0, keepdims=True)
    g_idx = jnp.min(jnp.where(gl == gmax, sub, big), axis=0, keepdims=True)
    el = lt[(N_EXPERT_GROUPS - 1) * EXPERTS_PER_GROUP:N_EXPERTS]
    for g in range(N_EXPERT_GROUPS - 2, -1, -1):
        el = jnp.where(g_idx == g, lt[g * EXPERTS_PER_GROUP:(g + 1) * EXPERTS_PER_GROUP], el)
    v1 = jnp.max(el, axis=0, keepdims=True)
    i1 = jnp.min(jnp.where(el == v1, sub, big), axis=0, keepdims=True)
    el2 = jnp.where(sub == i1, neg, el)
    v2 = jnp.max(el2, axis=0, keepdims=True)
    i2 = jnp.min(jnp.where(el2 == v2, sub, big), axis=0, keepdims=True)
    e2x = jnp.exp(v2 - v1)
    w1 = p_sel / (1.0 + e2x)
    w2 = p_sel * e2x / (1.0 + e2x)
    e1 = g_idx * EXPERTS_PER_GROUP + i1
    e2 = g_idx * EXPERTS_PER_GROUP + i2

    expert = lax.broadcasted_iota(jnp.int32, (N_EXPERTS, ROUTE_ROWS), 0)
    assigned = ((expert == e1) | (expert == e2)).astype(F32)
    earlier = cnt_ref[:, 0:1] + _dot(assigned.astype(BF16), later_ref[...])
    rank1 = jnp.sum(jnp.where(expert == e1, earlier, 0.0), axis=0, keepdims=True).astype(jnp.int32)
    rank2 = jnp.sum(jnp.where(expert == e2, earlier, 0.0), axis=0, keepdims=True).astype(jnp.int32)
    meta_ref[...] = jnp.where(sub == 0, e1, jnp.where(sub == 1, e2, jnp.where(
        sub == 2, rank1, jnp.where(sub == 3, rank2, 0))))
    cnt_ref[...] = cnt_ref[...] + jnp.sum(assigned, axis=1, keepdims=True)
    cnt_out_ref[...] = cnt_ref[...]

    w_rows = jnp.where(sub == 0, w1, jnp.where(sub == 1, w2, 0.0))
    pad_w = jnp.zeros((ROUTER_LANES - EXPERTS_PER_GROUP, ROUTE_ROWS), F32)
    wts_ref[...] = jnp.concatenate([w_rows, pad_w], axis=0).T


def _router(x2d, gain, wr_hi, wr_lo, later):
    n = x2d.shape[0]
    assert n % ROUTE_ROWS == 0
    const2 = lambda i: (0, 0)
    rows = lambda i: (i, 0)
    return pl.pallas_call(
        _router_kernel,
        grid=(n // ROUTE_ROWS,),
        in_specs=[
            pl.BlockSpec((ROUTE_ROWS, D_MODEL), rows),
            pl.BlockSpec((1, D_MODEL), const2),
            pl.BlockSpec((D_MODEL, ROUTER_LANES), const2),
            pl.BlockSpec((D_MODEL, ROUTER_LANES), const2),
            pl.BlockSpec((ROUTE_ROWS, ROUTE_ROWS), const2),
        ],
        out_specs=[
            pl.BlockSpec((ROUTE_ROWS, PACKED), rows),
            pl.BlockSpec((EXPERTS_PER_GROUP, ROUTE_ROWS), lambda i: (0, i)),
            pl.BlockSpec((ROUTE_ROWS, ROUTER_LANES), rows),
            pl.BlockSpec((N_EXPERTS, ROUTER_LANES), const2),
        ],
        out_shape=[
            jax.ShapeDtypeStruct((n, PACKED), jnp.uint32),
            jax.ShapeDtypeStruct((EXPERTS_PER_GROUP, n), jnp.int32),
            jax.ShapeDtypeStruct((n, ROUTER_LANES), F32),
            jax.ShapeDtypeStruct((N_EXPERTS, ROUTER_LANES), F32),
        ],
        scratch_shapes=[pltpu.VMEM((N_EXPERTS, ROUTER_LANES), F32)],
        compiler_params=pltpu.CompilerParams(
            dimension_semantics=("arbitrary",), vmem_limit_bytes=VMEM_LIMIT_BYTES),
        name="router",
    )(x2d, gain, wr_hi, wr_lo, later)


def _positions_kernel(meta_ref, row_start_ref, pos_ref):
    meta = meta_ref[...]
    expert = lax.broadcasted_iota(jnp.int32, (N_EXPERTS, ROUTE_ROWS), 0)
    starts = row_start_ref[:, 0:1]
    base1 = jnp.sum(jnp.where(expert == meta[0:1], starts, 0), axis=0, keepdims=True)
    base2 = jnp.sum(jnp.where(expert == meta[1:2], starts, 0), axis=0, keepdims=True)
    sub = lax.broadcasted_iota(jnp.int32, meta.shape, 0)
    pos_ref[...] = jnp.where(sub == 0, base1 + meta[2:3], jnp.where(sub == 1, base2 + meta[3:4], 0))


def _positions(meta, row_start_lanes):
    n = meta.shape[1]
    return pl.pallas_call(
        _positions_kernel,
        grid=(n // ROUTE_ROWS,),
        in_specs=[pl.BlockSpec((EXPERTS_PER_GROUP, ROUTE_ROWS), lambda i: (0, i)),
                  pl.BlockSpec((N_EXPERTS, ROUTER_LANES), lambda i: (0, 0))],
        out_specs=pl.BlockSpec((EXPERTS_PER_GROUP, ROUTE_ROWS), lambda i: (0, i)),
        out_shape=jax.ShapeDtypeStruct(meta.shape, jnp.int32),
        compiler_params=pltpu.CompilerParams(
            dimension_semantics=("arbitrary",), vmem_limit_bytes=VMEM_LIMIT_BYTES),
        name="positions",
    )(meta, row_start_lanes)


def _row_copy(src_ref, src_row, dst_ref, dst_row, sem):
    return pltpu.make_async_copy(src_ref.at[pl.ds(src_row, 1)], dst_ref.at[pl.ds(dst_row, 1)], sem)


def _scatter_kernel(pos1_ref, pos2_ref, hpk_ref, sorted_ref, sem):
    def issue(i, carry):
        _row_copy(hpk_ref, i, sorted_ref, pos1_ref[i], sem).start(priority=0)
        _row_copy(hpk_ref, i, sorted_ref, pos2_ref[i], sem).start(priority=1)
        return carry

    lax.fori_loop(0, SCATTER_ROWS, issue, 0, unroll=DMA_UNROLL)

    def drain(i, carry):
        _row_copy(hpk_ref, 0, sorted_ref, 0, sem).wait()
        _row_copy(hpk_ref, 0, sorted_ref, 0, sem).wait()
        return carry

    lax.fori_loop(0, SCATTER_ROWS, drain, 0, unroll=DMA_UNROLL)


def _scatter(pos1, pos2, hpk, n_rows):
    n = hpk.shape[0]
    assert n % SCATTER_ROWS == 0
    idx = lambda i: (i,)
    return pl.pallas_call(
        _scatter_kernel,
        grid=(n // SCATTER_ROWS,),
        in_specs=[
            pl.BlockSpec((SCATTER_ROWS,), idx, memory_space=pltpu.SMEM),
            pl.BlockSpec((SCATTER_ROWS,), idx, memory_space=pltpu.SMEM),
            pl.BlockSpec((SCATTER_ROWS, PACKED), lambda i: (i, 0)),
        ],
        out_specs=pl.BlockSpec(memory_space=pl.ANY),
        out_shape=jax.ShapeDtypeStruct((n_rows, PACKED), jnp.uint32),
        scratch_shapes=[pltpu.SemaphoreType.DMA],
        compiler_params=pltpu.CompilerParams(
            dimension_semantics=("arbitrary",), vmem_limit_bytes=VMEM_LIMIT_BYTES),
        name="scatter_rows",
    )(pos1, pos2, hpk)


def _expert_kernel(tile_expert_ref, tile_rows_ref, n_used_ref, lhs_ref, wg_ref, wu_ref, wd_ref,
                   o_ref):
    del tile_expert_ref, n_used_ref
    n_valid = tile_rows_ref[pl.program_id(0)]

    @pl.when(n_valid > 0)
    def _():
        valid = lax.broadcasted_iota(jnp.int32, (EXPERT_TILE, 1), 0) < n_valid
        lo, hi = _unpack_bf16_pairs(lhs_ref[...])
        hh = jnp.concatenate([jnp.where(valid, lo, 0.0).astype(BF16),
                              jnp.where(valid, hi, 0.0).astype(BF16)], axis=1)
        w_gu = jnp.concatenate([wg_ref[0, 0].astype(BF16), wu_ref[0, 0].astype(BF16)], axis=1)
        gu = _dot(hh, w_gu)
        hid = _silu(gu[:, :D_EXPERT]) * gu[:, D_EXPERT:]
        o_ref[...] = _pack_bf16_pairs(_dot(hid.astype(BF16), wd_ref[0, 0].astype(BF16)))

    @pl.when(n_valid <= 0)
    def _():
        o_ref[...] = jnp.zeros_like(o_ref)


def _experts(tile_expert, tile_rows, n_used, sorted_rows, w_gate, w_up, w_down, layer):
    n_tiles = sorted_rows.shape[0] // EXPERT_TILE
    live = lambda i, nu: jnp.minimum(i, nu[0] - 1)
    expert = lambda i, te, tr, nu: (layer, te[live(i, nu)], 0, 0)
    return pl.pallas_call(
        _expert_kernel,
        grid_spec=pltpu.PrefetchScalarGridSpec(
            num_scalar_prefetch=3,
            grid=(n_tiles,),
            in_specs=[
                pl.BlockSpec((EXPERT_TILE, PACKED), lambda i, te, tr, nu: (live(i, nu), 0)),
                pl.BlockSpec((1, 1, D_MODEL, D_EXPERT), expert),
                pl.BlockSpec((1, 1, D_MODEL, D_EXPERT), expert),
                pl.BlockSpec((1, 1, D_EXPERT, D_MODEL), expert),
            ],
            out_specs=pl.BlockSpec((EXPERT_TILE, PACKED), lambda i, te, tr, nu: (i, 0)),
        ),
        out_shape=jax.ShapeDtypeStruct(sorted_rows.shape, jnp.uint32),
        compiler_params=pltpu.CompilerParams(
            dimension_semantics=("arbitrary",), vmem_limit_bytes=VMEM_LIMIT_BYTES),
        name="experts",
    )(tile_expert, tile_rows, n_used, sorted_rows, w_gate, w_up, w_down)


def _combine_kernel(pos1_ref, pos2_ref, x_ref, wts_ref, ys_ref, gfin_ref, o_ref, buf1_ref, buf2_ref,
                    sem, *, final_norm):
    def issue(i, carry):
        _row_copy(ys_ref, pos1_ref[i], buf1_ref, i, sem).start(priority=0)
        _row_copy(ys_ref, pos2_ref[i], buf2_ref, i, sem).start(priority=1)
        return carry

    lax.fori_loop(0, COMBINE_ROWS, issue, 0, unroll=DMA_UNROLL)

    def drain(i, carry):
        _row_copy(ys_ref, 0, buf1_ref, 0, sem).wait()
        _row_copy(ys_ref, 0, buf2_ref, 0, sem).wait()
        return carry

    lax.fori_loop(0, COMBINE_ROWS, drain, 0, unroll=DMA_UNROLL)
    lo1, hi1 = _unpack_bf16_pairs(buf1_ref[...])
    lo2, hi2 = _unpack_bf16_pairs(buf2_ref[...])
    w1 = wts_ref[:, 0:1]
    w2 = wts_ref[:, 1:2]
    x = x_ref[...]
    y = jnp.concatenate([x[:, :PACKED] + (w1 * lo1 + w2 * lo2),
                         x[:, PACKED:] + (w1 * hi1 + w2 * hi2)], axis=1)
    if final_norm:
        ms = jnp.mean(y * y, axis=-1, keepdims=True)
        y = (y * lax.rsqrt(ms + RMS_EPS)) * gfin_ref[...]
    o_ref[...] = y


def _combine(pos1, pos2, x2d, wts, ys, gain_final, final_norm):
    n = x2d.shape[0]
    assert n % COMBINE_ROWS == 0
    idx = lambda i: (i,)
    rows = lambda i: (i, 0)
    return pl.pallas_call(
        functools.partial(_combine_kernel, final_norm=final_norm),
        grid=(n // COMBINE_ROWS,),
        in_specs=[
            pl.BlockSpec((COMBINE_ROWS,), idx, memory_space=pltpu.SMEM),
            pl.BlockSpec((COMBINE_ROWS,), idx, memory_space=pltpu.SMEM),
            pl.BlockSpec((COMBINE_ROWS, D_MODEL), rows),
            pl.BlockSpec((COMBINE_ROWS, ROUTER_LANES), rows),
            pl.BlockSpec(memory_space=pl.ANY),
            pl.BlockSpec((1, D_MODEL), lambda i: (0, 0)),
        ],
        out_specs=pl.BlockSpec((COMBINE_ROWS, D_MODEL), rows),
        out_shape=jax.ShapeDtypeStruct(x2d.shape, F32),
        scratch_shapes=[
            pltpu.VMEM((COMBINE_ROWS, PACKED), jnp.uint32),
            pltpu.VMEM((COMBINE_ROWS, PACKED), jnp.uint32),
            pltpu.SemaphoreType.DMA,
        ],
        compiler_params=pltpu.CompilerParams(
            dimension_semantics=("arbitrary",), vmem_limit_bytes=VMEM_LIMIT_BYTES),
        name="combine",
    )(pos1, pos2, x2d, wts, ys, gain_final)


def _moe(x2d, gain, wr_hi, wr_lo, later, w_gate, w_up, w_down, layer, gain_final, final_norm):
    n = x2d.shape[0]
    max_tiles = (2 * n) // EXPERT_TILE + N_EXPERTS
    hpk, meta, wts, cnt = _router(x2d, gain, wr_hi, wr_lo, later)
    counts = cnt[:, 0].astype(jnp.int32)
    tiles_per = (counts + (EXPERT_TILE - 1)) // EXPERT_TILE
    tile_end = jnp.cumsum(tiles_per)
    tile_start = tile_end - tiles_per
    row_start = tile_start * EXPERT_TILE
    tile_ids = jnp.arange(max_tiles, dtype=jnp.int32)
    tile_expert = jnp.minimum(
        jnp.sum((tile_ids[:, None] >= tile_end[None, :]).astype(jnp.int32), axis=1), N_EXPERTS - 1)
    n_used = tile_end[-1:].astype(jnp.int32)
    rows_left = counts[tile_expert] - (tile_ids - tile_start[tile_expert]) * EXPERT_TILE
    tile_rows = jnp.where(tile_ids < n_used[0], jnp.clip(rows_left, 0, EXPERT_TILE), 0)
    pos = _positions(meta, jnp.broadcast_to(row_start[:, None], (N_EXPERTS, ROUTER_LANES)))
    sorted_rows = _scatter(pos[0], pos[1], hpk, max_tiles * EXPERT_TILE)
    ys = _experts(tile_expert, tile_rows.astype(jnp.int32), n_used, sorted_rows, w_gate, w_up, w_down,
                  layer)
    return _combine(pos[0], pos[1], x2d, wts, ys, gain_final, final_norm)


def kernel(x, lb_logits, norm_mix, w_in, hgrn_norm, tmlp_ln_g, tmlp_ln_b, w_spatial, b_spatial,
           w_out, norm_ffn, w_router_group, w_router_expert, w_gate, w_up, w_down, norm_final):
    depth = w_in.shape[0]
    bsz, seq, _ = x.shape
    p = jax.nn.softmax(lb_logits.astype(F32), axis=0)
    lower_bounds = jnp.cumsum(p, axis=0) - p[0:1]
    tril = jnp.tril(jnp.ones((SUB, SUB), dtype=bool))
    later = jnp.triu(jnp.ones((ROUTE_ROWS, ROUTE_ROWS), BF16), 1)
    for layer in range(depth):
        w_sp = jnp.where(tril[None], w_spatial[layer], 0.0).astype(BF16)
        b_sp = jnp.repeat(b_spatial[layer].T, GROUP_DIM, axis=1)
        x = _mixer(x, norm_mix[layer][None], w_in[layer].astype(BF16), lower_bounds[layer][None],
                   hgrn_norm[layer][None], tmlp_ln_g[layer][None], tmlp_ln_b[layer][None],
                   w_sp, b_sp, w_out[layer].astype(BF16))
        w_r = jnp.concatenate([w_router_expert[layer], w_router_group[layer]], axis=1)
        w_r = jnp.pad(w_r, ((0, 0), (0, ROUTER_LANES - w_r.shape[1])))
        wr_hi, wr_lo = _split_bf16(w_r)
        x2d = _moe(x.reshape(bsz * seq, D_MODEL), norm_ffn[layer][None], wr_hi, wr_lo, later,
                   w_gate, w_up, w_down, layer, norm_final[None], final_norm=(layer == depth - 1))
        x = x2d.reshape(bsz, seq, D_MODEL)
    return x
```

```python
import functools

import jax
import jax.numpy as jnp
from jax import lax
from jax.experimental import pallas as pl
from jax.experimental.pallas import tpu as pltpu

F32 = jnp.float32
BF16 = jnp.bfloat16

D_MODEL = 1024
N_HEADS = 4
HEAD_DIM = 128
D_HGRN = N_HEADS * HEAD_DIM
N_GROUPS = 4
GROUP_DIM = 128
D_TMLP = N_GROUPS * GROUP_DIM
D_IN = 4 * D_HGRN + 2 * D_TMLP
SUB = 128
N_EXPERT_GROUPS = 4
EXPERTS_PER_GROUP = 8
N_EXPERTS = N_EXPERT_GROUPS * EXPERTS_PER_GROUP
D_EXPERT = 256
ROUTER_LANES = 128
RMS_EPS = 1e-6
LN_EPS = 1e-5
F_FLOOR = 1e-30
HGRN_SAFE_EXP = 60.0
SQRT_HALF = 0.7071067811865476

MIXER_ROWS = 512
PROJ_PIECE = 256
ROUTE_ROWS = 1024
SCATTER_ROWS = 2048
COMBINE_ROWS = 1024
EXPERT_TILE = 1024
PACKED = D_MODEL // 2
DMA_UNROLL = 8
VMEM_LIMIT_BYTES = 56 * 1024 * 1024


def _dot(a, b):
    return jnp.dot(a, b, preferred_element_type=F32)


def _dot_nt(a, b):
    return lax.dot_general(a, b, (((1,), (1,)), ((), ())), preferred_element_type=F32)


def _dot_tn(a, b):
    return lax.dot_general(a, b, (((0,), (0,)), ((), ())), preferred_element_type=F32)


def _split_bf16(a):
    hi = a.astype(BF16)
    lo = (a - hi.astype(F32)).astype(BF16)
    return hi, lo


def _gelu(a):
    return 0.5 * a * (1.0 + lax.erf(a * SQRT_HALF))


def _silu(a):
    return a * jax.nn.sigmoid(a)


def _boundary_rows(b_ref, r0, m, width):
    pieces = []
    if 2 * m >= 8:
        for s0 in range(0, SUB, 2 * m):
            row = b_ref[pl.ds(r0 + (s0 + m - 1), 1), :]
            pieces.append(jnp.broadcast_to(row, (2 * m, width)))
    else:
        row8 = lax.broadcasted_iota(jnp.int32, (8, width), 0)
        for g0 in range(0, SUB, 8):
            acc = None
            for s0 in range(0, 8, 2 * m):
                row = jnp.broadcast_to(b_ref[pl.ds(r0 + (g0 + s0 + m - 1), 1), :], (8, width))
                acc = row if acc is None else jnp.where(row8 >= s0, row, acc)
            pieces.append(acc)
    return jnp.concatenate(pieces, axis=0)


def _half_middle_rows(b_ref, r0, width):
    half = SUB // 2
    return jnp.concatenate(
        [jnp.broadcast_to(b_ref[pl.ds(r0 + (s0 + half // 2 - 1), 1), :], (half, width))
         for s0 in range(0, SUB, half)], axis=0)


def _hgrn_level(att, q, kk, b, b_ref, r0, m, tx, row):
    ref_pt = _boundary_rows(b_ref, r0, m, D_HGRN)
    decay = jnp.exp(-jnp.abs(b - ref_pt))
    right = jnp.bitwise_and(row, m) != 0
    qt = jnp.where(right, q * decay, 0.0).astype(BF16)
    kt = jnp.where(right, 0.0, kk * decay).astype(BF16)
    same_block = tx < 2 * m
    out = []
    for h in range(N_HEADS):
        sl = slice(h * HEAD_DIM, (h + 1) * HEAD_DIM)
        term = jnp.where(same_block, _dot_nt(qt[:, sl], kt[:, sl]), 0.0)
        out.append(term if att is None else att[h] + term)
    return out


def _hgrn_attention(q, kk, b, b_ref, r0, shared_reference):
    t_idx = lax.broadcasted_iota(jnp.int32, (SUB, SUB), 0)
    s_idx = lax.broadcasted_iota(jnp.int32, (SUB, SUB), 1)
    tx = jnp.bitwise_xor(t_idx, s_idx)
    row = lax.broadcasted_iota(jnp.int32, (SUB, D_HGRN), 0)
    half = SUB // 2
    att = _hgrn_level(None, q, kk, b, b_ref, r0, half, tx, row)
    if shared_reference:
        expo = b - _half_middle_rows(b_ref, r0, D_HGRN)
        qt = (q * jnp.exp(expo)).astype(BF16)
        kt = (kk * jnp.exp(-expo)).astype(BF16)
        keep = (tx < half) & (s_idx <= t_idx)
        for h in range(N_HEADS):
            sl = slice(h * HEAD_DIM, (h + 1) * HEAD_DIM)
            att[h] = att[h] + jnp.where(keep, _dot_nt(qt[:, sl], kt[:, sl]), 0.0)
        return att
    qb = q.astype(BF16)
    kb = kk.astype(BF16)
    for h in range(N_HEADS):
        sl = slice(h * HEAD_DIM, (h + 1) * HEAD_DIM)
        att[h] = att[h] + jnp.where(tx == 0, _dot_nt(qb[:, sl], kb[:, sl]), 0.0)
    m = 1
    while m < half:
        att = _hgrn_level(att, q, kk, b, b_ref, r0, m, tx, row)
        m *= 2
    return att


def _hgrn_sub_chunk(z_ref, kk_ref, b_ref, hn_ref, y_ref, r0, state, shared_reference, per_head=None):
    rows = pl.ds(r0, SUB)
    q = _silu(z_ref[rows, 0:D_HGRN])
    kk = kk_ref[rows, :]
    b = b_ref[rows, :]
    v = z_ref[rows, 2 * D_HGRN:3 * D_HGRN].astype(BF16)
    att = _hgrn_attention(q, kk, b, b_ref, r0, shared_reference)
    b_end = b_ref[pl.ds(r0 + (SUB - 1), 1), :]
    q0 = (q * jnp.exp(b)).astype(BF16)
    k_end = (kk * jnp.exp(b_end - b)).astype(BF16)
    s_decay = jnp.exp(b_end)
    g = _silu(z_ref[rows, 3 * D_HGRN:4 * D_HGRN])
    new_state = []
    for hd in range(N_HEADS):
        if per_head is not None:
            per_head(hd)
        sl = slice(hd * HEAD_DIM, (hd + 1) * HEAD_DIM)
        st = state[hd]
        o = _dot(att[hd].astype(BF16), v[:, sl]) + _dot_nt(q0[:, sl], st.astype(BF16))
        new_state.append(st * s_decay[:, sl] + _dot_tn(v[:, sl], k_end[:, sl]))
        oms = jnp.mean(o * o, axis=-1, keepdims=True)
        on = (o * lax.rsqrt(oms + RMS_EPS)) * hn_ref[...]
        y_ref[rows, sl] = (on * g[:, sl]).astype(BF16)
    return new_state


def _mixer_tile(z_ref, zn_ref, h_next, win_ref, lb_ref, hn_ref, lng_ref, lnb_ref, wsp_ref, bsp_ref,
                y_ref, b_ref, kk_ref, st_ref):
    def project(first, count):
        for p in range(first, first + count):
            cols = slice(p * PROJ_PIECE, (p + 1) * PROJ_PIECE)
            zn_ref[:, cols] = _dot(h_next, win_ref[:, cols])

    n_pieces = D_IN // PROJ_PIECE
    early = n_pieces // 3

    t_idx = lax.broadcasted_iota(jnp.int32, (SUB, SUB), 0)
    s_idx = lax.broadcasted_iota(jnp.int32, (SUB, SUB), 1)
    tri = (s_idx <= t_idx).astype(BF16)
    n_sub = MIXER_ROWS // SUB

    worst = jnp.zeros((SUB, D_HGRN), F32)
    for c in range(n_sub):
        rows = pl.ds(c * SUB, SUB)
        lb = lb_ref[...]
        fg = lb + (1.0 - lb) * jax.nn.sigmoid(z_ref[rows, D_HGRN:2 * D_HGRN])
        lf_hi, lf_lo = _split_bf16(jnp.log(jnp.maximum(fg, F_FLOOR)))
        kk_ref[rows, :] = 1.0 - fg
        b = _dot(tri, lf_hi) + _dot(tri, lf_lo)
        b_ref[rows, :] = b
        worst = jnp.maximum(worst, jnp.abs(b - _half_middle_rows(b_ref, c * SUB, D_HGRN)))
    shared_ok = jnp.max(worst) <= HGRN_SAFE_EXP

    for c in range(n_sub):
        project(c * early // n_sub, (c + 1) * early // n_sub - c * early // n_sub)
        rows = pl.ds(c * SUB, SUB)
        u = _gelu(z_ref[rows, 4 * D_HGRN:4 * D_HGRN + D_TMLP])
        vv = _gelu(z_ref[rows, 4 * D_HGRN + D_TMLP:D_IN])
        for gi in range(N_GROUPS):
            sl = slice(gi * GROUP_DIM, (gi + 1) * GROUP_DIM)
            vg = vv[:, sl]
            mu = jnp.mean(vg, axis=-1, keepdims=True)
            cen = vg - mu
            var = jnp.mean(cen * cen, axis=-1, keepdims=True)
            vn = (cen * lax.rsqrt(var + LN_EPS)) * lng_ref[:, sl] + lnb_ref[:, sl]
            mixed = _dot(wsp_ref[gi], vn.astype(BF16)) + bsp_ref[:, sl]
            y_ref[rows, D_HGRN + gi * GROUP_DIM:D_HGRN + (gi + 1) * GROUP_DIM] = (
                u[:, sl] * mixed).astype(BF16)

    late = n_pieces - early

    @pl.when(shared_ok)
    def _():
        state = [st_ref[hd] for hd in range(N_HEADS)]
        stride = (n_sub * N_HEADS) // late

        def piece_for_slot(slot):
            if slot % stride == 0:
                project(early + slot // stride, 1)

        for c in range(n_sub):
            state = _hgrn_sub_chunk(z_ref, kk_ref, b_ref, hn_ref, y_ref, c * SUB, state, True,
                                    lambda hd, c=c: piece_for_slot(c * N_HEADS + hd))
        for hd in range(N_HEADS):
            st_ref[hd] = state[hd]

    @pl.when(jnp.logical_not(shared_ok))
    def _():
        project(early, late)

        def sub_chunk(c, carry):
            r0 = pl.multiple_of(c * SUB, SUB)
            state = [st_ref[hd] for hd in range(N_HEADS)]
            state = _hgrn_sub_chunk(z_ref, kk_ref, b_ref, hn_ref, y_ref, r0, state, False)
            for hd in range(N_HEADS):
                st_ref[hd] = state[hd]
            return carry

        lax.fori_loop(0, n_sub, sub_chunk, 0)


def _mixer_kernel(x_ref, xn_ref, gain_ref, win_ref, lb_ref, hn_ref, lng_ref, lnb_ref, wsp_ref, bsp_ref,
                  wout_ref, o_ref, za_ref, zb_ref, y_ref, b_ref, kk_ref, st_ref):
    def normed(xv):
        ms = jnp.mean(xv * xv, axis=-1, keepdims=True)
        return ((xv * lax.rsqrt(ms + RMS_EPS)) * gain_ref[...]).astype(BF16)

    @pl.when(pl.program_id(1) == 0)
    def _():
        st_ref[...] = jnp.zeros_like(st_ref)

    @pl.when((pl.program_id(0) == 0) & (pl.program_id(1) == 0))
    def _():
        za_ref[...] = _dot(normed(x_ref[0, 0:MIXER_ROWS, :]), win_ref[...])

    tile_refs = (win_ref, lb_ref, hn_ref, lng_ref, lnb_ref, wsp_ref, bsp_ref)
    _mixer_tile(za_ref, zb_ref, normed(x_ref[0, MIXER_ROWS:2 * MIXER_ROWS, :]), *tile_refs,
                y_ref, b_ref, kk_ref, st_ref)
    o_ref[0, 0:MIXER_ROWS, :] = x_ref[0, 0:MIXER_ROWS, :] + _dot(y_ref[...], wout_ref[...])
    _mixer_tile(zb_ref, za_ref, normed(xn_ref[0]), *tile_refs, y_ref, b_ref, kk_ref, st_ref)
    o_ref[0, MIXER_ROWS:2 * MIXER_ROWS, :] = (
        x_ref[0, MIXER_ROWS:2 * MIXER_ROWS, :] + _dot(y_ref[...], wout_ref[...]))


def _mixer(x, gain, w_in, lb, hgrn_norm, ln_g, ln_b, w_sp, b_sp, w_out):
    bsz, seq, _ = x.shape
    step_rows = 2 * MIXER_ROWS
    assert seq % step_rows == 0 and MIXER_ROWS % SUB == 0
    steps = seq // step_rows
    const2 = lambda b, j: (0, 0)
    const3 = lambda b, j: (0, 0, 0)
    once = pl.Buffered(1)

    def next_tile_a(b, j):
        nxt = jnp.minimum(b * steps + j + 1, bsz * steps - 1)
        return (nxt // steps, 2 * (nxt % steps), 0)

    return pl.pallas_call(
        _mixer_kernel,
        grid=(bsz, steps),
        in_specs=[
            pl.BlockSpec((1, step_rows, D_MODEL), lambda b, j: (b, j, 0)),
            pl.BlockSpec((1, MIXER_ROWS, D_MODEL), next_tile_a),
            pl.BlockSpec((1, D_MODEL), const2),
            pl.BlockSpec((D_MODEL, D_IN), const2, pipeline_mode=once),
            pl.BlockSpec((1, D_HGRN), const2),
            pl.BlockSpec((1, HEAD_DIM), const2),
            pl.BlockSpec((1, D_TMLP), const2),
            pl.BlockSpec((1, D_TMLP), const2),
            pl.BlockSpec((N_GROUPS, SUB, SUB), const3),
            pl.BlockSpec((SUB, D_TMLP), const2),
            pl.BlockSpec((D_HGRN + D_TMLP, D_MODEL), const2, pipeline_mode=once),
        ],
        out_specs=pl.BlockSpec((1, step_rows, D_MODEL), lambda b, j: (b, j, 0)),
        out_shape=jax.ShapeDtypeStruct(x.shape, F32),
        scratch_shapes=[
            pltpu.VMEM((MIXER_ROWS, D_IN), F32),
            pltpu.VMEM((MIXER_ROWS, D_IN), F32),
            pltpu.VMEM((MIXER_ROWS, D_HGRN + D_TMLP), BF16),
            pltpu.VMEM((MIXER_ROWS, D_HGRN), F32),
            pltpu.VMEM((MIXER_ROWS, D_HGRN), F32),
            pltpu.VMEM((N_HEADS, HEAD_DIM, HEAD_DIM), F32),
        ],
        compiler_params=pltpu.CompilerParams(
            dimension_semantics=("arbitrary", "arbitrary"),
            vmem_limit_bytes=VMEM_LIMIT_BYTES),
        name="mixer",
    )(x, x, gain, w_in, lb, hgrn_norm, ln_g, ln_b, w_sp, b_sp, w_out)


def _pack_bf16_pairs(a):
    lo = lax.bitcast_convert_type(a[:, :PACKED].astype(BF16).astype(F32), jnp.uint32)
    hi = lax.bitcast_convert_type(a[:, PACKED:].astype(BF16).astype(F32), jnp.uint32)
    return jnp.bitwise_or(lax.shift_right_logical(lo, jnp.uint32(16)),
                          jnp.bitwise_and(hi, jnp.uint32(0xFFFF0000)))


def _unpack_bf16_pairs(u):
    lo = lax.bitcast_convert_type(lax.shift_left(u, jnp.uint32(16)), F32)
    hi = lax.bitcast_convert_type(jnp.bitwise_and(u, jnp.uint32(0xFFFF0000)), F32)
    return lo, hi


def _router_kernel(x_ref, gain_ref, wr_hi_ref, wr_lo_ref, later_ref, hpk_ref, meta_ref, wts_ref,
                   cnt_out_ref, cnt_ref):
    @pl.when(pl.program_id(0) == 0)
    def _():
        cnt_ref[...] = jnp.zeros_like(cnt_ref)

    x = x_ref[...]
    ms = jnp.mean(x * x, axis=-1, keepdims=True)
    h = (x * lax.rsqrt(ms + RMS_EPS)) * gain_ref[...]
    hpk_ref[...] = _pack_bf16_pairs(h)
    h_hi, h_lo = _split_bf16(h)
    logits = _dot(h_hi, wr_hi_ref[...]) + (_dot(h_hi, wr_lo_ref[...]) + _dot(h_lo, wr_hi_ref[...]))
    lt = logits.T
    sub = lax.broadcasted_iota(jnp.int32, (EXPERTS_PER_GROUP, ROUTE_ROWS), 0)
    neg = jnp.float32(-jnp.inf)
    big = jnp.int32(1 << 20)
    gl = jnp.where(sub < N_EXPERT_GROUPS, lt[N_EXPERTS:N_EXPERTS + EXPERTS_PER_GROUP], neg)
    gmax = jnp.max(gl, axis=0, keepdims=True)
    p_sel = 1.0 / jnp.sum(jnp.exp(gl - gmax), axis=0, keepdims=True)
    g_idx = jnp.min(jnp.where(gl == gmax, sub, big), axis=0, keepdims=True)
    el = lt[(N_EXPERT_GROUPS - 1) * EXPERTS_PER_GROUP:N_EXPERTS]
    for g in range(N_EXPERT_GROUPS - 2, -1, -1):
        el = jnp.where(g_idx == g, lt[g * EXPERTS_PER_GROUP:(g + 1) * EXPERTS_PER_GROUP], el)
    v1 = jnp.max(el, axis=0, keepdims=True)
    i1 = jnp.min(jnp.where(el == v1, sub, big), axis=0, keepdims=True)
    el2 = jnp.where(sub == i1, neg, el)
    v2 = jnp.max(el2, axis=0, keepdims=True)
    i2 = jnp.min(jnp.where(el2 == v2, sub, big), axis=0, keepdims=True)
    e2x = jnp.exp(v2 - v1)
    w1 = p_sel / (1.0 + e2x)
    w2 = p_sel * e2x / (1.0 + e2x)
    e1 = g_idx * EXPERTS_PER_GROUP + i1
    e2 = g_idx * EXPERTS_PER_GROUP + i2

    expert = lax.broadcasted_iota(jnp.int32, (N_EXPERTS, ROUTE_ROWS), 0)
    assigned = ((expert == e1) | (expert == e2)).astype(F32)
    carry = cnt_ref[:, 0:1]
    pieces = []
    for j in range(ROUTE_ROWS // SUB):
        piece = assigned[:, j * SUB:(j + 1) * SUB]
        pieces.append(carry + _dot(piece.astype(BF16), later_ref[...]))
        carry = carry + jnp.sum(piece, axis=1, keepdims=True)
    earlier = jnp.concatenate(pieces, axis=1)
    rank1 = jnp.sum(jnp.where(expert == e1, earlier, 0.0), axis=0, keepdims=True).astype(jnp.int32)
    rank2 = jnp.sum(jnp.where(expert == e2, earlier, 0.0), axis=0, keepdims=True).astype(jnp.int32)
    meta_ref[...] = jnp.where(sub == 0, e1, jnp.where(sub == 1, e2, jnp.where(
        sub == 2, rank1, jnp.where(sub == 3, rank2, 0))))
    cnt_ref[...] = jnp.broadcast_to(carry, cnt_ref.shape)
    cnt_out_ref[...] = cnt_ref[...]

    w_rows = jnp.where(sub == 0, w1, jnp.where(sub == 1, w2, 0.0))
    pad_w = jnp.zeros((ROUTER_LANES - EXPERTS_PER_GROUP, ROUTE_ROWS), F32)
    wts_ref[...] = jnp.concatenate([w_rows, pad_w], axis=0).T


def _router(x2d, gain, wr_hi, wr_lo, later):
    n = x2d.shape[0]
    assert n % ROUTE_ROWS == 0
    const2 = lambda i: (0, 0)
    rows = lambda i: (i, 0)
    return pl.pallas_call(
        _router_kernel,
        grid=(n // ROUTE_ROWS,),
        in_specs=[
            pl.BlockSpec((ROUTE_ROWS, D_MODEL), rows),
            pl.BlockSpec((1, D_MODEL), const2),
            pl.BlockSpec((D_MODEL, ROUTER_LANES), const2),
            pl.BlockSpec((D_MODEL, ROUTER_LANES), const2),
            pl.BlockSpec((SUB, SUB), const2),
        ],
        out_specs=[
            pl.BlockSpec((ROUTE_ROWS, PACKED), rows),
            pl.BlockSpec((EXPERTS_PER_GROUP, ROUTE_ROWS), lambda i: (0, i)),
            pl.BlockSpec((ROUTE_ROWS, ROUTER_LANES), rows),
            pl.BlockSpec((N_EXPERTS, ROUTER_LANES), const2),
        ],
        out_shape=[
            jax.ShapeDtypeStruct((n, PACKED), jnp.uint32),
            jax.ShapeDtypeStruct((EXPERTS_PER_GROUP, n), jnp.int32),
            jax.ShapeDtypeStruct((n, ROUTER_LANES), F32),
            jax.ShapeDtypeStruct((N_EXPERTS, ROUTER_LANES), F32),
        ],
        scratch_shapes=[pltpu.VMEM((N_EXPERTS, ROUTER_LANES), F32)],
        compiler_params=pltpu.CompilerParams(
            dimension_semantics=("arbitrary",), vmem_limit_bytes=VMEM_LIMIT_BYTES),
        name="router",
    )(x2d, gain, wr_hi, wr_lo, later)


def _positions_kernel(meta_ref, row_start_ref, pos_ref):
    meta = meta_ref[...]
    expert = lax.broadcasted_iota(jnp.int32, (N_EXPERTS, ROUTE_ROWS), 0)
    starts = row_start_ref[:, 0:1]
    base1 = jnp.sum(jnp.where(expert == meta[0:1], starts, 0), axis=0, keepdims=True)
    base2 = jnp.sum(jnp.where(expert == meta[1:2], starts, 0), axis=0, keepdims=True)
    sub = lax.broadcasted_iota(jnp.int32, meta.shape, 0)
    pos_ref[...] = jnp.where(sub == 0, base1 + meta[2:3], jnp.where(sub == 1, base2 + meta[3:4], 0))


def _positions(meta, row_start_lanes):
    n = meta.shape[1]
    return pl.pallas_call(
        _positions_kernel,
        grid=(n // ROUTE_ROWS,),
        in_specs=[pl.BlockSpec((EXPERTS_PER_GROUP, ROUTE_ROWS), lambda i: (0, i)),
                  pl.BlockSpec((N_EXPERTS, ROUTER_LANES), lambda i: (0, 0))],
        out_specs=pl.BlockSpec((EXPERTS_PER_GROUP, ROUTE_ROWS), lambda i: (0, i)),
        out_shape=jax.ShapeDtypeStruct(meta.shape, jnp.int32),
        compiler_params=pltpu.CompilerParams(
            dimension_semantics=("arbitrary",), vmem_limit_bytes=VMEM_LIMIT_BYTES),
        name="positions",
    )(meta, row_start_lanes)


def _row_copy(src_ref, src_row, dst_ref, dst_row, sem):
    return pltpu.make_async_copy(src_ref.at[pl.ds(src_row, 1)], dst_ref.at[pl.ds(dst_row, 1)], sem)


def _scatter_kernel(pos1_ref, pos2_ref, hpk_ref, sorted_ref, sem):
    def issue(i, carry):
        _row_copy(hpk_ref, i, sorted_ref, pos1_ref[i], sem).start(priority=0)
        _row_copy(hpk_ref, i, sorted_ref, pos2_ref[i], sem).start(priority=1)
        return carry

    lax.fori_loop(0, SCATTER_ROWS, issue, 0, unroll=DMA_UNROLL)

    def drain(i, carry):
        _row_copy(hpk_ref, 0, sorted_ref, 0, sem).wait()
        _row_copy(hpk_ref, 0, sorted_ref, 0, sem).wait()
        return carry

    lax.fori_loop(0, SCATTER_ROWS, drain, 0, unroll=DMA_UNROLL)


def _scatter(pos1, pos2, hpk, n_rows):
    n = hpk.shape[0]
    assert n % SCATTER_ROWS == 0
    idx = lambda i: (i,)
    return pl.pallas_call(
        _scatter_kernel,
        grid=(n // SCATTER_ROWS,),
        in_specs=[
            pl.BlockSpec((SCATTER_ROWS,), idx, memory_space=pltpu.SMEM),
            pl.BlockSpec((SCATTER_ROWS,), idx, memory_space=pltpu.SMEM),
            pl.BlockSpec((SCATTER_ROWS, PACKED), lambda i: (i, 0)),
        ],
        out_specs=pl.BlockSpec(memory_space=pl.ANY),
        out_shape=jax.ShapeDtypeStruct((n_rows, PACKED), jnp.uint32),
        scratch_shapes=[pltpu.SemaphoreType.DMA],
        compiler_params=pltpu.CompilerParams(
            dimension_semantics=("arbitrary",), vmem_limit_bytes=VMEM_LIMIT_BYTES),
        name="scatter_rows",
    )(pos1, pos2, hpk)


def _expert_kernel(tile_expert_ref, tile_rows_ref, n_used_ref, lhs_ref, wg_ref, wu_ref, wd_ref,
                   o_ref):
    del tile_expert_ref, n_used_ref
    n_valid = tile_rows_ref[pl.program_id(0)]

    @pl.when(n_valid > 0)
    def _():
        valid = lax.broadcasted_iota(jnp.int32, (EXPERT_TILE, 1), 0) < n_valid
        lo, hi = _unpack_bf16_pairs(lhs_ref[...])
        hh = jnp.concatenate([jnp.where(valid, lo, 0.0).astype(BF16),
                              jnp.where(valid, hi, 0.0).astype(BF16)], axis=1)
        w_gu = jnp.concatenate([wg_ref[0, 0].astype(BF16), wu_ref[0, 0].astype(BF16)], axis=1)
        gu = _dot(hh, w_gu)
        hid = _silu(gu[:, :D_EXPERT]) * gu[:, D_EXPERT:]
        o_ref[...] = _pack_bf16_pairs(_dot(hid.astype(BF16), wd_ref[0, 0].astype(BF16)))

    @pl.when(n_valid <= 0)
    def _():
        o_ref[...] = jnp.zeros_like(o_ref)


def _experts(tile_expert, tile_rows, n_used, sorted_rows, w_gate, w_up, w_down, layer):
    n_tiles = sorted_rows.shape[0] // EXPERT_TILE
    live = lambda i, nu: jnp.minimum(i, nu[0] - 1)
    expert = lambda i, te, tr, nu: (layer, te[live(i, nu)], 0, 0)
    return pl.pallas_call(
        _expert_kernel,
        grid_spec=pltpu.PrefetchScalarGridSpec(
            num_scalar_prefetch=3,
            grid=(n_tiles,),
            in_specs=[
                pl.BlockSpec((EXPERT_TILE, PACKED), lambda i, te, tr, nu: (live(i, nu), 0)),
                pl.BlockSpec((1, 1, D_MODEL, D_EXPERT), expert),
                pl.BlockSpec((1, 1, D_MODEL, D_EXPERT), expert),
                pl.BlockSpec((1, 1, D_EXPERT, D_MODEL), expert),
            ],
            out_specs=pl.BlockSpec((EXPERT_TILE, PACKED), lambda i, te, tr, nu: (i, 0)),
        ),
        out_shape=jax.ShapeDtypeStruct(sorted_rows.shape, jnp.uint32),
        compiler_params=pltpu.CompilerParams(
            dimension_semantics=("arbitrary",), vmem_limit_bytes=VMEM_LIMIT_BYTES),
        name="experts",
    )(tile_expert, tile_rows, n_used, sorted_rows, w_gate, w_up, w_down)


def _combine_kernel(pos1_ref, pos2_ref, x_ref, wts_ref, ys_ref, gfin_ref, o_ref, buf1_ref, buf2_ref,
                    sem, *, final_norm):
    def issue(i, carry):
        _row_copy(ys_ref, pos1_ref[i], buf1_ref, i, sem).start(priority=0)
        _row_copy(ys_ref, pos2_ref[i], buf2_ref, i, sem).start(priority=1)
        return carry

    lax.fori_loop(0, COMBINE_ROWS, issue, 0, unroll=DMA_UNROLL)

    def drain(i, carry):
        _row_copy(ys_ref, 0, buf1_ref, 0, sem).wait()
        _row_copy(ys_ref, 0, buf2_ref, 0, sem).wait()
        return carry

    lax.fori_loop(0, COMBINE_ROWS, drain, 0, unroll=DMA_UNROLL)
    lo1, hi1 = _unpack_bf16_pairs(buf1_ref[...])
    lo2, hi2 = _unpack_bf16_pairs(buf2_ref[...])
    w1 = wts_ref[:, 0:1]
    w2 = wts_ref[:, 1:2]
    x = x_ref[...]
    y = jnp.concatenate([x[:, :PACKED] + (w1 * lo1 + w2 * lo2),
                         x[:, PACKED:] + (w1 * hi1 + w2 * hi2)], axis=1)
    if final_norm:
        ms = jnp.mean(y * y, axis=-1, keepdims=True)
        y = (y * lax.rsqrt(ms + RMS_EPS)) * gfin_ref[...]
    o_ref[...] = y


def _combine(pos1, pos2, x2d, wts, ys, gain_final, final_norm):
    n = x2d.shape[0]
    assert n % COMBINE_ROWS == 0
    idx = lambda i: (i,)
    rows = lambda i: (i, 0)
    return pl.pallas_call(
        functools.partial(_combine_kernel, final_norm=final_norm),
        grid=(n // COMBINE_ROWS,),
        in_specs=[
            pl.BlockSpec((COMBINE_ROWS,), idx, memory_space=pltpu.SMEM),
            pl.BlockSpec((COMBINE_ROWS,), idx, memory_space=pltpu.SMEM),
            pl.BlockSpec((COMBINE_ROWS, D_MODEL), rows),
            pl.BlockSpec((COMBINE_ROWS, ROUTER_LANES), rows),
            pl.BlockSpec(memory_space=pl.ANY),
            pl.BlockSpec((1, D_MODEL), lambda i: (0, 0)),
        ],
        out_specs=pl.BlockSpec((COMBINE_ROWS, D_MODEL), rows),
        out_shape=jax.ShapeDtypeStruct(x2d.shape, F32),
        scratch_shapes=[
            pltpu.VMEM((COMBINE_ROWS, PACKED), jnp.uint32),
            pltpu.VMEM((COMBINE_ROWS, PACKED), jnp.uint32),
            pltpu.SemaphoreType.DMA,
        ],
        compiler_params=pltpu.CompilerParams(
            dimension_semantics=("arbitrary",), vmem_limit_bytes=VMEM_LIMIT_BYTES),
        name="combine",
    )(pos1, pos2, x2d, wts, ys, gain_final)


def _moe(x2d, gain, wr_hi, wr_lo, later, w_gate, w_up, w_down, layer, gain_final, final_norm):
    n = x2d.shape[0]
    max_tiles = (2 * n) // EXPERT_TILE + N_EXPERTS
    hpk, meta, wts, cnt = _router(x2d, gain, wr_hi, wr_lo, later)
    counts = cnt[:, 0].astype(jnp.int32)
    tiles_per = (counts + (EXPERT_TILE - 1)) // EXPERT_TILE
    tile_end = jnp.cumsum(tiles_per)
    tile_start = tile_end - tiles_per
    row_start = tile_start * EXPERT_TILE
    tile_ids = jnp.arange(max_tiles, dtype=jnp.int32)
    tile_expert = jnp.minimum(
        jnp.sum((tile_ids[:, None] >= tile_end[None, :]).astype(jnp.int32), axis=1), N_EXPERTS - 1)
    n_used = tile_end[-1:].astype(jnp.int32)
    rows_left = counts[tile_expert] - (tile_ids - tile_start[tile_expert]) * EXPERT_TILE
    tile_rows = jnp.where(tile_ids < n_used[0], jnp.clip(rows_left, 0, EXPERT_TILE), 0)
    pos = _positions(meta, jnp.broadcast_to(row_start[:, None], (N_EXPERTS, ROUTER_LANES)))
    sorted_rows = _scatter(pos[0], pos[1], hpk, max_tiles * EXPERT_TILE)
    ys = _experts(tile_expert, tile_rows.astype(jnp.int32), n_used, sorted_rows, w_gate, w_up, w_down,
                  layer)
    return _combine(pos[0], pos[1], x2d, wts, ys, gain_final, final_norm)


def kernel(x, lb_logits, norm_mix, w_in, hgrn_norm, tmlp_ln_g, tmlp_ln_b, w_spatial, b_spatial,
           w_out, norm_ffn, w_router_group, w_router_expert, w_gate, w_up, w_down, norm_final):
    depth = w_in.shape[0]
    bsz, seq, _ = x.shape
    p = jax.nn.softmax(lb_logits.astype(F32), axis=0)
    lower_bounds = jnp.cumsum(p, axis=0) - p[0:1]
    tril = jnp.tril(jnp.ones((SUB, SUB), dtype=bool))
    later = jnp.triu(jnp.ones((SUB, SUB), BF16), 1)
    for layer in range(depth):
        w_sp = jnp.where(tril[None], w_spatial[layer], 0.0).astype(BF16)
        b_sp = jnp.repeat(b_spatial[layer].T, GROUP_DIM, axis=1)
        x = _mixer(x, norm_mix[layer][None], w_in[layer].astype(BF16), lower_bounds[layer][None],
                   hgrn_norm[layer][None], tmlp_ln_g[layer][None], tmlp_ln_b[layer][None],
                   w_sp, b_sp, w_out[layer].astype(BF16))
        w_r = jnp.concatenate([w_router_expert[layer], w_router_group[layer]], axis=1)
        w_r = jnp.pad(w_r, ((0, 0), (0, ROUTER_LANES - w_r.shape[1])))
        wr_hi, wr_lo = _split_bf16(w_r)
        x2d = _moe(x.reshape(bsz * seq, D_MODEL), norm_ffn[layer][None], wr_hi, wr_lo, later,
                   w_gate, w_up, w_down, layer, norm_final[None], final_norm=(layer == depth - 1))
        x = x2d.reshape(bsz, seq, D_MODEL)
    return x
```

```python
import functools

import jax
import jax.numpy as jnp
from jax import lax
from jax.experimental import pallas as pl
from jax.experimental.pallas import tpu as pltpu

F32 = jnp.float32
BF16 = jnp.bfloat16

D_MODEL = 1024
N_HEADS = 4
HEAD_DIM = 128
D_HGRN = N_HEADS * HEAD_DIM
N_GROUPS = 4
GROUP_DIM = 128
D_TMLP = N_GROUPS * GROUP_DIM
D_IN = 4 * D_HGRN + 2 * D_TMLP
SUB = 128
N_EXPERT_GROUPS = 4
EXPERTS_PER_GROUP = 8
N_EXPERTS = N_EXPERT_GROUPS * EXPERTS_PER_GROUP
D_EXPERT = 256
ROUTER_LANES = 128
RMS_EPS = 1e-6
LN_EPS = 1e-5
F_FLOOR = 1e-30
HGRN_SAFE_EXP = 60.0
SQRT_HALF = 0.7071067811865476

MIXER_ROWS = 512
PROJ_PIECE = 256
ROUTE_ROWS = 1024
SCATTER_ROWS = 2048
COMBINE_ROWS = 1024
EXPERT_TILE = 1024
PACKED = D_MODEL // 2
DMA_UNROLL = 8
VMEM_LIMIT_BYTES = 56 * 1024 * 1024


def _dot(a, b):
    return jnp.dot(a, b, preferred_element_type=F32)


def _dot_nt(a, b):
    return lax.dot_general(a, b, (((1,), (1,)), ((), ())), preferred_element_type=F32)


def _dot_tn(a, b):
    return lax.dot_general(a, b, (((0,), (0,)), ((), ())), preferred_element_type=F32)


def _split_bf16(a):
    hi = a.astype(BF16)
    lo = (a - hi.astype(F32)).astype(BF16)
    return hi, lo


def _gelu(a):
    return 0.5 * a * (1.0 + lax.erf(a * SQRT_HALF))


def _silu(a):
    return a * jax.nn.sigmoid(a)


def _boundary_rows(b_ref, r0, m, width):
    pieces = []
    if 2 * m >= 8:
        for s0 in range(0, SUB, 2 * m):
            row = b_ref[pl.ds(r0 + (s0 + m - 1), 1), :]
            pieces.append(jnp.broadcast_to(row, (2 * m, width)))
    else:
        row8 = lax.broadcasted_iota(jnp.int32, (8, width), 0)
        for g0 in range(0, SUB, 8):
            acc = None
            for s0 in range(0, 8, 2 * m):
                row = jnp.broadcast_to(b_ref[pl.ds(r0 + (g0 + s0 + m - 1), 1), :], (8, width))
                acc = row if acc is None else jnp.where(row8 >= s0, row, acc)
            pieces.append(acc)
    return jnp.concatenate(pieces, axis=0)


def _half_middle_rows(b_ref, r0, width):
    half = SUB // 2
    return jnp.concatenate(
        [jnp.broadcast_to(b_ref[pl.ds(r0 + (s0 + half // 2 - 1), 1), :], (half, width))
         for s0 in range(0, SUB, half)], axis=0)


def _hgrn_level(att, q, kk, b, b_ref, r0, m, tx, row):
    ref_pt = _boundary_rows(b_ref, r0, m, D_HGRN)
    decay = jnp.exp(-jnp.abs(b - ref_pt))
    right = jnp.bitwise_and(row, m) != 0
    qt = jnp.where(right, q * decay, 0.0).astype(BF16)
    kt = jnp.where(right, 0.0, kk * decay).astype(BF16)
    same_block = tx < 2 * m
    out = []
    for h in range(N_HEADS):
        sl = slice(h * HEAD_DIM, (h + 1) * HEAD_DIM)
        term = jnp.where(same_block, _dot_nt(qt[:, sl], kt[:, sl]), 0.0)
        out.append(term if att is None else att[h] + term)
    return out


def _hgrn_attention(q, kk, b, b_ref, r0, shared_reference):
    t_idx = lax.broadcasted_iota(jnp.int32, (SUB, SUB), 0)
    s_idx = lax.broadcasted_iota(jnp.int32, (SUB, SUB), 1)
    tx = jnp.bitwise_xor(t_idx, s_idx)
    row = lax.broadcasted_iota(jnp.int32, (SUB, D_HGRN), 0)
    half = SUB // 2
    att = _hgrn_level(None, q, kk, b, b_ref, r0, half, tx, row)
    if shared_reference:
        expo = b - _half_middle_rows(b_ref, r0, D_HGRN)
        qt = (q * jnp.exp(expo)).astype(BF16)
        kt = (kk * jnp.exp(-expo)).astype(BF16)
        keep = (tx < half) & (s_idx <= t_idx)
        for h in range(N_HEADS):
            sl = slice(h * HEAD_DIM, (h + 1) * HEAD_DIM)
            att[h] = att[h] + jnp.where(keep, _dot_nt(qt[:, sl], kt[:, sl]), 0.0)
        return att
    qb = q.astype(BF16)
    kb = kk.astype(BF16)
    for h in range(N_HEADS):
        sl = slice(h * HEAD_DIM, (h + 1) * HEAD_DIM)
        att[h] = att[h] + jnp.where(tx == 0, _dot_nt(qb[:, sl], kb[:, sl]), 0.0)
    m = 1
    while m < half:
        att = _hgrn_level(att, q, kk, b, b_ref, r0, m, tx, row)
        m *= 2
    return att


def _hgrn_sub_chunk(z_ref, kk_ref, b_ref, hn_ref, y_ref, r0, state, shared_reference, per_head=None):
    rows = pl.ds(r0, SUB)
    q = _silu(z_ref[rows, 0:D_HGRN])
    kk = kk_ref[rows, :]
    b = b_ref[rows, :]
    v = z_ref[rows, 2 * D_HGRN:3 * D_HGRN].astype(BF16)
    att = _hgrn_attention(q, kk, b, b_ref, r0, shared_reference)
    b_end = b_ref[pl.ds(r0 + (SUB - 1), 1), :]
    q0 = (q * jnp.exp(b)).astype(BF16)
    k_end = (kk * jnp.exp(b_end - b)).astype(BF16)
    s_decay = jnp.exp(b_end)
    g = _silu(z_ref[rows, 3 * D_HGRN:4 * D_HGRN])
    new_state = []
    for hd in range(N_HEADS):
        if per_head is not None:
            per_head(hd)
        sl = slice(hd * HEAD_DIM, (hd + 1) * HEAD_DIM)
        st = state[hd]
        o = _dot(att[hd].astype(BF16), v[:, sl]) + _dot_nt(q0[:, sl], st.astype(BF16))
        new_state.append(st * s_decay[:, sl] + _dot_tn(v[:, sl], k_end[:, sl]))
        oms = jnp.mean(o * o, axis=-1, keepdims=True)
        on = (o * lax.rsqrt(oms + RMS_EPS)) * hn_ref[...]
        y_ref[rows, sl] = (on * g[:, sl]).astype(BF16)
    return new_state


def _mixer_tile(z_ref, zn_ref, h_next, win_ref, lb_ref, hn_ref, lng_ref, lnb_ref, wsp_ref, bsp_ref,
                y_ref, b_ref, kk_ref, st_ref):
    def project(first, count):
        for p in range(first, first + count):
            cols = slice(p * PROJ_PIECE, (p + 1) * PROJ_PIECE)
            zn_ref[:, cols] = _dot(h_next, win_ref[:, cols])

    n_pieces = D_IN // PROJ_PIECE
    early = n_pieces // 3

    t_idx = lax.broadcasted_iota(jnp.int32, (SUB, SUB), 0)
    s_idx = lax.broadcasted_iota(jnp.int32, (SUB, SUB), 1)
    tri = (s_idx <= t_idx).astype(BF16)
    n_sub = MIXER_ROWS // SUB

    worst = jnp.zeros((SUB, D_HGRN), F32)
    for c in range(n_sub):
        rows = pl.ds(c * SUB, SUB)
        lb = lb_ref[...]
        fg = lb + (1.0 - lb) * jax.nn.sigmoid(z_ref[rows, D_HGRN:2 * D_HGRN])
        lf_hi, lf_lo = _split_bf16(jnp.log(jnp.maximum(fg, F_FLOOR)))
        kk_ref[rows, :] = 1.0 - fg
        b = _dot(tri, lf_hi) + _dot(tri, lf_lo)
        b_ref[rows, :] = b
        worst = jnp.maximum(worst, jnp.abs(b - _half_middle_rows(b_ref, c * SUB, D_HGRN)))
    shared_ok = jnp.max(worst) <= HGRN_SAFE_EXP

    for c in range(n_sub):
        project(c * early // n_sub, (c + 1) * early // n_sub - c * early // n_sub)
        rows = pl.ds(c * SUB, SUB)
        u = _gelu(z_ref[rows, 4 * D_HGRN:4 * D_HGRN + D_TMLP])
        vv = _gelu(z_ref[rows, 4 * D_HGRN + D_TMLP:D_IN])
        for gi in range(N_GROUPS):
            sl = slice(gi * GROUP_DIM, (gi + 1) * GROUP_DIM)
            vg = vv[:, sl]
            mu = jnp.mean(vg, axis=-1, keepdims=True)
            cen = vg - mu
            var = jnp.mean(cen * cen, axis=-1, keepdims=True)
            vn = (cen * lax.rsqrt(var + LN_EPS)) * lng_ref[:, sl] + lnb_ref[:, sl]
            mixed = _dot(wsp_ref[gi], vn.astype(BF16)) + bsp_ref[:, sl]
            y_ref[rows, D_HGRN + gi * GROUP_DIM:D_HGRN + (gi + 1) * GROUP_DIM] = (
                u[:, sl] * mixed).astype(BF16)

    late = n_pieces - early

    @pl.when(shared_ok)
    def _():
        state = [st_ref[hd] for hd in range(N_HEADS)]
        stride = (n_sub * N_HEADS) // late

        def piece_for_slot(slot):
            if slot % stride == 0:
                project(early + slot // stride, 1)

        for c in range(n_sub):
            state = _hgrn_sub_chunk(z_ref, kk_ref, b_ref, hn_ref, y_ref, c * SUB, state, True,
                                    lambda hd, c=c: piece_for_slot(c * N_HEADS + hd))
        for hd in range(N_HEADS):
            st_ref[hd] = state[hd]

    @pl.when(jnp.logical_not(shared_ok))
    def _():
        project(early, late)

        def sub_chunk(c, carry):
            r0 = pl.multiple_of(c * SUB, SUB)
            state = [st_ref[hd] for hd in range(N_HEADS)]
            state = _hgrn_sub_chunk(z_ref, kk_ref, b_ref, hn_ref, y_ref, r0, state, False)
            for hd in range(N_HEADS):
                st_ref[hd] = state[hd]
            return carry

        lax.fori_loop(0, n_sub, sub_chunk, 0)


def _mixer_kernel(x_ref, xn_ref, gain_ref, win_ref, lb_ref, hn_ref, lng_ref, lnb_ref, wsp_ref, bsp_ref,
                  wout_ref, o_ref, za_ref, zb_ref, y_ref, b_ref, kk_ref, st_ref):
    def normed(xv):
        ms = jnp.mean(xv * xv, axis=-1, keepdims=True)
        return ((xv * lax.rsqrt(ms + RMS_EPS)) * gain_ref[...]).astype(BF16)

    @pl.when(pl.program_id(1) == 0)
    def _():
        st_ref[...] = jnp.zeros_like(st_ref)

    @pl.when((pl.program_id(0) == 0) & (pl.program_id(1) == 0))
    def _():
        za_ref[...] = _dot(normed(x_ref[0, 0:MIXER_ROWS, :]), win_ref[...])

    tile_refs = (win_ref, lb_ref, hn_ref, lng_ref, lnb_ref, wsp_ref, bsp_ref)
    _mixer_tile(za_ref, zb_ref, normed(x_ref[0, MIXER_ROWS:2 * MIXER_ROWS, :]), *tile_refs,
                y_ref, b_ref, kk_ref, st_ref)
    o_ref[0, 0:MIXER_ROWS, :] = x_ref[0, 0:MIXER_ROWS, :] + _dot(y_ref[...], wout_ref[...])
    _mixer_tile(zb_ref, za_ref, normed(xn_ref[0]), *tile_refs, y_ref, b_ref, kk_ref, st_ref)
    o_ref[0, MIXER_ROWS:2 * MIXER_ROWS, :] = (
        x_ref[0, MIXER_ROWS:2 * MIXER_ROWS, :] + _dot(y_ref[...], wout_ref[...]))


def _mixer(x, gain, w_in, lb, hgrn_norm, ln_g, ln_b, w_sp, b_sp, w_out):
    bsz, seq, _ = x.shape
    step_rows = 2 * MIXER_ROWS
    assert seq % step_rows == 0 and MIXER_ROWS % SUB == 0
    steps = seq // step_rows
    const2 = lambda b, j: (0, 0)
    const3 = lambda b, j: (0, 0, 0)
    once = pl.Buffered(1)

    def next_tile_a(b, j):
        nxt = jnp.minimum(b * steps + j + 1, bsz * steps - 1)
        return (nxt // steps, 2 * (nxt % steps), 0)

    return pl.pallas_call(
        _mixer_kernel,
        grid=(bsz, steps),
        in_specs=[
            pl.BlockSpec((1, step_rows, D_MODEL), lambda b, j: (b, j, 0)),
            pl.BlockSpec((1, MIXER_ROWS, D_MODEL), next_tile_a),
            pl.BlockSpec((1, D_MODEL), const2),
            pl.BlockSpec((D_MODEL, D_IN), const2, pipeline_mode=once),
            pl.BlockSpec((1, D_HGRN), const2),
            pl.BlockSpec((1, HEAD_DIM), const2),
            pl.BlockSpec((1, D_TMLP), const2),
            pl.BlockSpec((1, D_TMLP), const2),
            pl.BlockSpec((N_GROUPS, SUB, SUB), const3),
            pl.BlockSpec((SUB, D_TMLP), const2),
            pl.BlockSpec((D_HGRN + D_TMLP, D_MODEL), const2, pipeline_mode=once),
        ],
        out_specs=pl.BlockSpec((1, step_rows, D_MODEL), lambda b, j: (b, j, 0)),
        out_shape=jax.ShapeDtypeStruct(x.shape, F32),
        scratch_shapes=[
            pltpu.VMEM((MIXER_ROWS, D_IN), F32),
            pltpu.VMEM((MIXER_ROWS, D_IN), F32),
            pltpu.VMEM((MIXER_ROWS, D_HGRN + D_TMLP), BF16),
            pltpu.VMEM((MIXER_ROWS, D_HGRN), F32),
            pltpu.VMEM((MIXER_ROWS, D_HGRN), F32),
            pltpu.VMEM((N_HEADS, HEAD_DIM, HEAD_DIM), F32),
        ],
        compiler_params=pltpu.CompilerParams(
            dimension_semantics=("arbitrary", "arbitrary"),
            vmem_limit_bytes=VMEM_LIMIT_BYTES),
        name="mixer",
    )(x, x, gain, w_in, lb, hgrn_norm, ln_g, ln_b, w_sp, b_sp, w_out)


def _pack_bf16_pairs(a):
    lo = lax.bitcast_convert_type(a[:, :PACKED].astype(BF16).astype(F32), jnp.uint32)
    hi = lax.bitcast_convert_type(a[:, PACKED:].astype(BF16).astype(F32), jnp.uint32)
    return jnp.bitwise_or(lax.shift_right_logical(lo, jnp.uint32(16)),
                          jnp.bitwise_and(hi, jnp.uint32(0xFFFF0000)))


def _unpack_bf16_pairs(u):
    lo = lax.bitcast_convert_type(lax.shift_left(u, jnp.uint32(16)), F32)
    hi = lax.bitcast_convert_type(jnp.bitwise_and(u, jnp.uint32(0xFFFF0000)), F32)
    return lo, hi


def _router_kernel(x_ref, gain_ref, wr_hi_ref, wr_lo_ref, later_ref, hpk_ref, meta_ref, wts_ref,
                   cnt_out_ref, cnt_ref):
    @pl.when(pl.program_id(0) == 0)
    def _():
        cnt_ref[...] = jnp.zeros_like(cnt_ref)

    x = x_ref[...]
    ms = jnp.mean(x * x, axis=-1, keepdims=True)
    h = (x * lax.rsqrt(ms + RMS_EPS)) * gain_ref[...]
    hpk_ref[...] = _pack_bf16_pairs(h)
    h_hi, h_lo = _split_bf16(h)
    logits = _dot(h_hi, wr_hi_ref[...]) + (_dot(h_hi, wr_lo_ref[...]) + _dot(h_lo, wr_hi_ref[...]))
    lt = logits.T
    sub = lax.broadcasted_iota(jnp.int32, (EXPERTS_PER_GROUP, ROUTE_ROWS), 0)
    neg = jnp.float32(-jnp.inf)
    big = jnp.int32(1 << 20)
    gl = jnp.where(sub < N_EXPERT_GROUPS, lt[N_EXPERTS:N_EXPERTS + EXPERTS_PER_GROUP], neg)
    gmax = jnp.max(gl, axis=0, keepdims=True)
    p_sel = 1.0 / jnp.sum(jnp.exp(gl - gmax), axis=0, keepdims=True)
    g_idx = jnp.min(jnp.where(gl == gmax, sub, big), axis=0, keepdims=True)
    el = lt[(N_EXPERT_GROUPS - 1) * EXPERTS_PER_GROUP:N_EXPERTS]
    for g in range(N_EXPERT_GROUPS - 2, -1, -1):
        el = jnp.where(g_idx == g, lt[g * EXPERTS_PER_GROUP:(g + 1) * EXPERTS_PER_GROUP], el)
    v1 = jnp.max(el, axis=0, keepdims=True)
    i1 = jnp.min(jnp.where(el == v1, sub, big), axis=0, keepdims=True)
    el2 = jnp.where(sub == i1, neg, el)
    v2 = jnp.max(el2, axis=0, keepdims=True)
    i2 = jnp.min(jnp.where(el2 == v2, sub, big), axis=0, keepdims=True)
    e2x = jnp.exp(v2 - v1)
    w1 = p_sel / (1.0 + e2x)
    w2 = p_sel * e2x / (1.0 + e2x)
    e1 = g_idx * EXPERTS_PER_GROUP + i1
    e2 = g_idx * EXPERTS_PER_GROUP + i2

    expert = lax.broadcasted_iota(jnp.int32, (N_EXPERTS, ROUTE_ROWS), 0)
    assigned = ((expert == e1) | (expert == e2)).astype(F32)
    carry = cnt_ref[:, 0:1]
    pieces = []
    for j in range(ROUTE_ROWS // SUB):
        piece = assigned[:, j * SUB:(j + 1) * SUB]
        pieces.append(carry + _dot(piece.astype(BF16), later_ref[...]))
        carry = carry + jnp.sum(piece, axis=1, keepdims=True)
    earlier = jnp.concatenate(pieces, axis=1)
    rank1 = jnp.sum(jnp.where(expert == e1, earlier, 0.0), axis=0, keepdims=True).astype(jnp.int32)
    rank2 = jnp.sum(jnp.where(expert == e2, earlier, 0.0), axis=0, keepdims=True).astype(jnp.int32)
    meta_ref[...] = jnp.where(sub == 0, e1, jnp.where(sub == 1, e2, jnp.where(
        sub == 2, rank1, jnp.where(sub == 3, rank2, 0))))
    cnt_ref[...] = jnp.broadcast_to(carry, cnt_ref.shape)
    cnt_out_ref[...] = cnt_ref[...]

    w_rows = jnp.where(sub == 0, w1, jnp.where(sub == 1, w2, 0.0))
    pad_w = jnp.zeros((ROUTER_LANES - EXPERTS_PER_GROUP, ROUTE_ROWS), F32)
    wts_ref[...] = jnp.concatenate([w_rows, pad_w], axis=0).T


def _router(x2d, gain, wr_hi, wr_lo, later):
    n = x2d.shape[0]
    assert n % ROUTE_ROWS == 0
    const2 = lambda i: (0, 0)
    rows = lambda i: (i, 0)
    return pl.pallas_call(
        _router_kernel,
        grid=(n // ROUTE_ROWS,),
        in_specs=[
            pl.BlockSpec((ROUTE_ROWS, D_MODEL), rows),
            pl.BlockSpec((1, D_MODEL), const2),
            pl.BlockSpec((D_MODEL, ROUTER_LANES), const2),
            pl.BlockSpec((D_MODEL, ROUTER_LANES), const2),
            pl.BlockSpec((SUB, SUB), const2),
        ],
        out_specs=[
            pl.BlockSpec((ROUTE_ROWS, PACKED), rows),
            pl.BlockSpec((EXPERTS_PER_GROUP, ROUTE_ROWS), lambda i: (0, i)),
            pl.BlockSpec((ROUTE_ROWS, ROUTER_LANES), rows),
            pl.BlockSpec((N_EXPERTS, ROUTER_LANES), const2),
        ],
        out_shape=[
            jax.ShapeDtypeStruct((n, PACKED), jnp.uint32),
            jax.ShapeDtypeStruct((EXPERTS_PER_GROUP, n), jnp.int32),
            jax.ShapeDtypeStruct((n, ROUTER_LANES), F32),
            jax.ShapeDtypeStruct((N_EXPERTS, ROUTER_LANES), F32),
        ],
        scratch_shapes=[pltpu.VMEM((N_EXPERTS, ROUTER_LANES), F32)],
        compiler_params=pltpu.CompilerParams(
            dimension_semantics=("arbitrary",), vmem_limit_bytes=VMEM_LIMIT_BYTES),
        name="router",
    )(x2d, gain, wr_hi, wr_lo, later)


def _positions_kernel(meta_ref, row_start_ref, pos_ref):
    meta = meta_ref[...]
    expert = lax.broadcasted_iota(jnp.int32, (N_EXPERTS, ROUTE_ROWS), 0)
    starts = row_start_ref[:, 0:1]
    base1 = jnp.sum(jnp.where(expert == meta[0:1], starts, 0), axis=0, keepdims=True)
    base2 = jnp.sum(jnp.where(expert == meta[1:2], starts, 0), axis=0, keepdims=True)
    sub = lax.broadcasted_iota(jnp.int32, meta.shape, 0)
    pos_ref[...] = jnp.where(sub == 0, base1 + meta[2:3], jnp.where(sub == 1, base2 + meta[3:4], 0))


def _positions(meta, row_start_lanes):
    n = meta.shape[1]
    return pl.pallas_call(
        _positions_kernel,
        grid=(n // ROUTE_ROWS,),
        in_specs=[pl.BlockSpec((EXPERTS_PER_GROUP, ROUTE_ROWS), lambda i: (0, i)),
                  pl.BlockSpec((N_EXPERTS, ROUTER_LANES), lambda i: (0, 0))],
        out_specs=pl.BlockSpec((EXPERTS_PER_GROUP, ROUTE_ROWS), lambda i: (0, i)),
        out_shape=jax.ShapeDtypeStruct(meta.shape, jnp.int32),
        compiler_params=pltpu.CompilerParams(
            dimension_semantics=("arbitrary",), vmem_limit_bytes=VMEM_LIMIT_BYTES),
        name="positions",
    )(meta, row_start_lanes)


def _row_copy(src_ref, src_row, dst_ref, dst_row, sem):
    return pltpu.make_async_copy(src_ref.at[pl.ds(src_row, 1)], dst_ref.at[pl.ds(dst_row, 1)], sem)


def _scatter_kernel(pos1_ref, pos2_ref, hpk_ref, sorted_ref, sem):
    def issue(i, carry):
        _row_copy(hpk_ref, i, sorted_ref, pos1_ref[i], sem).start(priority=0)
        _row_copy(hpk_ref, i, sorted_ref, pos2_ref[i], sem).start(priority=1)
        return carry

    lax.fori_loop(0, SCATTER_ROWS, issue, 0, unroll=DMA_UNROLL)

    def drain(i, carry):
        _row_copy(hpk_ref, 0, sorted_ref, 0, sem).wait()
        _row_copy(hpk_ref, 0, sorted_ref, 0, sem).wait()
        return carry

    lax.fori_loop(0, SCATTER_ROWS, drain, 0, unroll=DMA_UNROLL)


def _scatter(pos1, pos2, hpk, n_rows):
    n = hpk.shape[0]
    assert n % SCATTER_ROWS == 0
    idx = lambda i: (i,)
    return pl.pallas_call(
        _scatter_kernel,
        grid=(n // SCATTER_ROWS,),
        in_specs=[
            pl.BlockSpec((SCATTER_ROWS,), idx, memory_space=pltpu.SMEM),
            pl.BlockSpec((SCATTER_ROWS,), idx, memory_space=pltpu.SMEM),
            pl.BlockSpec((SCATTER_ROWS, PACKED), lambda i: (i, 0)),
        ],
        out_specs=pl.BlockSpec(memory_space=pl.ANY),
        out_shape=jax.ShapeDtypeStruct((n_rows, PACKED), jnp.uint32),
        scratch_shapes=[pltpu.SemaphoreType.DMA],
        compiler_params=pltpu.CompilerParams(
            dimension_semantics=("arbitrary",), vmem_limit_bytes=VMEM_LIMIT_BYTES),
        name="scatter_rows",
    )(pos1, pos2, hpk)


def _expert_kernel(tile_expert_ref, tile_rows_ref, n_used_ref, lhs_ref, wg_ref, wu_ref, wd_ref,
                   o_ref, wgu_ref, wdn_ref):
    del n_used_ref
    i = pl.program_id(0)
    n_valid = tile_rows_ref[i]
    new_expert = (i == 0) | (tile_expert_ref[i] != tile_expert_ref[jnp.maximum(i - 1, 0)])

    @pl.when((n_valid > 0) & new_expert)
    def _():
        wgu_ref[...] = jnp.concatenate([wg_ref[0, 0].astype(BF16), wu_ref[0, 0].astype(BF16)], axis=1)
        wdn_ref[...] = wd_ref[0, 0].astype(BF16)

    @pl.when(n_valid > 0)
    def _():
        valid = lax.broadcasted_iota(jnp.int32, (EXPERT_TILE, 1), 0) < n_valid
        lo, hi = _unpack_bf16_pairs(lhs_ref[...])
        hh = jnp.concatenate([jnp.where(valid, lo, 0.0).astype(BF16),
                              jnp.where(valid, hi, 0.0).astype(BF16)], axis=1)
        gu = _dot(hh, wgu_ref[...])
        hid = _silu(gu[:, :D_EXPERT]) * gu[:, D_EXPERT:]
        o_ref[...] = _pack_bf16_pairs(_dot(hid.astype(BF16), wdn_ref[...]))

    @pl.when(n_valid <= 0)
    def _():
        o_ref[...] = jnp.zeros_like(o_ref)


def _experts(tile_expert, tile_rows, n_used, sorted_rows, w_gate, w_up, w_down, layer):
    n_tiles = sorted_rows.shape[0] // EXPERT_TILE
    live = lambda i, nu: jnp.minimum(i, nu[0] - 1)
    expert = lambda i, te, tr, nu: (layer, te[live(i, nu)], 0, 0)
    return pl.pallas_call(
        _expert_kernel,
        grid_spec=pltpu.PrefetchScalarGridSpec(
            num_scalar_prefetch=3,
            grid=(n_tiles,),
            in_specs=[
                pl.BlockSpec((EXPERT_TILE, PACKED), lambda i, te, tr, nu: (live(i, nu), 0)),
                pl.BlockSpec((1, 1, D_MODEL, D_EXPERT), expert),
                pl.BlockSpec((1, 1, D_MODEL, D_EXPERT), expert),
                pl.BlockSpec((1, 1, D_EXPERT, D_MODEL), expert),
            ],
            out_specs=pl.BlockSpec((EXPERT_TILE, PACKED), lambda i, te, tr, nu: (i, 0)),
            scratch_shapes=[
                pltpu.VMEM((D_MODEL, 2 * D_EXPERT), BF16),
                pltpu.VMEM((D_EXPERT, D_MODEL), BF16),
            ],
        ),
        out_shape=jax.ShapeDtypeStruct(sorted_rows.shape, jnp.uint32),
        compiler_params=pltpu.CompilerParams(
            dimension_semantics=("arbitrary",), vmem_limit_bytes=VMEM_LIMIT_BYTES),
        name="experts",
    )(tile_expert, tile_rows, n_used, sorted_rows, w_gate, w_up, w_down)


def _combine_kernel(pos1_ref, pos2_ref, x_ref, wts_ref, ys_ref, gfin_ref, o_ref, buf1_ref, buf2_ref,
                    sem, *, final_norm):
    def issue(i, carry):
        _row_copy(ys_ref, pos1_ref[i], buf1_ref, i, sem).start(priority=0)
        _row_copy(ys_ref, pos2_ref[i], buf2_ref, i, sem).start(priority=1)
        return carry

    lax.fori_loop(0, COMBINE_ROWS, issue, 0, unroll=DMA_UNROLL)

    def drain(i, carry):
        _row_copy(ys_ref, 0, buf1_ref, 0, sem).wait()
        _row_copy(ys_ref, 0, buf2_ref, 0, sem).wait()
        return carry

    lax.fori_loop(0, COMBINE_ROWS, drain, 0, unroll=DMA_UNROLL)
    lo1, hi1 = _unpack_bf16_pairs(buf1_ref[...])
    lo2, hi2 = _unpack_bf16_pairs(buf2_ref[...])
    w1 = wts_ref[:, 0:1]
    w2 = wts_ref[:, 1:2]
    x = x_ref[...]
    y = jnp.concatenate([x[:, :PACKED] + (w1 * lo1 + w2 * lo2),
                         x[:, PACKED:] + (w1 * hi1 + w2 * hi2)], axis=1)
    if final_norm:
        ms = jnp.mean(y * y, axis=-1, keepdims=True)
        y = (y * lax.rsqrt(ms + RMS_EPS)) * gfin_ref[...]
    o_ref[...] = y


def _combine(pos1, pos2, x2d, wts, ys, gain_final, final_norm):
    n = x2d.shape[0]
    assert n % COMBINE_ROWS == 0
    idx = lambda i: (i,)
    rows = lambda i: (i, 0)
    return pl.pallas_call(
        functools.partial(_combine_kernel, final_norm=final_norm),
        grid=(n // COMBINE_ROWS,),
        in_specs=[
            pl.BlockSpec((COMBINE_ROWS,), idx, memory_space=pltpu.SMEM),
            pl.BlockSpec((COMBINE_ROWS,), idx, memory_space=pltpu.SMEM),
            pl.BlockSpec((COMBINE_ROWS, D_MODEL), rows),
            pl.BlockSpec((COMBINE_ROWS, ROUTER_LANES), rows),
            pl.BlockSpec(memory_space=pl.ANY),
            pl.BlockSpec((1, D_MODEL), lambda i: (0, 0)),
        ],
        out_specs=pl.BlockSpec((COMBINE_ROWS, D_MODEL), rows),
        out_shape=jax.ShapeDtypeStruct(x2d.shape, F32),
        scratch_shapes=[
            pltpu.VMEM((COMBINE_ROWS, PACKED), jnp.uint32),
            pltpu.VMEM((COMBINE_ROWS, PACKED), jnp.uint32),
            pltpu.SemaphoreType.DMA,
        ],
        compiler_params=pltpu.CompilerParams(
            dimension_semantics=("arbitrary",), vmem_limit_bytes=VMEM_LIMIT_BYTES),
        name="combine",
    )(pos1, pos2, x2d, wts, ys, gain_final)


def _moe(x2d, gain, wr_hi, wr_lo, later, w_gate, w_up, w_down, layer, gain_final, final_norm):
    n = x2d.shape[0]
    max_tiles = (2 * n) // EXPERT_TILE + N_EXPERTS
    hpk, meta, wts, cnt = _router(x2d, gain, wr_hi, wr_lo, later)
    counts = cnt[:, 0].astype(jnp.int32)
    tiles_per = (counts + (EXPERT_TILE - 1)) // EXPERT_TILE
    tile_end = jnp.cumsum(tiles_per)
    tile_start = tile_end - tiles_per
    row_start = tile_start * EXPERT_TILE
    tile_ids = jnp.arange(max_tiles, dtype=jnp.int32)
    tile_expert = jnp.minimum(
        jnp.sum((tile_ids[:, None] >= tile_end[None, :]).astype(jnp.int32), axis=1), N_EXPERTS - 1)
    n_used = tile_end[-1:].astype(jnp.int32)
    rows_left = counts[tile_expert] - (tile_ids - tile_start[tile_expert]) * EXPERT_TILE
    tile_rows = jnp.where(tile_ids < n_used[0], jnp.clip(rows_left, 0, EXPERT_TILE), 0)
    pos = _positions(meta, jnp.broadcast_to(row_start[:, None], (N_EXPERTS, ROUTER_LANES)))
    sorted_rows = _scatter(pos[0], pos[1], hpk, max_tiles * EXPERT_TILE)
    ys = _experts(tile_expert, tile_rows.astype(jnp.int32), n_used, sorted_rows, w_gate, w_up, w_down,
                  layer)
    return _combine(pos[0], pos[1], x2d, wts, ys, gain_final, final_norm)


def kernel(x, lb_logits, norm_mix, w_in, hgrn_norm, tmlp_ln_g, tmlp_ln_b, w_spatial, b_spatial,
           w_out, norm_ffn, w_router_group, w_router_expert, w_gate, w_up, w_down, norm_final):
    depth = w_in.shape[0]
    bsz, seq, _ = x.shape
    p = jax.nn.softmax(lb_logits.astype(F32), axis=0)
    lower_bounds = jnp.cumsum(p, axis=0) - p[0:1]
    tril = jnp.tril(jnp.ones((SUB, SUB), dtype=bool))
    later = jnp.triu(jnp.ones((SUB, SUB), BF16), 1)
    for layer in range(depth):
        w_sp = jnp.where(tril[None], w_spatial[layer], 0.0).astype(BF16)
        b_sp = jnp.repeat(b_spatial[layer].T, GROUP_DIM, axis=1)
        x = _mixer(x, norm_mix[layer][None], w_in[layer].astype(BF16), lower_bounds[layer][None],
                   hgrn_norm[layer][None], tmlp_ln_g[layer][None], tmlp_ln_b[layer][None],
                   w_sp, b_sp, w_out[layer].astype(BF16))
        w_r = jnp.concatenate([w_router_expert[layer], w_router_group[layer]], axis=1)
        w_r = jnp.pad(w_r, ((0, 0), (0, ROUTER_LANES - w_r.shape[1])))
        wr_hi, wr_lo = _split_bf16(w_r)
        x2d = _moe(x.reshape(bsz * seq, D_MODEL), norm_ffn[layer][None], wr_hi, wr_lo, later,
                   w_gate, w_up, w_down, layer, norm_final[None], final_norm=(layer == depth - 1))
        x = x2d.reshape(bsz, seq, D_MODEL)
    return x
```
